```python
import jax, jax.numpy as jnp
from jax import lax
import numpy as np

D_MODEL = 1024
BATCH = 8
SEQ = 2048
DEPTH = 2
DEC_BATCH = 128
DEC_SEQ = 1
PAST_LEN = 16384
PAGE_SIZE = 128

N_MIXERS = 2
N_LRU = (DEPTH + 1) // 2
N_CMOD = DEPTH // 2
D_RNN = D_MODEL
N_RG_BLOCKS = 4
RG_BLOCK = D_RNN // N_RG_BLOCKS
RG_C = 8.0
LRU_CONV = 4
D_CONV = D_MODEL
CMOD_CONV = 31
D_FF = 3 * D_MODEL
FFN_CONV = 3
N_MEM = 256
XA_HEADS = 4
XA_HEAD_DIM = D_MODEL // XA_HEADS
EPS = 1e-6

kernel_name = "hybrid_rglru_conformer_convffn_step"


def rms_norm(x, g):
    xf = x.astype(jnp.float32)
    y = xf * lax.rsqrt(jnp.mean(xf * xf, axis=-1, keepdims=True) + EPS) * g.astype(jnp.float32)
    return y.astype(x.dtype)


def layer_norm(x, g, b):
    xf = x.astype(jnp.float32)
    mu = jnp.mean(xf, axis=-1, keepdims=True)
    xc = xf - mu
    var = jnp.mean(xc * xc, axis=-1, keepdims=True)
    y = xc * lax.rsqrt(var + EPS) * g.astype(jnp.float32) + b.astype(jnp.float32)
    return y.astype(x.dtype)


def causal_dwconv(x, buf, w, b):
    xp = jnp.concatenate([buf.astype(x.dtype), x], axis=1)
    y = lax.conv_general_dilated(
        xp, w[:, None, :].astype(x.dtype), window_strides=(1,), padding='VALID',
        dimension_numbers=('NWC', 'WIO', 'NWC'), feature_group_count=x.shape[-1])
    return y + b.astype(x.dtype), xp[:, -(w.shape[0] - 1):]


def rg_lru(x, gate_a_pre, gate_x_pre, lam, h0):
    f32 = jnp.float32
    log_a = -RG_C * jax.nn.sigmoid(gate_a_pre.astype(f32)) * jax.nn.softplus(-lam.astype(f32))
    a = jnp.exp(log_a)
    u = jnp.sqrt(-jnp.expm1(2.0 * log_a)) * jax.nn.sigmoid(gate_x_pre.astype(f32)) * x.astype(f32)

    def step(h, au):
        a_t, u_t = au
        h = a_t * h + u_t
        return h, h

    h_last, hs = lax.scan(step, h0.astype(f32), (jnp.swapaxes(a, 0, 1), jnp.swapaxes(u, 0, 1)))
    return jnp.swapaxes(hs, 0, 1).astype(x.dtype), h_last


def recurrent_block(h, j, st_h, st_conv, p):
    bsz, t = h.shape[0], h.shape[1]
    proj = h @ p['lru_w_in'][j]
    gate_branch, rec = proj[..., :D_RNN], proj[..., D_RNN:]
    rec, new_conv = causal_dwconv(rec, st_conv, p['lru_conv_w'][j], p['lru_conv_b'][j])
    rb = rec.reshape(bsz, t, N_RG_BLOCKS, RG_BLOCK)
    ga = jnp.einsum('btnc,ncd->btnd', rb, p['lru_wa'][j]).reshape(bsz, t, D_RNN) + p['lru_ba'][j]
    gx = jnp.einsum('btnc,ncd->btnd', rb, p['lru_wx'][j]).reshape(bsz, t, D_RNN) + p['lru_bx'][j]
    y, h_last = rg_lru(rec, ga, gx, p['lru_lambda'][j], st_h)
    out = (jax.nn.gelu(gate_branch) * y) @ p['lru_w_out'][j]
    return out, h_last, new_conv


def conformer_conv(h, j, st_conv, p):
    ab = h @ p['cm_w_pw1'][j] + p['cm_b_pw1'][j]
    g = ab[..., :D_CONV] * jax.nn.sigmoid(ab[..., D_CONV:])
    c, new_buf = causal_dwconv(g, st_conv, p['cm_dw_w'][j], p['cm_dw_b'][j])
    c = jax.nn.silu(layer_norm(c, p['cm_ln_g'][j], p['cm_ln_b'][j]))
    return c @ p['cm_w_pw2'][j], new_buf


def mem_kv(mem, i, p):
    bsz = mem.shape[0]
    kv = rms_norm(mem, p['mem_norm'][i]) @ p['xa_w_kv'][i]
    k = kv[..., :D_MODEL].reshape(bsz, N_MEM, XA_HEADS, XA_HEAD_DIM)
    v = kv[..., D_MODEL:].reshape(bsz, N_MEM, XA_HEADS, XA_HEAD_DIM)
    return k, v


def cross_attn(h, k, v, i, p):
    bsz, t = h.shape[0], h.shape[1]
    q = (h @ p['xa_w_q'][i]).reshape(bsz, t, XA_HEADS, XA_HEAD_DIM)
    s = jnp.einsum('bthd,bmhd->bhtm', q.astype(jnp.float32), k.astype(jnp.float32)) * (XA_HEAD_DIM ** -0.5)
    pr = jax.nn.softmax(s, axis=-1)
    o = jnp.einsum('bhtm,bmhd->bthd', pr.astype(v.dtype), v).reshape(bsz, t, D_MODEL)
    return o.astype(h.dtype) @ p['xa_w_o'][i]


def conv_ffn(h, i, st_conv, p):
    gu = h @ p['ffn_w_up'][i]
    g, u = gu[..., :D_FF], gu[..., D_FF:]
    g, new_buf = causal_dwconv(g, st_conv, p['ffn_conv_w'][i], p['ffn_conv_b'][i])
    return (jax.nn.gelu(g) * u) @ p['ffn_w_down'][i], new_buf


def trunk(x, lru_h, lru_conv, cmod_conv, ffn_conv, mem_k, mem_v, p):
    new_h, new_lconv, new_cconv, new_fconv = [], [], [], []
    for i in range(DEPTH):
        j = i // N_MIXERS
        hn = rms_norm(x, p['norm_mix'][i])
        if i % N_MIXERS == 0:
            out, h_last, cbuf = recurrent_block(hn, j, lru_h[j], lru_conv[j], p)
            new_h.append(h_last)
            new_lconv.append(cbuf)
        else:
            out, cbuf = conformer_conv(hn, j, cmod_conv[j], p)
            new_cconv.append(cbuf)
        x = x + out
        x = x + cross_attn(rms_norm(x, p['norm_xa'][i]), mem_k[i], mem_v[i], i, p)
        out, fbuf = conv_ffn(rms_norm(x, p['norm_ffn'][i]), i, ffn_conv[i], p)
        x = x + out
        new_fconv.append(fbuf)
    y = rms_norm(x, p['norm_final'])
    return y, jnp.stack(new_h), jnp.stack(new_lconv), jnp.stack(new_cconv), jnp.stack(new_fconv)


def setup_inputs(seed: int = 0) -> dict:
    key = jax.random.key(seed)
    ks = iter(jax.random.split(key, 64))
    f32 = jnp.float32

    def nrm(shape, scale):
        return jax.random.normal(next(ks), shape, f32) * scale

    def gain(shape):
        return 1.0 + nrm(shape, 0.01)

    a0 = jax.random.uniform(next(ks), (N_LRU, D_RNN), f32, 0.9, 0.999)
    a_base = a0 ** (1.0 / RG_C)
    lru_lambda = jnp.log(a_base) - jnp.log1p(-a_base)
    return {
        'x_prompt': nrm((BATCH, SEQ, D_MODEL), 1.0),
        'x_sample': nrm((DEC_BATCH, DEC_SEQ, D_MODEL), 1.0),
        'state_lru_h': nrm((N_LRU, DEC_BATCH, D_RNN), 0.5),
        'state_lru_conv': nrm((N_LRU, DEC_BATCH, LRU_CONV - 1, D_RNN), 1.0),
        'state_cmod_conv': nrm((N_CMOD, DEC_BATCH, CMOD_CONV - 1, D_CONV), 1.0),
        'state_ffn_conv': nrm((DEPTH, DEC_BATCH, FFN_CONV - 1, D_FF), 1.0),
        'cache_mem_k': nrm((DEPTH, DEC_BATCH, N_MEM, XA_HEADS, XA_HEAD_DIM), 1.0),
        'cache_mem_v': nrm((DEPTH, DEC_BATCH, N_MEM, XA_HEADS, XA_HEAD_DIM), 1.0),
        'mem_prompt': nrm((BATCH, N_MEM, D_MODEL), 1.0),
        'norm_mix': gain((DEPTH, D_MODEL)),
        'norm_xa': gain((DEPTH, D_MODEL)),
        'norm_ffn': gain((DEPTH, D_MODEL)),
        'norm_final': gain((D_MODEL,)),
        'lru_w_in': nrm((N_LRU, D_MODEL, 2 * D_RNN), D_MODEL ** -0.5),
        'lru_conv_w': nrm((N_LRU, LRU_CONV, D_RNN), LRU_CONV ** -0.5),
        'lru_conv_b': nrm((N_LRU, D_RNN), 0.01),
        'lru_wa': nrm((N_LRU, N_RG_BLOCKS, RG_BLOCK, RG_BLOCK), RG_BLOCK ** -0.5),
        'lru_ba': nrm((N_LRU, D_RNN), 0.01),
        'lru_wx': nrm((N_LRU, N_RG_BLOCKS, RG_BLOCK, RG_BLOCK), RG_BLOCK ** -0.5),
        'lru_bx': nrm((N_LRU, D_RNN), 0.01),
        'lru_lambda': lru_lambda,
        'lru_w_out': nrm((N_LRU, D_RNN, D_MODEL), D_RNN ** -0.5),
        'cm_w_pw1': nrm((N_CMOD, D_MODEL, 2 * D_CONV), D_MODEL ** -0.5),
        'cm_b_pw1': nrm((N_CMOD, 2 * D_CONV), 0.01),
        'cm_dw_w': nrm((N_CMOD, CMOD_CONV, D_CONV), CMOD_CONV ** -0.5),
        'cm_dw_b': nrm((N_CMOD, D_CONV), 0.01),
        'cm_ln_g': gain((N_CMOD, D_CONV)),
        'cm_ln_b': nrm((N_CMOD, D_CONV), 0.01),
        'cm_w_pw2': nrm((N_CMOD, D_CONV, D_MODEL), D_CONV ** -0.5),
        'mem_norm': gain((DEPTH, D_MODEL)),
        'xa_w_q': nrm((DEPTH, D_MODEL, D_MODEL), D_MODEL ** -0.5),
        'xa_w_kv': nrm((DEPTH, D_MODEL, 2 * D_MODEL), D_MODEL ** -0.5),
        'xa_w_o': nrm((DEPTH, D_MODEL, D_MODEL), D_MODEL ** -0.5),
        'ffn_w_up': nrm((DEPTH, D_MODEL, 2 * D_FF), D_MODEL ** -0.5),
        'ffn_conv_w': nrm((DEPTH, FFN_CONV, D_FF), FFN_CONV ** -0.5),
        'ffn_conv_b': nrm((DEPTH, D_FF), 0.01),
        'ffn_w_down': nrm((DEPTH, D_FF, D_MODEL), D_FF ** -0.5),
    }


def reference(x_prompt, x_sample, state_lru_h, state_lru_conv, state_cmod_conv, state_ffn_conv,
              cache_mem_k, cache_mem_v, mem_prompt,
              norm_mix, norm_xa, norm_ffn, norm_final,
              lru_w_in, lru_conv_w, lru_conv_b, lru_wa, lru_ba, lru_wx, lru_bx, lru_lambda, lru_w_out,
              cm_w_pw1, cm_b_pw1, cm_dw_w, cm_dw_b, cm_ln_g, cm_ln_b, cm_w_pw2,
              mem_norm, xa_w_q, xa_w_kv, xa_w_o,
              ffn_w_up, ffn_conv_w, ffn_conv_b, ffn_w_down):
    p = dict(norm_mix=norm_mix, norm_xa=norm_xa, norm_ffn=norm_ffn, norm_final=norm_final,
             lru_w_in=lru_w_in, lru_conv_w=lru_conv_w, lru_conv_b=lru_conv_b, lru_wa=lru_wa,
             lru_ba=lru_ba, lru_wx=lru_wx, lru_bx=lru_bx, lru_lambda=lru_lambda, lru_w_out=lru_w_out,
             cm_w_pw1=cm_w_pw1, cm_b_pw1=cm_b_pw1, cm_dw_w=cm_dw_w, cm_dw_b=cm_dw_b,
             cm_ln_g=cm_ln_g, cm_ln_b=cm_ln_b, cm_w_pw2=cm_w_pw2,
             mem_norm=mem_norm, xa_w_q=xa_w_q, xa_w_kv=xa_w_kv, xa_w_o=xa_w_o,
             ffn_w_up=ffn_w_up, ffn_conv_w=ffn_conv_w, ffn_conv_b=ffn_conv_b, ffn_w_down=ffn_w_down)

    bsz = x_prompt.shape[0]
    dt = x_prompt.dtype
    kvs = [mem_kv(mem_prompt, i, p) for i in range(DEPTH)]
    mem_k_prompt = jnp.stack([kv[0] for kv in kvs])
    mem_v_prompt = jnp.stack([kv[1] for kv in kvs])
    y_prompt, new_lru_h_prompt, new_lru_conv_prompt, new_cmod_conv_prompt, new_ffn_conv_prompt = trunk(
        x_prompt,
        jnp.zeros((N_LRU, bsz, D_RNN), jnp.float32),
        jnp.zeros((N_LRU, bsz, LRU_CONV - 1, D_RNN), dt),
        jnp.zeros((N_CMOD, bsz, CMOD_CONV - 1, D_CONV), dt),
        jnp.zeros((DEPTH, bsz, FFN_CONV - 1, D_FF), dt),
        mem_k_prompt, mem_v_prompt, p)

    y_sample, new_lru_h_sample, new_lru_conv_sample, new_cmod_conv_sample, new_ffn_conv_sample = trunk(
        x_sample, state_lru_h, state_lru_conv, state_cmod_conv, state_ffn_conv,
        cache_mem_k, cache_mem_v, p)

    return (y_prompt, y_sample,
            new_lru_h_prompt, new_lru_conv_prompt, new_cmod_conv_prompt, new_ffn_conv_prompt,
            mem_k_prompt, mem_v_prompt,
            new_lru_h_sample, new_lru_conv_sample, new_cmod_conv_sample, new_ffn_conv_sample)
```

```python
import functools

import jax
import jax.numpy as jnp
from jax import lax
from jax.experimental import pallas as pl
from jax.experimental.pallas import tpu as pltpu

F32 = jnp.float32
BF16 = jnp.bfloat16

EPS = 1e-6
RG_C = 8.0
N_RG_BLOCKS = 4
XA_HEADS = 4
SUBLANES = 8
VMEM_LIMIT = 56 << 20

PROMPT_T_TILE = 256
FFN_COL_TILE = 512
SAMPLE_ATTN_BLOCK = 4
SAMPLE_CONV_ROWS = 16


def _rms(x, g):
    return x * lax.rsqrt(jnp.mean(x * x, axis=-1, keepdims=True) + EPS) * g


def _layer_norm(x, g, b):
    mu = jnp.mean(x, axis=-1, keepdims=True)
    xc = x - mu
    var = jnp.mean(xc * xc, axis=-1, keepdims=True)
    return xc * lax.rsqrt(var + EPS) * g + b


def _mm(a, w):
    return jnp.dot(a.astype(BF16), w, preferred_element_type=F32)


def _softplus(z):
    return jnp.maximum(z, 0.0) + jnp.log1p(jnp.exp(-jnp.abs(z)))


def _block_diag(xb, w_ref):
    blk = xb.shape[-1] // N_RG_BLOCKS
    return jnp.concatenate(
        [jnp.dot(xb[:, n * blk:(n + 1) * blk], w_ref[n], preferred_element_type=F32)
         for n in range(N_RG_BLOCKS)], axis=-1)


def _lru_gates(rec, wa_ref, ba, wx_ref, bx, lam):
    rb = rec.astype(BF16)
    ga = _block_diag(rb, wa_ref) + ba
    gx = _block_diag(rb, wx_ref) + bx
    log_a = -RG_C * jax.nn.sigmoid(ga) * _softplus(-lam)
    a = jnp.exp(log_a)
    u = jnp.sqrt(1.0 - a * a) * jax.nn.sigmoid(gx) * rec
    return a, u


def _glu(ab, d):
    return ab[:, :d] * jax.nn.sigmoid(ab[:, d:])


def _p_memkv_kernel(m_ref, g_ref, wkv_ref, k_ref, v_ref, kt_ref, vb_ref):
    d = m_ref.shape[-1]
    hn = _rms(m_ref[...], g_ref[...]).astype(BF16)
    k = jnp.dot(hn, wkv_ref[:, :d], preferred_element_type=F32)
    v = jnp.dot(hn, wkv_ref[:, d:], preferred_element_type=F32)
    k_ref[...] = k
    v_ref[...] = v
    kt_ref[...] = k.T.astype(BF16)
    vb_ref[...] = v.astype(BF16)


def _p_lru_kernel(x_ref, nm_ref, win_ref, cw_ref, cb_ref, wa_ref, ba_ref, wx_ref, bx_ref,
                  lam_ref, wout_ref,
                  xo_ref, ho_ref, co_ref,
                  cbuf, a_s, u_s, y_s, h_s, *, layer, j):
    tt, d = x_ref.shape
    kw = cw_ref.shape[0]
    hist = kw - 1
    tc = pl.program_id(1)

    @pl.when(tc == 0)
    def _():
        cbuf[0:SUBLANES, :] = jnp.zeros((SUBLANES, d), F32)
        h_s[...] = jnp.zeros_like(h_s)

    x = x_ref[...]
    hn = _rms(x, nm_ref[layer:layer + 1, :]).astype(BF16)
    gate = jnp.dot(hn, win_ref[:, :d], preferred_element_type=F32)
    rec = jnp.dot(hn, win_ref[:, d:], preferred_element_type=F32)

    cbuf[SUBLANES:SUBLANES + tt, :] = rec
    conv = cb_ref[j:j + 1, :] + cw_ref[hist:kw, :] * rec
    for k in range(hist):
        off = SUBLANES - hist + k
        conv = conv + cw_ref[k:k + 1, :] * cbuf[off:off + tt, :]
    tail = rec[tt - hist:, :]
    co_ref[...] = tail
    cbuf[SUBLANES - hist:SUBLANES, :] = tail

    a, u = _lru_gates(conv, wa_ref, ba_ref[j:j + 1, :], wx_ref, bx_ref[j:j + 1, :],
                      lam_ref[j:j + 1, :])
    a_s[...] = a
    u_s[...] = u

    row = lax.broadcasted_iota(jnp.int32, (SUBLANES, d), 0)

    def group(g, h):
        r0 = pl.multiple_of(g * SUBLANES, SUBLANES)
        ag = a_s[pl.ds(r0, SUBLANES), :]
        ug = u_s[pl.ds(r0, SUBLANES), :]
        for k in (1, 2, 4):
            keep = row >= k
            a_sh = pltpu.roll(ag, k, 0)
            u_sh = pltpu.roll(ug, k, 0)
            ug = ug + ag * jnp.where(keep, u_sh, 0.0)
            ag = ag * jnp.where(keep, a_sh, 1.0)
        hg = ug + ag * h
        y_s[pl.ds(r0, SUBLANES), :] = hg
        return hg[SUBLANES - 1:SUBLANES, :]

    h = lax.fori_loop(0, tt // SUBLANES, group, h_s[...])
    h_s[...] = h
    ho_ref[...] = h

    out = _mm(jax.nn.gelu(gate) * y_s[...], wout_ref[...])
    xo_ref[...] = x + out


def _p_cmod_kernel(x_ref, nm_ref, pw1_ref, b1_ref, dw_ref, dwb_ref, lng_ref, lnb_ref, pw2_ref,
                   xo_ref, so_ref, gbuf, *, layer, j, pad):
    tt, d = x_ref.shape
    kw = dw_ref.shape[0]
    hist = kw - 1
    tc = pl.program_id(1)

    @pl.when(tc == 0)
    def _():
        gbuf[0:pad, :] = jnp.zeros((pad, d), F32)

    x = x_ref[...]
    hn = _rms(x, nm_ref[layer:layer + 1, :]).astype(BF16)
    ab = jnp.dot(hn, pw1_ref[...], preferred_element_type=F32) + b1_ref[j:j + 1, :]
    g = _glu(ab, d)

    gbuf[pad:pad + tt, :] = g
    c = dwb_ref[j:j + 1, :] + dw_ref[hist:kw, :] * g
    for k in range(hist):
        off = pad - hist + k
        c = c + dw_ref[k:k + 1, :] * gbuf[off:off + tt, :]
    tail = gbuf[pad + tt - hist:pad + tt, :]
    so_ref[...] = tail
    gbuf[pad - hist:pad, :] = tail

    c = jax.nn.silu(_layer_norm(c, lng_ref[j:j + 1, :], lnb_ref[j:j + 1, :]))
    xo_ref[...] = x + _mm(c, pw2_ref[...])


def _p_xattn_kernel(x_ref, nm_ref, wq_ref, kt_ref, v_ref, wo_ref, xo_ref, *, layer):
    tt, d = x_ref.shape
    dh = d // XA_HEADS
    x = x_ref[...]
    q = _mm(_rms(x, nm_ref[layer:layer + 1, :]), wq_ref[...]).astype(BF16)
    heads = []
    for h in range(XA_HEADS):
        cs = slice(h * dh, (h + 1) * dh)
        s = jnp.dot(q[:, cs], kt_ref[cs, :], preferred_element_type=F32) * (dh ** -0.5)
        e = jnp.exp(s - jnp.max(s, axis=-1, keepdims=True))
        p = e / jnp.sum(e, axis=-1, keepdims=True)
        heads.append(jnp.dot(p.astype(BF16), v_ref[:, cs], preferred_element_type=F32))
    o = jnp.concatenate(heads, axis=-1)
    xo_ref[...] = x + _mm(o, wo_ref[...])


def _p_ffn_kernel(x_ref, nm_ref, wup_ref, cw_ref, cb_ref, wdn_ref, nf_ref,
                  xo_ref, so_ref, buf, *, layer, final):
    tt, d = x_ref.shape
    dff = wdn_ref.shape[0]
    kw = cw_ref.shape[0]
    hist = kw - 1
    tc = pl.program_id(1)

    @pl.when(tc == 0)
    def _():
        buf[0:SUBLANES, :] = jnp.zeros((SUBLANES, dff), F32)

    x = x_ref[...]
    hn = _rms(x, nm_ref[layer:layer + 1, :]).astype(BF16)
    acc = jnp.zeros((tt, d), F32)
    for c in range(dff // FFN_COL_TILE):
        cs = slice(c * FFN_COL_TILE, (c + 1) * FFN_COL_TILE)
        us = slice(dff + c * FFN_COL_TILE, dff + (c + 1) * FFN_COL_TILE)
        g = jnp.dot(hn, wup_ref[:, cs], preferred_element_type=F32)
        u = jnp.dot(hn, wup_ref[:, us], preferred_element_type=F32)
        buf[SUBLANES:SUBLANES + tt, cs] = g
        conv = cb_ref[layer:layer + 1, cs] + cw_ref[hist:kw, cs] * g
        for k in range(hist):
            off = SUBLANES - hist + k
            conv = conv + cw_ref[k:k + 1, cs] * buf[off:off + tt, cs]
        act = (jax.nn.gelu(conv) * u).astype(BF16)
        acc = acc + jnp.dot(act, wdn_ref[cs, :], preferred_element_type=F32)
    tail = buf[SUBLANES + tt - hist:SUBLANES + tt, :]
    so_ref[...] = tail
    buf[SUBLANES - hist:SUBLANES, :] = tail

    xn = x + acc
    if final:
        xn = _rms(xn, nf_ref[...])
    xo_ref[...] = xn


def _s_lru_kernel(x_ref, h0_ref, cst_ref, nm_ref, win_ref, cw_ref, cb_ref, wa_ref, ba_ref,
                  wx_ref, bx_ref, lam_ref, wout_ref, nxa_ref, wq_ref,
                  xo_ref, q_ref, ho_ref, co_ref, *, layer, j):
    d = x_ref.shape[-1]
    kw = cw_ref.shape[0]
    hist = kw - 1
    x = x_ref[...]
    hn = _rms(x, nm_ref[layer:layer + 1, :]).astype(BF16)
    gate = jnp.dot(hn, win_ref[:, :d], preferred_element_type=F32)
    rec = jnp.dot(hn, win_ref[:, d:], preferred_element_type=F32)

    conv = cb_ref[j:j + 1, :] + cw_ref[hist:kw, :] * rec
    for k in range(hist):
        conv = conv + cw_ref[k:k + 1, :] * cst_ref[:, k * d:(k + 1) * d]
    co_ref[:, :(hist - 1) * d] = cst_ref[:, d:]
    co_ref[:, (hist - 1) * d:] = rec

    a, u = _lru_gates(conv, wa_ref, ba_ref[j:j + 1, :], wx_ref, bx_ref[j:j + 1, :],
                      lam_ref[j:j + 1, :])
    h = a * h0_ref[...] + u
    ho_ref[...] = h
    x1 = x + _mm(jax.nn.gelu(gate) * h, wout_ref[...])
    xo_ref[...] = x1
    q_ref[...] = _mm(_rms(x1, nxa_ref[layer:layer + 1, :]), wq_ref[...])


def _s_attn_kernel(q_ref, k_ref, v_ref, o_ref):
    nb, _, d = q_ref.shape
    dh = d // XA_HEADS
    for b in range(nb):
        prod = k_ref[b] * q_ref[b]
        heads = []
        for h in range(XA_HEADS):
            cs = slice(h * dh, (h + 1) * dh)
            s = jnp.sum(prod[:, cs], axis=-1, keepdims=True) * (dh ** -0.5)
            e = jnp.exp(s - jnp.max(s, axis=0, keepdims=True))
            p = e / jnp.sum(e, axis=0, keepdims=True)
            heads.append(jnp.sum(p * v_ref[b, :, cs], axis=0, keepdims=True))
        o_ref[b] = jnp.concatenate(heads, axis=-1)


def _s_ffn_kernel(x_ref, o_ref, wo_ref, nm_ref, wup_ref, cw_ref, cb_ref, wdn_ref, st_ref, nf_ref,
                  xo_ref, so_ref, *, layer, final):
    dff = wdn_ref.shape[0]
    kw = cw_ref.shape[0]
    hist = kw - 1
    x = x_ref[...] + _mm(o_ref[...], wo_ref[...])
    hn = _rms(x, nm_ref[layer:layer + 1, :]).astype(BF16)
    g = jnp.dot(hn, wup_ref[:, :dff], preferred_element_type=F32)
    u = jnp.dot(hn, wup_ref[:, dff:], preferred_element_type=F32)
    conv = cb_ref[layer:layer + 1, :] + cw_ref[hist:kw, :] * g
    for k in range(hist):
        conv = conv + cw_ref[k:k + 1, :] * st_ref[:, k * dff:(k + 1) * dff]
    so_ref[:, :(hist - 1) * dff] = st_ref[:, dff:]
    so_ref[:, (hist - 1) * dff:] = g
    xn = x + _mm(jax.nn.gelu(conv) * u, wdn_ref[...])
    if final:
        xn = _rms(xn, nf_ref[...])
    xo_ref[...] = xn


def _s_cmod_pre_kernel(x_ref, nm_ref, pw1_ref, b1_ref, g_ref, *, layer, j):
    d = x_ref.shape[-1]
    hn = _rms(x_ref[...], nm_ref[layer:layer + 1, :])
    g_ref[...] = _glu(_mm(hn, pw1_ref[...]) + b1_ref[j:j + 1, :], d)


def _s_cmod_conv_kernel(st_ref, g_ref, dw_ref, dwb_ref, lng_ref, lnb_ref, c_ref, so_ref, *, j):
    d = g_ref.shape[-1]
    kw = dw_ref.shape[0]
    hist = kw - 1
    g = g_ref[...]
    c = dwb_ref[j:j + 1, :] + dw_ref[hist:kw, :] * g
    for k in range(hist):
        c = c + dw_ref[k:k + 1, :] * st_ref[:, k * d:(k + 1) * d]
    so_ref[:, :(hist - 1) * d] = st_ref[:, d:]
    so_ref[:, (hist - 1) * d:] = g
    c_ref[...] = jax.nn.silu(_layer_norm(c, lng_ref[j:j + 1, :], lnb_ref[j:j + 1, :]))


def _s_cmod_post_kernel(x_ref, c_ref, pw2_ref, nxa_ref, wq_ref, xo_ref, q_ref, *, layer):
    x1 = x_ref[...] + _mm(c_ref[...], pw2_ref[...])
    xo_ref[...] = x1
    q_ref[...] = _mm(_rms(x1, nxa_ref[layer:layer + 1, :]), wq_ref[...])


def _params(n_grid):
    return pltpu.CompilerParams(dimension_semantics=("arbitrary",) * n_grid,
                                vmem_limit_bytes=VMEM_LIMIT)


def _resident(shape, lead=None):
    if lead is None:
        nd = len(shape)
        return pl.BlockSpec(shape, lambda *_: (0,) * nd, pipeline_mode=pl.Buffered(1))
    nd = len(shape) - 1
    return pl.BlockSpec((None,) + tuple(shape[1:]), lambda *_: (lead,) + (0,) * nd,
                        pipeline_mode=pl.Buffered(1))


_whole = _resident


def _whole_out(shape):
    nd = len(shape)
    return pl.BlockSpec(shape, lambda *_: (0,) * nd)


def _sds(shape, dtype=F32):
    return jax.ShapeDtypeStruct(shape, dtype)


def _prompt_memkv(mem, mem_norm3, wkv):
    depth = wkv.shape[0]
    b, m, d = mem.shape
    return pl.pallas_call(
        _p_memkv_kernel,
        grid=(depth, b),
        in_specs=[pl.BlockSpec((None, m, d), lambda i, n: (n, 0, 0)),
                  pl.BlockSpec((None, 1, d), lambda i, n: (i, 0, 0)),
                  pl.BlockSpec((None, d, 2 * d), lambda i, n: (i, 0, 0))],
        out_specs=[pl.BlockSpec((None, None, m, d), lambda i, n: (i, n, 0, 0)),
                   pl.BlockSpec((None, None, m, d), lambda i, n: (i, n, 0, 0)),
                   pl.BlockSpec((None, None, d, m), lambda i, n: (i, n, 0, 0)),
                   pl.BlockSpec((None, None, m, d), lambda i, n: (i, n, 0, 0))],
        out_shape=[_sds((depth, b, m, d)), _sds((depth, b, m, d)),
                   _sds((depth, b, d, m), BF16), _sds((depth, b, m, d), BF16)],
        compiler_params=_params(2),
        name="prompt_memkv",
    )(mem, mem_norm3, wkv)


def _x_spec(tt, d):
    return pl.BlockSpec((None, tt, d), lambda n, t: (n, t, 0))


def _state_spec(rows, cols):
    return pl.BlockSpec((None, rows, cols), lambda n, t: (n, 0, 0))


def _prompt_lru(x, layer, j, w):
    b, t, d = x.shape
    tt = PROMPT_T_TILE
    hist = w['lru_conv_w'].shape[1] - 1
    ins = [(x, _x_spec(tt, d)),
           (w['norm_mix'], _resident(w['norm_mix'].shape)),
           (w['lru_w_in'], _resident(w['lru_w_in'].shape, j)),
           (w['lru_conv_w'], _resident(w['lru_conv_w'].shape, j)),
           (w['lru_conv_b'], _resident(w['lru_conv_b'].shape)),
           (w['lru_wa'], _resident(w['lru_wa'].shape, j)),
           (w['lru_ba'], _resident(w['lru_ba'].shape)),
           (w['lru_wx'], _resident(w['lru_wx'].shape, j)),
           (w['lru_bx'], _resident(w['lru_bx'].shape)),
           (w['lru_lambda'], _resident(w['lru_lambda'].shape)),
           (w['lru_w_out'], _resident(w['lru_w_out'].shape, j))]
    return pl.pallas_call(
        functools.partial(_p_lru_kernel, layer=layer, j=j),
        grid=(b, t // tt),
        in_specs=[s for _, s in ins],
        out_specs=[_x_spec(tt, d), _state_spec(1, d), _state_spec(hist, d)],
        out_shape=[_sds((b, t, d)), _sds((b, 1, d)), _sds((b, hist, d))],
        scratch_shapes=[pltpu.VMEM((SUBLANES + tt, d), F32), pltpu.VMEM((tt, d), F32),
                        pltpu.VMEM((tt, d), F32), pltpu.VMEM((tt, d), F32),
                        pltpu.VMEM((1, d), F32)],
        compiler_params=_params(2),
        name="prompt_lru",
    )(*[a for a, _ in ins])


def _prompt_cmod(x, layer, j, w):
    b, t, d = x.shape
    tt = PROMPT_T_TILE
    hist = w['cm_dw_w'].shape[1] - 1
    pad = -(-hist // SUBLANES) * SUBLANES
    ins = [(x, _x_spec(tt, d)),
           (w['norm_mix'], _resident(w['norm_mix'].shape)),
           (w['cm_w_pw1'], _resident(w['cm_w_pw1'].shape, j)),
           (w['cm_b_pw1'], _resident(w['cm_b_pw1'].shape)),
           (w['cm_dw_w'], _resident(w['cm_dw_w'].shape, j)),
           (w['cm_dw_b'], _resident(w['cm_dw_b'].shape)),
           (w['cm_ln_g'], _resident(w['cm_ln_g'].shape)),
           (w['cm_ln_b'], _resident(w['cm_ln_b'].shape)),
           (w['cm_w_pw2'], _resident(w['cm_w_pw2'].shape, j))]
    return pl.pallas_call(
        functools.partial(_p_cmod_kernel, layer=layer, j=j, pad=pad),
        grid=(b, t // tt),
        in_specs=[s for _, s in ins],
        out_specs=[_x_spec(tt, d), _state_spec(hist, d)],
        out_shape=[_sds((b, t, d)), _sds((b, hist, d))],
        scratch_shapes=[pltpu.VMEM((pad + tt, d), F32)],
        compiler_params=_params(2),
        name="prompt_cmod",
    )(*[a for a, _ in ins])


def _prompt_xattn(x, layer, kt, vb, w):
    b, t, d = x.shape
    tt = PROMPT_T_TILE
    m = vb.shape[2]
    ins = [(x, _x_spec(tt, d)),
           (w['norm_xa'], _resident(w['norm_xa'].shape)),
           (w['xa_w_q'], _resident(w['xa_w_q'].shape, layer)),
           (kt, pl.BlockSpec((None, None, d, m), lambda n, t_: (layer, n, 0, 0))),
           (vb, pl.BlockSpec((None, None, m, d), lambda n, t_: (layer, n, 0, 0))),
           (w['xa_w_o'], _resident(w['xa_w_o'].shape, layer))]
    return pl.pallas_call(
        functools.partial(_p_xattn_kernel, layer=layer),
        grid=(b, t // tt),
        in_specs=[s for _, s in ins],
        out_specs=_x_spec(tt, d),
        out_shape=_sds((b, t, d)),
        compiler_params=_params(2),
        name="prompt_xattn",
    )(*[a for a, _ in ins])


def _prompt_ffn(x, layer, final, w):
    b, t, d = x.shape
    tt = PROMPT_T_TILE
    dff = w['ffn_w_down'].shape[1]
    hist = w['ffn_conv_w'].shape[1] - 1
    ins = [(x, _x_spec(tt, d)),
           (w['norm_ffn'], _resident(w['norm_ffn'].shape)),
           (w['ffn_w_up'], _resident(w['ffn_w_up'].shape, layer)),
           (w['ffn_conv_w'], _resident(w['ffn_conv_w'].shape, layer)),
           (w['ffn_conv_b'], _resident(w['ffn_conv_b'].shape)),
           (w['ffn_w_down'], _resident(w['ffn_w_down'].shape, layer)),
           (w['norm_final'], _resident(w['norm_final'].shape))]
    return pl.pallas_call(
        functools.partial(_p_ffn_kernel, layer=layer, final=final),
        grid=(b, t // tt),
        in_specs=[s for _, s in ins],
        out_specs=[_x_spec(tt, d), _state_spec(hist, dff)],
        out_shape=[_sds((b, t, d)), _sds((b, hist, dff))],
        scratch_shapes=[pltpu.VMEM((SUBLANES + tt, dff), F32)],
        compiler_params=_params(2),
        name="prompt_ffn",
    )(*[a for a, _ in ins])


def _sample_lru(x, h0, cst, layer, j, w):
    r, d = x.shape
    ins = [(x, _whole(x.shape)), (h0, _whole(h0.shape, j)), (cst, _whole(cst.shape, j)),
           (w['norm_mix'], _whole(w['norm_mix'].shape)),
           (w['lru_w_in'], _whole(w['lru_w_in'].shape, j)),
           (w['lru_conv_w'], _whole(w['lru_conv_w'].shape, j)),
           (w['lru_conv_b'], _whole(w['lru_conv_b'].shape)),
           (w['lru_wa'], _whole(w['lru_wa'].shape, j)),
           (w['lru_ba'], _whole(w['lru_ba'].shape)),
           (w['lru_wx'], _whole(w['lru_wx'].shape, j)),
           (w['lru_bx'], _whole(w['lru_bx'].shape)),
           (w['lru_lambda'], _whole(w['lru_lambda'].shape)),
           (w['lru_w_out'], _whole(w['lru_w_out'].shape, j)),
           (w['norm_xa'], _whole(w['norm_xa'].shape)),
           (w['xa_w_q'], _whole(w['xa_w_q'].shape, layer))]
    return pl.pallas_call(
        functools.partial(_s_lru_kernel, layer=layer, j=j),
        in_specs=[s for _, s in ins],
        out_specs=[_whole_out((r, d)), _whole_out((r, d)), _whole_out((r, d)),
                   _whole_out(cst.shape[1:])],
        out_shape=[_sds((r, d)), _sds((r, d)), _sds((r, d)), _sds(cst.shape[1:])],
        grid=(1,),
        compiler_params=_params(1),
        name="sample_lru",
    )(*[a for a, _ in ins])


def _sample_attn(q, k, v, layer):
    r, d = q.shape
    m = k.shape[2]
    nb = SAMPLE_ATTN_BLOCK
    q3 = q.reshape(r, 1, d)
    kv_spec = pl.BlockSpec((None, nb, m, d), lambda n: (layer, n, 0, 0))
    row_spec = pl.BlockSpec((nb, 1, d), lambda n: (n, 0, 0))
    o = pl.pallas_call(
        _s_attn_kernel,
        grid=(r // nb,),
        in_specs=[row_spec, kv_spec, kv_spec],
        out_specs=row_spec,
        out_shape=_sds((r, 1, d)),
        compiler_params=_params(1),
        name="sample_attn",
    )(q3, k, v)
    return o.reshape(r, d)


def _sample_ffn(x, o, st, layer, final, w):
    r, d = x.shape
    ins = [(x, _whole(x.shape)), (o, _whole(o.shape)),
           (w['xa_w_o'], _whole(w['xa_w_o'].shape, layer)),
           (w['norm_ffn'], _whole(w['norm_ffn'].shape)),
           (w['ffn_w_up'], _whole(w['ffn_w_up'].shape, layer)),
           (w['ffn_conv_w'], _whole(w['ffn_conv_w'].shape, layer)),
           (w['ffn_conv_b'], _whole(w['ffn_conv_b'].shape)),
           (w['ffn_w_down'], _whole(w['ffn_w_down'].shape, layer)),
           (st, _whole(st.shape, layer)),
           (w['norm_final'], _whole(w['norm_final'].shape))]
    return pl.pallas_call(
        functools.partial(_s_ffn_kernel, layer=layer, final=final),
        in_specs=[s for _, s in ins],
        out_specs=[_whole_out((r, d)), _whole_out(st.shape[1:])],
        out_shape=[_sds((r, d)), _sds(st.shape[1:])],
        grid=(1,),
        compiler_params=_params(1),
        name="sample_ffn",
    )(*[a for a, _ in ins])


def _sample_cmod(x, st, layer, j, w):
    r, d = x.shape
    ins = [(x, _whole(x.shape)),
           (w['norm_mix'], _whole(w['norm_mix'].shape)),
           (w['cm_w_pw1'], _whole(w['cm_w_pw1'].shape, j)),
           (w['cm_b_pw1'], _whole(w['cm_b_pw1'].shape))]
    g = pl.pallas_call(
        functools.partial(_s_cmod_pre_kernel, layer=layer, j=j),
        in_specs=[s for _, s in ins],
        out_specs=_whole_out((r, d)),
        out_shape=_sds((r, d)),
        grid=(1,),
        compiler_params=_params(1),
        name="sample_cmod_pre",
    )(*[a for a, _ in ins])

    rows = SAMPLE_CONV_ROWS
    sc = st.shape[-1]
    ins = [(st, pl.BlockSpec((None, rows, sc), lambda n: (j, n, 0))),
           (g, pl.BlockSpec((rows, d), lambda n: (n, 0))),
           (w['cm_dw_w'], _resident(w['cm_dw_w'].shape, j)),
           (w['cm_dw_b'], _resident(w['cm_dw_b'].shape)),
           (w['cm_ln_g'], _resident(w['cm_ln_g'].shape)),
           (w['cm_ln_b'], _resident(w['cm_ln_b'].shape))]
    c, new_st = pl.pallas_call(
        functools.partial(_s_cmod_conv_kernel, j=j),
        grid=(r // rows,),
        in_specs=[s for _, s in ins],
        out_specs=[pl.BlockSpec((rows, d), lambda n: (n, 0)),
                   pl.BlockSpec((rows, sc), lambda n: (n, 0))],
        out_shape=[_sds((r, d)), _sds((r, sc))],
        compiler_params=_params(1),
        name="sample_cmod_conv",
    )(*[a for a, _ in ins])

    ins = [(x, _whole(x.shape)), (c, _whole(c.shape)),
           (w['cm_w_pw2'], _whole(w['cm_w_pw2'].shape, j)),
           (w['norm_xa'], _whole(w['norm_xa'].shape)),
           (w['xa_w_q'], _whole(w['xa_w_q'].shape, layer))]
    x1, q = pl.pallas_call(
        functools.partial(_s_cmod_post_kernel, layer=layer),
        in_specs=[s for _, s in ins],
        out_specs=[_whole_out((r, d)), _whole_out((r, d))],
        out_shape=[_sds((r, d)), _sds((r, d))],
        grid=(1,),
        compiler_params=_params(1),
        name="sample_cmod_post",
    )(*[a for a, _ in ins])
    return x1, q, new_st


_MATMUL_WEIGHTS = ('lru_w_in', 'lru_wa', 'lru_wx', 'lru_w_out', 'cm_w_pw1', 'cm_w_pw2',
                   'xa_w_q', 'xa_w_kv', 'xa_w_o', 'ffn_w_up', 'ffn_w_down')


def kernel(x_prompt, x_sample, state_lru_h, state_lru_conv, state_cmod_conv, state_ffn_conv, cache_mem_k, cache_mem_v, mem_prompt, norm_mix, norm_xa, norm_ffn, norm_final, lru_w_in, lru_conv_w, lru_conv_b, lru_wa, lru_ba, lru_wx, lru_bx, lru_lambda, lru_w_out, cm_w_pw1, cm_b_pw1, cm_dw_w, cm_dw_b, cm_ln_g, cm_ln_b, cm_w_pw2, mem_norm, xa_w_q, xa_w_kv, xa_w_o, ffn_w_up, ffn_conv_w, ffn_conv_b, ffn_w_down):
    w = dict(norm_mix=norm_mix, norm_xa=norm_xa, norm_ffn=norm_ffn,
             norm_final=norm_final.reshape(1, -1),
             lru_w_in=lru_w_in, lru_conv_w=lru_conv_w, lru_conv_b=lru_conv_b, lru_wa=lru_wa,
             lru_ba=lru_ba, lru_wx=lru_wx, lru_bx=lru_bx, lru_lambda=lru_lambda,
             lru_w_out=lru_w_out, cm_w_pw1=cm_w_pw1, cm_b_pw1=cm_b_pw1, cm_dw_w=cm_dw_w,
             cm_dw_b=cm_dw_b, cm_ln_g=cm_ln_g, cm_ln_b=cm_ln_b, cm_w_pw2=cm_w_pw2,
             xa_w_q=xa_w_q, xa_w_kv=xa_w_kv, xa_w_o=xa_w_o,
             ffn_w_up=ffn_w_up, ffn_conv_w=ffn_conv_w, ffn_conv_b=ffn_conv_b,
             ffn_w_down=ffn_w_down)
    for name in _MATMUL_WEIGHTS:
        w[name] = w[name].astype(BF16)

    depth = norm_mix.shape[0]
    bsz, _, d = x_prompt.shape
    n_mem = mem_prompt.shape[1]
    heads = cache_mem_k.shape[3]

    k_p, v_p, kt_p, vb_p = _prompt_memkv(mem_prompt, mem_norm.reshape(depth, 1, d), w['xa_w_kv'])
    x = x_prompt
    p_h, p_lconv, p_cconv, p_fconv = [], [], [], []
    for i in range(depth):
        j = i // 2
        if i % 2 == 0:
            x, h_last, cbuf = _prompt_lru(x, i, j, w)
            p_h.append(h_last[:, 0, :])
            p_lconv.append(cbuf)
        else:
            x, cbuf = _prompt_cmod(x, i, j, w)
            p_cconv.append(cbuf)
        x = _prompt_xattn(x, i, kt_p, vb_p, w)
        x, fbuf = _prompt_ffn(x, i, i == depth - 1, w)
        p_fconv.append(fbuf)
    y_prompt = x

    r = x_sample.shape[0]
    xs = x_sample.reshape(r, d)
    lconv = state_lru_conv.reshape(state_lru_conv.shape[0], r, -1)
    cconv = state_cmod_conv.reshape(state_cmod_conv.shape[0], r, -1)
    fconv = state_ffn_conv.reshape(depth, r, -1)
    ck = cache_mem_k.reshape(depth, r, n_mem, d)
    cv = cache_mem_v.reshape(depth, r, n_mem, d)
    s_h, s_lconv, s_cconv, s_fconv = [], [], [], []
    for i in range(depth):
        j = i // 2
        if i % 2 == 0:
            xs, q, h_new, c_new = _sample_lru(xs, state_lru_h, lconv, i, j, w)
            s_h.append(h_new)
            s_lconv.append(c_new.reshape(r, -1, d))
        else:
            xs, q, c_new = _sample_cmod(xs, cconv, i, j, w)
            s_cconv.append(c_new.reshape(r, -1, d))
        o = _sample_attn(q, ck, cv, i)
        xs, f_new = _sample_ffn(xs, o, fconv, i, i == depth - 1, w)
        s_fconv.append(f_new.reshape(r, state_ffn_conv.shape[2], -1))
    y_sample = xs.reshape(x_sample.shape)

    return (y_prompt, y_sample,
            jnp.stack(p_h), jnp.stack(p_lconv), jnp.stack(p_cconv), jnp.stack(p_fconv),
            k_p.reshape(depth, bsz, n_mem, heads, d // heads),
            v_p.reshape(depth, bsz, n_mem, heads, d // heads),
            jnp.stack(s_h), jnp.stack(s_lconv), jnp.stack(s_cconv), jnp.stack(s_fconv))
```

```python
import functools

import jax
import jax.numpy as jnp
from jax import lax
from jax.experimental import pallas as pl
from jax.experimental.pallas import tpu as pltpu

F32 = jnp.float32
BF16 = jnp.bfloat16

EPS = 1e-6
RG_C = 8.0
N_RG_BLOCKS = 4
XA_HEADS = 4
SUBLANES = 8
LANES = 128
VMEM_LIMIT = 56 << 20

LRU_T_TILE = 256
CMOD_T_TILE = 256
XATTN_T_TILE = 512
FFN_T_TILE = 512
FFN_COL_TILE = 512
CMOD_COL_TILE = 256
CMOD_ROW_TILE = 64
SAMPLE_ATTN_BLOCK = 4
SAMPLE_CONV_ROWS = 16


def _rms(x, g):
    return x * lax.rsqrt(jnp.mean(x * x, axis=-1, keepdims=True) + EPS) * g


def _layer_norm(x, g, b):
    mu = jnp.mean(x, axis=-1, keepdims=True)
    xc = x - mu
    var = jnp.mean(xc * xc, axis=-1, keepdims=True)
    return xc * lax.rsqrt(var + EPS) * g + b


def _mm(a, w):
    return jnp.dot(a.astype(BF16), w, preferred_element_type=F32)


def _softplus(z):
    return jnp.maximum(z, 0.0) + jnp.log1p(jnp.exp(-jnp.abs(z)))


def _block_diag(xb, w_ref):
    blk = xb.shape[-1] // N_RG_BLOCKS
    return jnp.concatenate(
        [jnp.dot(xb[:, n * blk:(n + 1) * blk], w_ref[n], preferred_element_type=F32)
         for n in range(N_RG_BLOCKS)], axis=-1)


def _lru_gates(rec, wa_ref, ba, wx_ref, bx, lam):
    rb = rec.astype(BF16)
    ga = _block_diag(rb, wa_ref) + ba
    gx = _block_diag(rb, wx_ref) + bx
    log_a = -RG_C * jax.nn.sigmoid(ga) * _softplus(-lam)
    a = jnp.exp(log_a)
    u = jnp.sqrt(1.0 - a * a) * jax.nn.sigmoid(gx) * rec
    return a, u


def _glu(ab, d):
    return ab[:, :d] * jax.nn.sigmoid(ab[:, d:])


def _p_memkv_kernel(m_ref, g_ref, wkv_ref, k_ref, v_ref, kt_ref, vb_ref):
    d = m_ref.shape[-1]
    hn = _rms(m_ref[...], g_ref[...]).astype(BF16)
    k = jnp.dot(hn, wkv_ref[:, :d], preferred_element_type=F32)
    v = jnp.dot(hn, wkv_ref[:, d:], preferred_element_type=F32)
    k_ref[...] = k
    v_ref[...] = v
    kt_ref[...] = k.T.astype(BF16)
    vb_ref[...] = v.astype(BF16)


def _p_lru_kernel(x_ref, nm_ref, win_ref, cw_ref, cb_ref, wa_ref, ba_ref, wx_ref, bx_ref,
                  lam_ref, wout_ref,
                  xo_ref, ho_ref, co_ref,
                  cbuf, a_s, u_s, y_s, h_s, *, layer, j):
    tt, d = x_ref.shape
    kw = cw_ref.shape[0]
    hist = kw - 1
    tc = pl.program_id(1)

    @pl.when(tc == 0)
    def _():
        cbuf[0:SUBLANES, :] = jnp.zeros((SUBLANES, d), F32)
        h_s[...] = jnp.zeros_like(h_s)

    x = x_ref[...]
    hn = _rms(x, nm_ref[layer:layer + 1, :]).astype(BF16)
    gate = jnp.dot(hn, win_ref[:, :d], preferred_element_type=F32)
    rec = jnp.dot(hn, win_ref[:, d:], preferred_element_type=F32)

    cbuf[SUBLANES:SUBLANES + tt, :] = rec
    conv = cb_ref[j:j + 1, :] + cw_ref[hist:kw, :] * rec
    for k in range(hist):
        off = SUBLANES - hist + k
        conv = conv + cw_ref[k:k + 1, :] * cbuf[off:off + tt, :]
    tail = rec[tt - hist:, :]
    co_ref[...] = tail
    cbuf[SUBLANES - hist:SUBLANES, :] = tail

    a, u = _lru_gates(conv, wa_ref, ba_ref[j:j + 1, :], wx_ref, bx_ref[j:j + 1, :],
                      lam_ref[j:j + 1, :])
    a_s[...] = a
    u_s[...] = u

    row = lax.broadcasted_iota(jnp.int32, (SUBLANES, d), 0)

    def group(g, h):
        r0 = pl.multiple_of(g * SUBLANES, SUBLANES)
        ag = a_s[pl.ds(r0, SUBLANES), :]
        ug = u_s[pl.ds(r0, SUBLANES), :]
        for k in (1, 2, 4):
            keep = row >= k
            a_sh = pltpu.roll(ag, k, 0)
            u_sh = pltpu.roll(ug, k, 0)
            ug = ug + ag * jnp.where(keep, u_sh, 0.0)
            ag = ag * jnp.where(keep, a_sh, 1.0)
        hg = ug + ag * h
        y_s[pl.ds(r0, SUBLANES), :] = hg
        return hg[SUBLANES - 1:SUBLANES, :]

    h = lax.fori_loop(0, tt // SUBLANES, group, h_s[...])
    h_s[...] = h
    ho_ref[...] = h

    out = _mm(jax.nn.gelu(gate) * y_s[...], wout_ref[...])
    xo_ref[...] = x + out


def _p_cmod_kernel(x_ref, nm_ref, pw1_ref, b1_ref, dw_ref, dwb_ref, lng_ref, lnb_ref, pw2_ref,
                   xo_ref, so_ref, gbuf, shbuf, cbuf, *, layer, j, pad):
    tt, d = x_ref.shape
    kw = dw_ref.shape[0]
    hist = kw - 1
    lead = pad - SUBLANES
    tc = pl.program_id(1)

    @pl.when(tc == 0)
    def _():
        gbuf[0:pad, :] = jnp.zeros((pad, d), F32)

    x = x_ref[...]
    hn = _rms(x, nm_ref[layer:layer + 1, :]).astype(BF16)
    for c in range(d // CMOD_COL_TILE):
        lo = c * CMOD_COL_TILE
        cs = slice(lo, lo + CMOD_COL_TILE)
        gs = slice(d + lo, d + lo + CMOD_COL_TILE)
        a = jnp.dot(hn, pw1_ref[:, cs], preferred_element_type=F32) + b1_ref[j:j + 1, cs]
        bg = jnp.dot(hn, pw1_ref[:, gs], preferred_element_type=F32) + b1_ref[j:j + 1, gs]
        gbuf[pad:pad + tt, cs] = a * jax.nn.sigmoid(bg)
        for r in range(1, SUBLANES):
            shbuf[r - 1, :, cs] = gbuf[SUBLANES - r:SUBLANES - r + lead + tt, cs]
        for rb in range(tt // CMOD_ROW_TILE):
            r0 = rb * CMOD_ROW_TILE
            for lt in range(CMOD_COL_TILE // LANES):
                ls = slice(lo + lt * LANES, lo + (lt + 1) * LANES)
                acc = dwb_ref[j:j + 1, ls] + dw_ref[hist:kw, ls] * gbuf[pad + r0:pad + r0 + CMOD_ROW_TILE, ls]
                for s in range(1, kw):
                    i, r = divmod(s, SUBLANES)
                    if r == 0:
                        xs = gbuf[pad - s + r0:pad - s + r0 + CMOD_ROW_TILE, ls]
                    else:
                        u0 = lead - SUBLANES * i + r0
                        xs = shbuf[r - 1, u0:u0 + CMOD_ROW_TILE, ls]
                    acc = acc + dw_ref[hist - s:hist - s + 1, ls] * xs
                cbuf[r0:r0 + CMOD_ROW_TILE, ls] = acc
    tail = gbuf[pad + tt - hist:pad + tt, :]
    so_ref[...] = tail
    gbuf[pad - hist:pad, :] = tail

    c = jax.nn.silu(_layer_norm(cbuf[...], lng_ref[j:j + 1, :], lnb_ref[j:j + 1, :]))
    xo_ref[...] = x + _mm(c, pw2_ref[...])


def _p_xattn_kernel(x_ref, nm_ref, wq_ref, kt_ref, v_ref, wo_ref, xo_ref, *, layer):
    tt, d = x_ref.shape
    dh = d // XA_HEADS
    x = x_ref[...]
    q = _mm(_rms(x, nm_ref[layer:layer + 1, :]), wq_ref[...]).astype(BF16)
    heads = []
    for h in range(XA_HEADS):
        cs = slice(h * dh, (h + 1) * dh)
        s = jnp.dot(q[:, cs], kt_ref[cs, :], preferred_element_type=F32) * (dh ** -0.5)
        e = jnp.exp(s - jnp.max(s, axis=-1, keepdims=True))
        p = e / jnp.sum(e, axis=-1, keepdims=True)
        heads.append(jnp.dot(p.astype(BF16), v_ref[:, cs], preferred_element_type=F32))
    o = jnp.concatenate(heads, axis=-1)
    xo_ref[...] = x + _mm(o, wo_ref[...])


def _p_ffn_kernel(x_ref, nm_ref, wup_ref, cw_ref, cb_ref, wdn_ref, nf_ref,
                  xo_ref, so_ref, buf, *, layer, final):
    tt, d = x_ref.shape
    dff = wdn_ref.shape[0]
    kw = cw_ref.shape[0]
    hist = kw - 1
    tc = pl.program_id(1)

    @pl.when(tc == 0)
    def _():
        buf[0:SUBLANES, :] = jnp.zeros((SUBLANES, dff), F32)

    x = x_ref[...]
    hn = _rms(x, nm_ref[layer:layer + 1, :]).astype(BF16)
    acc = jnp.zeros((tt, d), F32)
    for c in range(dff // FFN_COL_TILE):
        cs = slice(c * FFN_COL_TILE, (c + 1) * FFN_COL_TILE)
        us = slice(dff + c * FFN_COL_TILE, dff + (c + 1) * FFN_COL_TILE)
        g = jnp.dot(hn, wup_ref[:, cs], preferred_element_type=F32)
        u = jnp.dot(hn, wup_ref[:, us], preferred_element_type=F32)
        buf[SUBLANES:SUBLANES + tt, cs] = g
        conv = cb_ref[layer:layer + 1, cs] + cw_ref[hist:kw, cs] * g
        for k in range(hist):
            off = SUBLANES - hist + k
            conv = conv + cw_ref[k:k + 1, cs] * buf[off:off + tt, cs]
        act = (jax.nn.gelu(conv) * u).astype(BF16)
        acc = acc + jnp.dot(act, wdn_ref[cs, :], preferred_element_type=F32)
    tail = buf[SUBLANES + tt - hist:SUBLANES + tt, :]
    so_ref[...] = tail
    buf[SUBLANES - hist:SUBLANES, :] = tail

    xn = x + acc
    if final:
        xn = _rms(xn, nf_ref[...])
    xo_ref[...] = xn


def _s_lru_kernel(x_ref, h0_ref, cst_ref, nm_ref, win_ref, cw_ref, cb_ref, wa_ref, ba_ref,
                  wx_ref, bx_ref, lam_ref, wout_ref, nxa_ref, wq_ref,
                  xo_ref, q_ref, ho_ref, co_ref, *, layer, j):
    d = x_ref.shape[-1]
    kw = cw_ref.shape[0]
    hist = kw - 1
    x = x_ref[...]
    hn = _rms(x, nm_ref[layer:layer + 1, :]).astype(BF16)
    gate = jnp.dot(hn, win_ref[:, :d], preferred_element_type=F32)
    rec = jnp.dot(hn, win_ref[:, d:], preferred_element_type=F32)

    conv = cb_ref[j:j + 1, :] + cw_ref[hist:kw, :] * rec
    for k in range(hist):
        conv = conv + cw_ref[k:k + 1, :] * cst_ref[:, k * d:(k + 1) * d]
    co_ref[:, :(hist - 1) * d] = cst_ref[:, d:]
    co_ref[:, (hist - 1) * d:] = rec

    a, u = _lru_gates(conv, wa_ref, ba_ref[j:j + 1, :], wx_ref, bx_ref[j:j + 1, :],
                      lam_ref[j:j + 1, :])
    h = a * h0_ref[...] + u
    ho_ref[...] = h
    x1 = x + _mm(jax.nn.gelu(gate) * h, wout_ref[...])
    xo_ref[...] = x1
    q_ref[...] = _mm(_rms(x1, nxa_ref[layer:layer + 1, :]), wq_ref[...])


def _s_attn_kernel(q_ref, k_ref, v_ref, o_ref):
    nb, _, dh = q_ref.shape
    for b in range(nb):
        s = jnp.sum(k_ref[b] * q_ref[b][None], axis=-1, keepdims=True) * (dh ** -0.5)
        e = jnp.exp(s - jnp.max(s, axis=0, keepdims=True))
        p = e / jnp.sum(e, axis=0, keepdims=True)
        o_ref[b] = jnp.sum(p * v_ref[b], axis=0)


def _s_ffn_kernel(x_ref, o_ref, wo_ref, nm_ref, wup_ref, cw_ref, cb_ref, wdn_ref, st_ref, nf_ref,
                  xo_ref, so_ref, *, layer, final):
    dff = wdn_ref.shape[0]
    kw = cw_ref.shape[0]
    hist = kw - 1
    x = x_ref[...] + _mm(o_ref[...], wo_ref[...])
    hn = _rms(x, nm_ref[layer:layer + 1, :]).astype(BF16)
    g = jnp.dot(hn, wup_ref[:, :dff], preferred_element_type=F32)
    u = jnp.dot(hn, wup_ref[:, dff:], preferred_element_type=F32)
    conv = cb_ref[layer:layer + 1, :] + cw_ref[hist:kw, :] * g
    for k in range(hist):
        conv = conv + cw_ref[k:k + 1, :] * st_ref[:, k * dff:(k + 1) * dff]
    so_ref[:, :(hist - 1) * dff] = st_ref[:, dff:]
    so_ref[:, (hist - 1) * dff:] = g
    xn = x + _mm(jax.nn.gelu(conv) * u, wdn_ref[...])
    if final:
        xn = _rms(xn, nf_ref[...])
    xo_ref[...] = xn


def _s_cmod_pre_kernel(x_ref, nm_ref, pw1_ref, b1_ref, g_ref, *, layer, j):
    d = x_ref.shape[-1]
    hn = _rms(x_ref[...], nm_ref[layer:layer + 1, :])
    g_ref[...] = _glu(_mm(hn, pw1_ref[...]) + b1_ref[j:j + 1, :], d)


def _s_cmod_conv_kernel(st_ref, g_ref, dw_ref, dwb_ref, lng_ref, lnb_ref, c_ref, so_ref, *, j):
    d = g_ref.shape[-1]
    kw = dw_ref.shape[0]
    hist = kw - 1
    g = g_ref[...]
    c = dwb_ref[j:j + 1, :] + dw_ref[hist:kw, :] * g
    for k in range(hist):
        c = c + dw_ref[k:k + 1, :] * st_ref[:, k * d:(k + 1) * d]
    so_ref[:, :(hist - 1) * d] = st_ref[:, d:]
    so_ref[:, (hist - 1) * d:] = g
    c_ref[...] = jax.nn.silu(_layer_norm(c, lng_ref[j:j + 1, :], lnb_ref[j:j + 1, :]))


def _s_cmod_post_kernel(x_ref, c_ref, pw2_ref, nxa_ref, wq_ref, xo_ref, q_ref, *, layer):
    x1 = x_ref[...] + _mm(c_ref[...], pw2_ref[...])
    xo_ref[...] = x1
    q_ref[...] = _mm(_rms(x1, nxa_ref[layer:layer + 1, :]), wq_ref[...])


def _params(n_grid):
    return pltpu.CompilerParams(dimension_semantics=("arbitrary",) * n_grid,
                                vmem_limit_bytes=VMEM_LIMIT)


def _resident(shape, lead=None):
    if lead is None:
        nd = len(shape)
        return pl.BlockSpec(shape, lambda *_: (0,) * nd, pipeline_mode=pl.Buffered(1))
    nd = len(shape) - 1
    return pl.BlockSpec((None,) + tuple(shape[1:]), lambda *_: (lead,) + (0,) * nd,
                        pipeline_mode=pl.Buffered(1))


_whole = _resident


def _whole_out(shape):
    nd = len(shape)
    return pl.BlockSpec(shape, lambda *_: (0,) * nd)


def _sds(shape, dtype=F32):
    return jax.ShapeDtypeStruct(shape, dtype)


def _prompt_memkv(mem, mem_norm3, wkv):
    depth = wkv.shape[0]
    b, m, d = mem.shape
    return pl.pallas_call(
        _p_memkv_kernel,
        grid=(depth, b),
        in_specs=[pl.BlockSpec((None, m, d), lambda i, n: (n, 0, 0)),
                  pl.BlockSpec((None, 1, d), lambda i, n: (i, 0, 0)),
                  pl.BlockSpec((None, d, 2 * d), lambda i, n: (i, 0, 0))],
        out_specs=[pl.BlockSpec((None, None, m, d), lambda i, n: (i, n, 0, 0)),
                   pl.BlockSpec((None, None, m, d), lambda i, n: (i, n, 0, 0)),
                   pl.BlockSpec((None, None, d, m), lambda i, n: (i, n, 0, 0)),
                   pl.BlockSpec((None, None, m, d), lambda i, n: (i, n, 0, 0))],
        out_shape=[_sds((depth, b, m, d)), _sds((depth, b, m, d)),
                   _sds((depth, b, d, m), BF16), _sds((depth, b, m, d), BF16)],
        compiler_params=_params(2),
        name="prompt_memkv",
    )(mem, mem_norm3, wkv)


def _x_spec(tt, d):
    return pl.BlockSpec((None, tt, d), lambda n, t: (n, t, 0))


def _state_spec(rows, cols):
    return pl.BlockSpec((None, rows, cols), lambda n, t: (n, 0, 0))


def _prompt_lru(x, layer, j, w):
    b, t, d = x.shape
    tt = LRU_T_TILE
    hist = w['lru_conv_w'].shape[1] - 1
    ins = [(x, _x_spec(tt, d)),
           (w['norm_mix'], _resident(w['norm_mix'].shape)),
           (w['lru_w_in'], _resident(w['lru_w_in'].shape, j)),
           (w['lru_conv_w'], _resident(w['lru_conv_w'].shape, j)),
           (w['lru_conv_b'], _resident(w['lru_conv_b'].shape)),
           (w['lru_wa'], _resident(w['lru_wa'].shape, j)),
           (w['lru_ba'], _resident(w['lru_ba'].shape)),
           (w['lru_wx'], _resident(w['lru_wx'].shape, j)),
           (w['lru_bx'], _resident(w['lru_bx'].shape)),
           (w['lru_lambda'], _resident(w['lru_lambda'].shape)),
           (w['lru_w_out'], _resident(w['lru_w_out'].shape, j))]
    return pl.pallas_call(
        functools.partial(_p_lru_kernel, layer=layer, j=j),
        grid=(b, t // tt),
        in_specs=[s for _, s in ins],
        out_specs=[_x_spec(tt, d), _state_spec(1, d), _state_spec(hist, d)],
        out_shape=[_sds((b, t, d)), _sds((b, 1, d)), _sds((b, hist, d))],
        scratch_shapes=[pltpu.VMEM((SUBLANES + tt, d), F32), pltpu.VMEM((tt, d), F32),
                        pltpu.VMEM((tt, d), F32), pltpu.VMEM((tt, d), F32),
                        pltpu.VMEM((1, d), F32)],
        compiler_params=_params(2),
        name="prompt_lru",
    )(*[a for a, _ in ins])


def _prompt_cmod(x, layer, j, w):
    b, t, d = x.shape
    tt = CMOD_T_TILE
    hist = w['cm_dw_w'].shape[1] - 1
    pad = -(-hist // SUBLANES) * SUBLANES
    ins = [(x, _x_spec(tt, d)),
           (w['norm_mix'], _resident(w['norm_mix'].shape)),
           (w['cm_w_pw1'], _resident(w['cm_w_pw1'].shape, j)),
           (w['cm_b_pw1'], _resident(w['cm_b_pw1'].shape)),
           (w['cm_dw_w'], _resident(w['cm_dw_w'].shape, j)),
           (w['cm_dw_b'], _resident(w['cm_dw_b'].shape)),
           (w['cm_ln_g'], _resident(w['cm_ln_g'].shape)),
           (w['cm_ln_b'], _resident(w['cm_ln_b'].shape)),
           (w['cm_w_pw2'], _resident(w['cm_w_pw2'].shape, j))]
    return pl.pallas_call(
        functools.partial(_p_cmod_kernel, layer=layer, j=j, pad=pad),
        grid=(b, t // tt),
        in_specs=[s for _, s in ins],
        out_specs=[_x_spec(tt, d), _state_spec(hist, d)],
        out_shape=[_sds((b, t, d)), _sds((b, hist, d))],
        scratch_shapes=[pltpu.VMEM((pad + tt, d), F32),
                        pltpu.VMEM((SUBLANES - 1, pad - SUBLANES + tt, d), F32),
                        pltpu.VMEM((tt, d), F32)],
        compiler_params=_params(2),
        name="prompt_cmod",
    )(*[a for a, _ in ins])


def _prompt_xattn(x, layer, kt, vb, w):
    b, t, d = x.shape
    tt = XATTN_T_TILE
    m = vb.shape[2]
    ins = [(x, _x_spec(tt, d)),
           (w['norm_xa'], _resident(w['norm_xa'].shape)),
           (w['xa_w_q'], _resident(w['xa_w_q'].shape, layer)),
           (kt, pl.BlockSpec((None, None, d, m), lambda n, t_: (layer, n, 0, 0))),
           (vb, pl.BlockSpec((None, None, m, d), lambda n, t_: (layer, n, 0, 0))),
           (w['xa_w_o'], _resident(w['xa_w_o'].shape, layer))]
    return pl.pallas_call(
        functools.partial(_p_xattn_kernel, layer=layer),
        grid=(b, t // tt),
        in_specs=[s for _, s in ins],
        out_specs=_x_spec(tt, d),
        out_shape=_sds((b, t, d)),
        compiler_params=_params(2),
        name="prompt_xattn",
    )(*[a for a, _ in ins])


def _prompt_ffn(x, layer, final, w):
    b, t, d = x.shape
    tt = FFN_T_TILE
    dff = w['ffn_w_down'].shape[1]
    hist = w['ffn_conv_w'].shape[1] - 1
    ins = [(x, _x_spec(tt, d)),
           (w['norm_ffn'], _resident(w['norm_ffn'].shape)),
           (w['ffn_w_up'], _resident(w['ffn_w_up'].shape, layer)),
           (w['ffn_conv_w'], _resident(w['ffn_conv_w'].shape, layer)),
           (w['ffn_conv_b'], _resident(w['ffn_conv_b'].shape)),
           (w['ffn_w_down'], _resident(w['ffn_w_down'].shape, layer)),
           (w['norm_final'], _resident(w['norm_final'].shape))]
    return pl.pallas_call(
        functools.partial(_p_ffn_kernel, layer=layer, final=final),
        grid=(b, t // tt),
        in_specs=[s for _, s in ins],
        out_specs=[_x_spec(tt, d), _state_spec(hist, dff)],
        out_shape=[_sds((b, t, d)), _sds((b, hist, dff))],
        scratch_shapes=[pltpu.VMEM((SUBLANES + tt, dff), F32)],
        compiler_params=_params(2),
        name="prompt_ffn",
    )(*[a for a, _ in ins])


def _sample_lru(x, h0, cst, layer, j, w):
    r, d = x.shape
    ins = [(x, _whole(x.shape)), (h0, _whole(h0.shape, j)), (cst, _whole(cst.shape, j)),
           (w['norm_mix'], _whole(w['norm_mix'].shape)),
           (w['lru_w_in'], _whole(w['lru_w_in'].shape, j)),
           (w['lru_conv_w'], _whole(w['lru_conv_w'].shape, j)),
           (w['lru_conv_b'], _whole(w['lru_conv_b'].shape)),
           (w['lru_wa'], _whole(w['lru_wa'].shape, j)),
           (w['lru_ba'], _whole(w['lru_ba'].shape)),
           (w['lru_wx'], _whole(w['lru_wx'].shape, j)),
           (w['lru_bx'], _whole(w['lru_bx'].shape)),
           (w['lru_lambda'], _whole(w['lru_lambda'].shape)),
           (w['lru_w_out'], _whole(w['lru_w_out'].shape, j)),
           (w['norm_xa'], _whole(w['norm_xa'].shape)),
           (w['xa_w_q'], _whole(w['xa_w_q'].shape, layer))]
    return pl.pallas_call(
        functools.partial(_s_lru_kernel, layer=layer, j=j),
        in_specs=[s for _, s in ins],
        out_specs=[_whole_out((r, d)), _whole_out((r, d)), _whole_out((r, d)),
                   _whole_out(cst.shape[1:])],
        out_shape=[_sds((r, d)), _sds((r, d)), _sds((r, d)), _sds(cst.shape[1:])],
        grid=(1,),
        compiler_params=_params(1),
        name="sample_lru",
    )(*[a for a, _ in ins])


def _sample_attn(q, k, v, layer):
    r, d = q.shape
    _, _, m, nh, dh = k.shape
    nb = SAMPLE_ATTN_BLOCK
    kv_spec = pl.BlockSpec((None, nb, m, nh, dh), lambda n: (layer, n, 0, 0, 0))
    row_spec = pl.BlockSpec((nb, nh, dh), lambda n: (n, 0, 0))
    o = pl.pallas_call(
        _s_attn_kernel,
        grid=(r // nb,),
        in_specs=[row_spec, kv_spec, kv_spec],
        out_specs=row_spec,
        out_shape=_sds((r, nh, dh)),
        compiler_params=_params(1),
        name="sample_attn",
    )(q.reshape(r, nh, dh), k, v)
    return o.reshape(r, d)


def _sample_ffn(x, o, st, layer, final, w):
    r, d = x.shape
    ins = [(x, _whole(x.shape)), (o, _whole(o.shape)),
           (w['xa_w_o'], _whole(w['xa_w_o'].shape, layer)),
           (w['norm_ffn'], _whole(w['norm_ffn'].shape)),
           (w['ffn_w_up'], _whole(w['ffn_w_up'].shape, layer)),
           (w['ffn_conv_w'], _whole(w['ffn_conv_w'].shape, layer)),
           (w['ffn_conv_b'], _whole(w['ffn_conv_b'].shape)),
           (w['ffn_w_down'], _whole(w['ffn_w_down'].shape, layer)),
           (st, _whole(st.shape, layer)),
           (w['norm_final'], _whole(w['norm_final'].shape))]
    return pl.pallas_call(
        functools.partial(_s_ffn_kernel, layer=layer, final=final),
        in_specs=[s for _, s in ins],
        out_specs=[_whole_out((r, d)), _whole_out(st.shape[1:])],
        out_shape=[_sds((r, d)), _sds(st.shape[1:])],
        grid=(1,),
        compiler_params=_params(1),
        name="sample_ffn",
    )(*[a for a, _ in ins])


def _sample_cmod(x, st, layer, j, w):
    r, d = x.shape
    ins = [(x, _whole(x.shape)),
           (w['norm_mix'], _whole(w['norm_mix'].shape)),
           (w['cm_w_pw1'], _whole(w['cm_w_pw1'].shape, j)),
           (w['cm_b_pw1'], _whole(w['cm_b_pw1'].shape))]
    g = pl.pallas_call(
        functools.partial(_s_cmod_pre_kernel, layer=layer, j=j),
        in_specs=[s for _, s in ins],
        out_specs=_whole_out((r, d)),
        out_shape=_sds((r, d)),
        grid=(1,),
        compiler_params=_params(1),
        name="sample_cmod_pre",
    )(*[a for a, _ in ins])

    rows = SAMPLE_CONV_ROWS
    sc = st.shape[-1]
    ins = [(st, pl.BlockSpec((None, rows, sc), lambda n: (j, n, 0))),
           (g, pl.BlockSpec((rows, d), lambda n: (n, 0))),
           (w['cm_dw_w'], _resident(w['cm_dw_w'].shape, j)),
           (w['cm_dw_b'], _resident(w['cm_dw_b'].shape)),
           (w['cm_ln_g'], _resident(w['cm_ln_g'].shape)),
           (w['cm_ln_b'], _resident(w['cm_ln_b'].shape))]
    c, new_st = pl.pallas_call(
        functools.partial(_s_cmod_conv_kernel, j=j),
        grid=(r // rows,),
        in_specs=[s for _, s in ins],
        out_specs=[pl.BlockSpec((rows, d), lambda n: (n, 0)),
                   pl.BlockSpec((rows, sc), lambda n: (n, 0))],
        out_shape=[_sds((r, d)), _sds((r, sc))],
        compiler_params=_params(1),
        name="sample_cmod_conv",
    )(*[a for a, _ in ins])

    ins = [(x, _whole(x.shape)), (c, _whole(c.shape)),
           (w['cm_w_pw2'], _whole(w['cm_w_pw2'].shape, j)),
           (w['norm_xa'], _whole(w['norm_xa'].shape)),
           (w['xa_w_q'], _whole(w['xa_w_q'].shape, layer))]
    x1, q = pl.pallas_call(
        functools.partial(_s_cmod_post_kernel, layer=layer),
        in_specs=[s for _, s in ins],
        out_specs=[_whole_out((r, d)), _whole_out((r, d))],
        out_shape=[_sds((r, d)), _sds((r, d))],
        grid=(1,),
        compiler_params=_params(1),
        name="sample_cmod_post",
    )(*[a for a, _ in ins])
    return x1, q, new_st


_MATMUL_WEIGHTS = ('lru_w_in', 'lru_wa', 'lru_wx', 'lru_w_out', 'cm_w_pw1', 'cm_w_pw2',
                   'xa_w_q', 'xa_w_kv', 'xa_w_o', 'ffn_w_up', 'ffn_w_down')


def kernel(x_prompt, x_sample, state_lru_h, state_lru_conv, state_cmod_conv, state_ffn_conv, cache_mem_k, cache_mem_v, mem_prompt, norm_mix, norm_xa, norm_ffn, norm_final, lru_w_in, lru_conv_w, lru_conv_b, lru_wa, lru_ba, lru_wx, lru_bx, lru_lambda, lru_w_out, cm_w_pw1, cm_b_pw1, cm_dw_w, cm_dw_b, cm_ln_g, cm_ln_b, cm_w_pw2, mem_norm, xa_w_q, xa_w_kv, xa_w_o, ffn_w_up, ffn_conv_w, ffn_conv_b, ffn_w_down):
    w = dict(norm_mix=norm_mix, norm_xa=norm_xa, norm_ffn=norm_ffn,
             norm_final=norm_final.reshape(1, -1),
             lru_w_in=lru_w_in, lru_conv_w=lru_conv_w, lru_conv_b=lru_conv_b, lru_wa=lru_wa,
             lru_ba=lru_ba, lru_wx=lru_wx, lru_bx=lru_bx, lru_lambda=lru_lambda,
             lru_w_out=lru_w_out, cm_w_pw1=cm_w_pw1, cm_b_pw1=cm_b_pw1, cm_dw_w=cm_dw_w,
             cm_dw_b=cm_dw_b, cm_ln_g=cm_ln_g, cm_ln_b=cm_ln_b, cm_w_pw2=cm_w_pw2,
             xa_w_q=xa_w_q, xa_w_kv=xa_w_kv, xa_w_o=xa_w_o,
             ffn_w_up=ffn_w_up, ffn_conv_w=ffn_conv_w, ffn_conv_b=ffn_conv_b,
             ffn_w_down=ffn_w_down)
    for name in _MATMUL_WEIGHTS:
        w[name] = w[name].astype(BF16)

    depth = norm_mix.shape[0]
    bsz, _, d = x_prompt.shape
    n_mem = mem_prompt.shape[1]
    heads = cache_mem_k.shape[3]

    k_p, v_p, kt_p, vb_p = _prompt_memkv(mem_prompt, mem_norm.reshape(depth, 1, d), w['xa_w_kv'])
    x = x_prompt
    p_h, p_lconv, p_cconv, p_fconv = [], [], [], []
    for i in range(depth):
        j = i // 2
        if i % 2 == 0:
            x, h_last, cbuf = _prompt_lru(x, i, j, w)
            p_h.append(h_last[:, 0, :])
            p_lconv.append(cbuf)
        else:
            x, cbuf = _prompt_cmod(x, i, j, w)
            p_cconv.append(cbuf)
        x = _prompt_xattn(x, i, kt_p, vb_p, w)
        x, fbuf = _prompt_ffn(x, i, i == depth - 1, w)
        p_fconv.append(fbuf)
    y_prompt = x

    r = x_sample.shape[0]
    xs = x_sample.reshape(r, d)
    lconv = state_lru_conv.reshape(state_lru_conv.shape[0], r, -1)
    cconv = state_cmod_conv.reshape(state_cmod_conv.shape[0], r, -1)
    fconv = state_ffn_conv.reshape(depth, r, -1)
    s_h, s_lconv, s_cconv, s_fconv = [], [], [], []
    for i in range(depth):
        j = i // 2
        if i % 2 == 0:
            xs, q, h_new, c_new = _sample_lru(xs, state_lru_h, lconv, i, j, w)
            s_h.append(h_new)
            s_lconv.append(c_new.reshape(r, -1, d))
        else:
            xs, q, c_new = _sample_cmod(xs, cconv, i, j, w)
            s_cconv.append(c_new.reshape(r, -1, d))
        o = _sample_attn(q, cache_mem_k, cache_mem_v, i)
        xs, f_new = _sample_ffn(xs, o, fconv, i, i == depth - 1, w)
        s_fconv.append(f_new.reshape(r, state_ffn_conv.shape[2], -1))
    y_sample = xs.reshape(x_sample.shape)

    return (y_prompt, y_sample,
            jnp.stack(p_h), jnp.stack(p_lconv), jnp.stack(p_cconv), jnp.stack(p_fconv),
            k_p.reshape(depth, bsz, n_mem, heads, d // heads),
            v_p.reshape(depth, bsz, n_mem, heads, d // heads),
            jnp.stack(s_h), jnp.stack(s_lconv), jnp.stack(s_cconv), jnp.stack(s_fconv))
```

```python
import functools

import jax
import jax.numpy as jnp
from jax import lax
from jax.experimental import pallas as pl
from jax.experimental.pallas import tpu as pltpu

F32 = jnp.float32
BF16 = jnp.bfloat16

EPS = 1e-6
RG_C = 8.0
N_RG_BLOCKS = 4
XA_HEADS = 4
SUBLANES = 8
LANES = 128
VMEM_LIMIT = 56 << 20

MIXER_TIME_ROWS = 64
XATTN_T_TILE = 512
FFN_T_TILE = 512
FFN_COL_TILE = 1536
CMOD_COL_TILE = 256
CMOD_ROW_TILE = 64
SAMPLE_ATTN_BLOCK = 4
SAMPLE_CONV_ROWS = 16


def _rms(x, g):
    return x * lax.rsqrt(jnp.mean(x * x, axis=-1, keepdims=True) + EPS) * g


def _layer_norm(x, g, b):
    mu = jnp.mean(x, axis=-1, keepdims=True)
    xc = x - mu
    var = jnp.mean(xc * xc, axis=-1, keepdims=True)
    return xc * lax.rsqrt(var + EPS) * g + b


def _mm(a, w):
    return jnp.dot(a.astype(BF16), w, preferred_element_type=F32)


def _softplus(z):
    return jnp.maximum(z, 0.0) + jnp.log1p(jnp.exp(-jnp.abs(z)))


def _block_diag(xb, w_ref):
    blk = xb.shape[-1] // N_RG_BLOCKS
    return jnp.concatenate(
        [jnp.dot(xb[:, n * blk:(n + 1) * blk], w_ref[n], preferred_element_type=F32)
         for n in range(N_RG_BLOCKS)], axis=-1)


def _lru_decay_input(rec, ga, gx, lam):
    log_a = -RG_C * jax.nn.sigmoid(ga) * _softplus(-lam)
    a = jnp.exp(log_a)
    u = jnp.sqrt(1.0 - a * a) * jax.nn.sigmoid(gx) * rec
    return a, u


def _lru_gates(rec, wa_ref, ba, wx_ref, bx, lam):
    rb = rec.astype(BF16)
    return _lru_decay_input(rec, _block_diag(rb, wa_ref) + ba, _block_diag(rb, wx_ref) + bx, lam)


def _glu(ab, d):
    return ab[:, :d] * jax.nn.sigmoid(ab[:, d:])


def _p_memkv_kernel(m_ref, g_ref, wkv_ref, k_ref, v_ref, kt_ref, vb_ref):
    d = m_ref.shape[-1]
    hn = _rms(m_ref[...], g_ref[...]).astype(BF16)
    k = jnp.dot(hn, wkv_ref[:, :d], preferred_element_type=F32)
    v = jnp.dot(hn, wkv_ref[:, d:], preferred_element_type=F32)
    dh = d // k_ref.shape[1]
    for h in range(k_ref.shape[1]):
        k_ref[:, h, :] = k[:, h * dh:(h + 1) * dh]
        v_ref[:, h, :] = v[:, h * dh:(h + 1) * dh]
    kt_ref[...] = k.T.astype(BF16)
    vb_ref[...] = v.astype(BF16)


def _p_lru_kernel(x_ref, nm_ref, win_ref, cw_ref, cb_ref, wa_ref, ba_ref, wx_ref, bx_ref,
                  lam_ref, wout_ref,
                  xo_ref, ho_ref, co_ref,
                  rec_t, y_t, h_s, *, layer, j):
    nb, tr, d = x_ref.shape
    rows = nb * tr
    kw = cw_ref.shape[0]
    hist = kw - 1
    hrows = hist * nb
    nl = d // LANES
    tc = pl.program_id(0)

    @pl.when(tc == 0)
    def _():
        rec_t[:, 0:hrows, :] = jnp.zeros((nl, hrows, LANES), F32)
        h_s[...] = jnp.zeros_like(h_s)

    x = x_ref[...].reshape(rows, d)
    hn = _rms(x, nm_ref[layer:layer + 1, :]).astype(BF16)
    cblk = d // N_RG_BLOCKS
    lanes_per_blk = cblk // LANES

    def in_proj(n):
        lo = n * cblk
        return (jnp.dot(hn, win_ref[:, lo:lo + cblk], preferred_element_type=F32),
                jnp.dot(hn, win_ref[:, d + lo:d + lo + cblk], preferred_element_type=F32))

    out = jnp.zeros((rows, d), F32)
    nxt = in_proj(0)
    for n in range(N_RG_BLOCKS):
        gate, rec = nxt
        if n + 1 < N_RG_BLOCKS:
            nxt = in_proj(n + 1)
        cs = slice(n * cblk, (n + 1) * cblk)
        slabs = []
        for lt in range(lanes_per_blk):
            l = n * lanes_per_blk + lt
            ls = slice(l * LANES, (l + 1) * LANES)
            for b in range(nb):
                rec_t[l, pl.ds(hrows + b, tr, stride=nb), :] = (
                    rec[b * tr:(b + 1) * tr, lt * LANES:(lt + 1) * LANES])
            acc = cb_ref[j:j + 1, ls] + cw_ref[hist:kw, ls] * rec_t[l, hrows:hrows + rows, :]
            for k in range(hist):
                acc = acc + cw_ref[k:k + 1, ls] * rec_t[l, k * nb:k * nb + rows, :]
            slabs.append(acc)
        conv = jnp.concatenate(slabs, axis=-1)
        cb16 = conv.astype(BF16)
        ga = jnp.dot(cb16, wa_ref[n], preferred_element_type=F32) + ba_ref[j:j + 1, cs]
        gx = jnp.dot(cb16, wx_ref[n], preferred_element_type=F32) + bx_ref[j:j + 1, cs]
        a, u = _lru_decay_input(conv, ga, gx, lam_ref[j:j + 1, cs])

        h = h_s[:, cs]
        for t in range(tr):
            h = a[t * nb:(t + 1) * nb, :] * h + u[t * nb:(t + 1) * nb, :]
            for lt in range(lanes_per_blk):
                y_t[n * lanes_per_blk + lt, t * nb:(t + 1) * nb, :] = (
                    h[:, lt * LANES:(lt + 1) * LANES])
        h_s[:, cs] = h
        ho_ref[:, cs] = h

        y = jnp.concatenate(
            [jnp.concatenate([y_t[n * lanes_per_blk + lt, pl.ds(b, tr, stride=nb), :]
                              for lt in range(lanes_per_blk)], axis=-1)
             for b in range(nb)], axis=0)
        out = out + jnp.dot((jax.nn.gelu(gate) * y).astype(BF16), wout_ref[cs, :],
                            preferred_element_type=F32)
    tail = rec_t[:, rows:rows + hrows, :]
    co_ref[...] = tail
    rec_t[:, 0:hrows, :] = tail
    xo_ref[...] = (x + out).reshape(nb, tr, d)


def _p_cmod_kernel(x_ref, nm_ref, pw1_ref, b1_ref, dw_ref, dwb_ref, lng_ref, lnb_ref, pw2_ref,
                   xo_ref, so_ref, g_t, c_t, *, layer, j):
    nb, tr, d = x_ref.shape
    rows = nb * tr
    kw = dw_ref.shape[0]
    hist = kw - 1
    hrows = hist * nb
    lanes_per_col = CMOD_COL_TILE // LANES
    tc = pl.program_id(0)

    @pl.when(tc == 0)
    def _():
        g_t[:, 0:hrows, :] = jnp.zeros((d // LANES, hrows, LANES), F32)

    x = x_ref[...].reshape(rows, d)
    hn = _rms(x, nm_ref[layer:layer + 1, :]).astype(BF16)
    def pw1(c):
        lo = c * CMOD_COL_TILE
        cs = slice(lo, lo + CMOD_COL_TILE)
        gs = slice(d + lo, d + lo + CMOD_COL_TILE)
        return (jnp.dot(hn, pw1_ref[:, cs], preferred_element_type=F32) + b1_ref[j:j + 1, cs],
                jnp.dot(hn, pw1_ref[:, gs], preferred_element_type=F32) + b1_ref[j:j + 1, gs])

    n_col = d // CMOD_COL_TILE
    nxt = pw1(0)
    for c in range(n_col):
        a, bg = nxt
        if c + 1 < n_col:
            nxt = pw1(c + 1)
        g = a * jax.nn.sigmoid(bg)
        for lt in range(lanes_per_col):
            l = c * lanes_per_col + lt
            ls = slice(l * LANES, (l + 1) * LANES)
            for b in range(nb):
                g_t[l, pl.ds(hrows + b, tr, stride=nb), :] = (
                    g[b * tr:(b + 1) * tr, lt * LANES:(lt + 1) * LANES])
            for rb in range(rows // CMOD_ROW_TILE):
                r0 = hrows + rb * CMOD_ROW_TILE
                acc = dwb_ref[j:j + 1, ls] + dw_ref[hist:kw, ls] * g_t[l, r0:r0 + CMOD_ROW_TILE, :]
                for s in range(1, kw):
                    acc = acc + (dw_ref[hist - s:kw - s, ls]
                                 * g_t[l, r0 - s * nb:r0 - s * nb + CMOD_ROW_TILE, :])
                c_t[l, rb * CMOD_ROW_TILE:(rb + 1) * CMOD_ROW_TILE, :] = acc
    tail = g_t[:, rows:rows + hrows, :]
    so_ref[...] = tail
    g_t[:, 0:hrows, :] = tail

    conv = jnp.concatenate(
        [jnp.concatenate([c_t[l, pl.ds(b, tr, stride=nb), :] for l in range(d // LANES)], axis=-1)
         for b in range(nb)], axis=0)
    c = jax.nn.silu(_layer_norm(conv, lng_ref[j:j + 1, :], lnb_ref[j:j + 1, :]))
    xo_ref[...] = (x + _mm(c, pw2_ref[...])).reshape(nb, tr, d)


def _p_xattn_kernel(x_ref, nm_ref, wq_ref, kt_ref, v_ref, wo_ref, xo_ref, *, layer):
    tt, d = x_ref.shape
    dh = d // XA_HEADS
    x = x_ref[...]
    q = _mm(_rms(x, nm_ref[layer:layer + 1, :]), wq_ref[...]).astype(BF16)
    cols = [slice(h * dh, (h + 1) * dh) for h in range(XA_HEADS)]
    scores = [jnp.dot(q[:, cs], kt_ref[cs, :], preferred_element_type=F32) * (dh ** -0.5)
              for cs in cols]
    heads = []
    for s, cs in zip(scores, cols):
        e = jnp.exp(s - jnp.max(s, axis=-1, keepdims=True))
        p = e / jnp.sum(e, axis=-1, keepdims=True)
        heads.append(jnp.dot(p.astype(BF16), v_ref[:, cs], preferred_element_type=F32))
    o = jnp.concatenate(heads, axis=-1)
    xo_ref[...] = x + _mm(o, wo_ref[...])


def _p_ffn_kernel(x_ref, nm_ref, wup_ref, cw_ref, cb_ref, wdn_ref, nf_ref,
                  xo_ref, so_ref, buf, *, layer, final):
    tt, d = x_ref.shape
    dff = wdn_ref.shape[0]
    kw = cw_ref.shape[0]
    hist = kw - 1
    tc = pl.program_id(1)

    @pl.when(tc == 0)
    def _():
        buf[0:SUBLANES, :] = jnp.zeros((SUBLANES, dff), F32)

    x = x_ref[...]
    hn = _rms(x, nm_ref[layer:layer + 1, :]).astype(BF16)
    def up(c):
        lo = c * FFN_COL_TILE
        return (jnp.dot(hn, wup_ref[:, lo:lo + FFN_COL_TILE], preferred_element_type=F32),
                jnp.dot(hn, wup_ref[:, dff + lo:dff + lo + FFN_COL_TILE],
                        preferred_element_type=F32))

    n_col = dff // FFN_COL_TILE
    acc = jnp.zeros((tt, d), F32)
    nxt = up(0)
    for c in range(n_col):
        cs = slice(c * FFN_COL_TILE, (c + 1) * FFN_COL_TILE)
        g, u = nxt
        if c + 1 < n_col:
            nxt = up(c + 1)
        buf[SUBLANES:SUBLANES + tt, cs] = g
        conv = cb_ref[layer:layer + 1, cs] + cw_ref[hist:kw, cs] * g
        for k in range(hist):
            off = SUBLANES - hist + k
            conv = conv + cw_ref[k:k + 1, cs] * buf[off:off + tt, cs]
        act = (jax.nn.gelu(conv) * u).astype(BF16)
        acc = acc + jnp.dot(act, wdn_ref[cs, :], preferred_element_type=F32)
    tail = buf[SUBLANES + tt - hist:SUBLANES + tt, :]
    so_ref[...] = tail
    buf[SUBLANES - hist:SUBLANES, :] = tail

    xn = x + acc
    if final:
        xn = _rms(xn, nf_ref[...])
    xo_ref[...] = xn


def _s_lru_kernel(x_ref, h0_ref, cst_ref, nm_ref, win_ref, cw_ref, cb_ref, wa_ref, ba_ref,
                  wx_ref, bx_ref, lam_ref, wout_ref, nxa_ref, wq_ref,
                  xo_ref, q_ref, ho_ref, co_ref, *, layer, j):
    d = x_ref.shape[-1]
    kw = cw_ref.shape[0]
    hist = kw - 1
    x = x_ref[...]
    hn = _rms(x, nm_ref[layer:layer + 1, :]).astype(BF16)
    gate = jnp.dot(hn, win_ref[:, :d], preferred_element_type=F32)
    rec = jnp.dot(hn, win_ref[:, d:], preferred_element_type=F32)

    conv = cb_ref[j:j + 1, :] + cw_ref[hist:kw, :] * rec
    for k in range(hist):
        conv = conv + cw_ref[k:k + 1, :] * cst_ref[:, k * d:(k + 1) * d]
    co_ref[:, :(hist - 1) * d] = cst_ref[:, d:]
    co_ref[:, (hist - 1) * d:] = rec

    a, u = _lru_gates(conv, wa_ref, ba_ref[j:j + 1, :], wx_ref, bx_ref[j:j + 1, :],
                      lam_ref[j:j + 1, :])
    h = a * h0_ref[...] + u
    ho_ref[...] = h
    x1 = x + _mm(jax.nn.gelu(gate) * h, wout_ref[...])
    xo_ref[...] = x1
    q_ref[...] = _mm(_rms(x1, nxa_ref[layer:layer + 1, :]), wq_ref[...])


def _s_attn_kernel(q_ref, k_ref, v_ref, o_ref):
    nb, _, dh = q_ref.shape
    for b in range(nb):
        s = jnp.sum(k_ref[b] * q_ref[b][None], axis=-1, keepdims=True) * (dh ** -0.5)
        e = jnp.exp(s - jnp.max(s, axis=0, keepdims=True))
        p = e / jnp.sum(e, axis=0, keepdims=True)
        o_ref[b] = jnp.sum(p * v_ref[b], axis=0)


def _s_ffn_kernel(x_ref, o_ref, wo_ref, nm_ref, wup_ref, cw_ref, cb_ref, wdn_ref, st_ref, nf_ref,
                  xo_ref, so_ref, *, layer, final):
    dff = wdn_ref.shape[0]
    kw = cw_ref.shape[0]
    hist = kw - 1
    x = x_ref[...] + _mm(o_ref[...], wo_ref[...])
    hn = _rms(x, nm_ref[layer:layer + 1, :]).astype(BF16)
    g = jnp.dot(hn, wup_ref[:, :dff], preferred_element_type=F32)
    u = jnp.dot(hn, wup_ref[:, dff:], preferred_element_type=F32)
    conv = cb_ref[layer:layer + 1, :] + cw_ref[hist:kw, :] * g
    for k in range(hist):
        conv = conv + cw_ref[k:k + 1, :] * st_ref[:, k * dff:(k + 1) * dff]
    so_ref[:, :(hist - 1) * dff] = st_ref[:, dff:]
    so_ref[:, (hist - 1) * dff:] = g
    xn = x + _mm(jax.nn.gelu(conv) * u, wdn_ref[...])
    if final:
        xn = _rms(xn, nf_ref[...])
    xo_ref[...] = xn


def _s_cmod_pre_kernel(x_ref, nm_ref, pw1_ref, b1_ref, g_ref, *, layer, j):
    d = x_ref.shape[-1]
    hn = _rms(x_ref[...], nm_ref[layer:layer + 1, :])
    g_ref[...] = _glu(_mm(hn, pw1_ref[...]) + b1_ref[j:j + 1, :], d)


def _s_cmod_conv_kernel(st_ref, g_ref, dw_ref, dwb_ref, lng_ref, lnb_ref, c_ref, so_ref, *, j):
    d = g_ref.shape[-1]
    kw = dw_ref.shape[0]
    hist = kw - 1
    g = g_ref[...]
    c = dwb_ref[j:j + 1, :] + dw_ref[hist:kw, :] * g
    for k in range(hist):
        c = c + dw_ref[k:k + 1, :] * st_ref[:, k * d:(k + 1) * d]
    so_ref[:, :(hist - 1) * d] = st_ref[:, d:]
    so_ref[:, (hist - 1) * d:] = g
    c_ref[...] = jax.nn.silu(_layer_norm(c, lng_ref[j:j + 1, :], lnb_ref[j:j + 1, :]))


def _s_cmod_post_kernel(x_ref, c_ref, pw2_ref, nxa_ref, wq_ref, xo_ref, q_ref, *, layer):
    x1 = x_ref[...] + _mm(c_ref[...], pw2_ref[...])
    xo_ref[...] = x1
    q_ref[...] = _mm(_rms(x1, nxa_ref[layer:layer + 1, :]), wq_ref[...])


def _params(n_grid):
    return pltpu.CompilerParams(dimension_semantics=("arbitrary",) * n_grid,
                                vmem_limit_bytes=VMEM_LIMIT)


def _resident(shape, lead=None):
    if lead is None:
        nd = len(shape)
        return pl.BlockSpec(shape, lambda *_: (0,) * nd, pipeline_mode=pl.Buffered(1))
    nd = len(shape) - 1
    return pl.BlockSpec((None,) + tuple(shape[1:]), lambda *_: (lead,) + (0,) * nd,
                        pipeline_mode=pl.Buffered(1))


_whole = _resident


def _whole_out(shape):
    nd = len(shape)
    return pl.BlockSpec(shape, lambda *_: (0,) * nd)


def _sds(shape, dtype=F32):
    return jax.ShapeDtypeStruct(shape, dtype)


def _prompt_memkv(mem, mem_norm3, wkv, heads):
    depth = wkv.shape[0]
    b, m, d = mem.shape
    dh = d // heads
    return pl.pallas_call(
        _p_memkv_kernel,
        grid=(depth, b),
        in_specs=[pl.BlockSpec((None, m, d), lambda i, n: (n, 0, 0)),
                  pl.BlockSpec((None, 1, d), lambda i, n: (i, 0, 0)),
                  pl.BlockSpec((None, d, 2 * d), lambda i, n: (i, 0, 0))],
        out_specs=[pl.BlockSpec((None, None, m, heads, dh), lambda i, n: (i, n, 0, 0, 0)),
                   pl.BlockSpec((None, None, m, heads, dh), lambda i, n: (i, n, 0, 0, 0)),
                   pl.BlockSpec((None, None, d, m), lambda i, n: (i, n, 0, 0)),
                   pl.BlockSpec((None, None, m, d), lambda i, n: (i, n, 0, 0))],
        out_shape=[_sds((depth, b, m, heads, dh)), _sds((depth, b, m, heads, dh)),
                   _sds((depth, b, d, m), BF16), _sds((depth, b, m, d), BF16)],
        compiler_params=_params(2),
        name="prompt_memkv",
    )(mem, mem_norm3, wkv)


def _x_spec(tt, d):
    return pl.BlockSpec((None, tt, d), lambda n, t: (n, t, 0))


def _state_spec(rows, cols):
    return pl.BlockSpec((None, rows, cols), lambda n, t: (n, 0, 0))


def _xt_spec(b, tr, d):
    return pl.BlockSpec((b, tr, d), lambda t: (0, t, 0))


def _unslab_history(s, b):
    nl, hrows, lanes = s.shape
    return s.reshape(nl, hrows // b, b, lanes).transpose(2, 1, 0, 3).reshape(b, hrows // b, -1)


def _prompt_lru(x, layer, j, w):
    b, t, d = x.shape
    tr = MIXER_TIME_ROWS
    hist = w['lru_conv_w'].shape[1] - 1
    assert b == SUBLANES and tr % SUBLANES == 0 and tr >= hist and d % LANES == 0
    nl, hrows = d // LANES, hist * b
    ins = [(x, _xt_spec(b, tr, d)),
           (w['norm_mix'], _resident(w['norm_mix'].shape)),
           (w['lru_w_in'], _resident(w['lru_w_in'].shape, j)),
           (w['lru_conv_w'], _resident(w['lru_conv_w'].shape, j)),
           (w['lru_conv_b'], _resident(w['lru_conv_b'].shape)),
           (w['lru_wa'], _resident(w['lru_wa'].shape, j)),
           (w['lru_ba'], _resident(w['lru_ba'].shape)),
           (w['lru_wx'], _resident(w['lru_wx'].shape, j)),
           (w['lru_bx'], _resident(w['lru_bx'].shape)),
           (w['lru_lambda'], _resident(w['lru_lambda'].shape)),
           (w['lru_w_out'], _resident(w['lru_w_out'].shape, j))]
    return pl.pallas_call(
        functools.partial(_p_lru_kernel, layer=layer, j=j),
        grid=(t // tr,),
        in_specs=[s for _, s in ins],
        out_specs=[_xt_spec(b, tr, d), _whole_out((b, d)), _whole_out((nl, hrows, LANES))],
        out_shape=[_sds((b, t, d)), _sds((b, d)), _sds((nl, hrows, LANES))],
        scratch_shapes=[pltpu.VMEM((nl, hrows + b * tr, LANES), F32),
                        pltpu.VMEM((nl, b * tr, LANES), F32),
                        pltpu.VMEM((b, d), F32)],
        compiler_params=_params(1),
        name="prompt_lru",
    )(*[a for a, _ in ins])


def _prompt_cmod(x, layer, j, w):
    b, t, d = x.shape
    tr = MIXER_TIME_ROWS
    hist = w['cm_dw_w'].shape[1] - 1
    assert b == SUBLANES and tr % SUBLANES == 0 and tr >= hist and d % CMOD_COL_TILE == 0
    nl, hrows = d // LANES, hist * b
    ins = [(x, _xt_spec(b, tr, d)),
           (w['norm_mix'], _resident(w['norm_mix'].shape)),
           (w['cm_w_pw1'], _resident(w['cm_w_pw1'].shape, j)),
           (w['cm_b_pw1'], _resident(w['cm_b_pw1'].shape)),
           (w['cm_dw_w'], _resident(w['cm_dw_w'].shape, j)),
           (w['cm_dw_b'], _resident(w['cm_dw_b'].shape)),
           (w['cm_ln_g'], _resident(w['cm_ln_g'].shape)),
           (w['cm_ln_b'], _resident(w['cm_ln_b'].shape)),
           (w['cm_w_pw2'], _resident(w['cm_w_pw2'].shape, j))]
    return pl.pallas_call(
        functools.partial(_p_cmod_kernel, layer=layer, j=j),
        grid=(t // tr,),
        in_specs=[s for _, s in ins],
        out_specs=[_xt_spec(b, tr, d), _whole_out((nl, hrows, LANES))],
        out_shape=[_sds((b, t, d)), _sds((nl, hrows, LANES))],
        scratch_shapes=[pltpu.VMEM((nl, hrows + b * tr, LANES), F32),
                        pltpu.VMEM((nl, b * tr, LANES), F32)],
        compiler_params=_params(1),
        name="prompt_cmod",
    )(*[a for a, _ in ins])


def _prompt_xattn(x, layer, kt, vb, w):
    b, t, d = x.shape
    tt = XATTN_T_TILE
    m = vb.shape[2]
    ins = [(x, _x_spec(tt, d)),
           (w['norm_xa'], _resident(w['norm_xa'].shape)),
           (w['xa_w_q'], _resident(w['xa_w_q'].shape, layer)),
           (kt, pl.BlockSpec((None, None, d, m), lambda n, t_: (layer, n, 0, 0))),
           (vb, pl.BlockSpec((None, None, m, d), lambda n, t_: (layer, n, 0, 0))),
           (w['xa_w_o'], _resident(w['xa_w_o'].shape, layer))]
    return pl.pallas_call(
        functools.partial(_p_xattn_kernel, layer=layer),
        grid=(b, t // tt),
        in_specs=[s for _, s in ins],
        out_specs=_x_spec(tt, d),
        out_shape=_sds((b, t, d)),
        compiler_params=_params(2),
        name="prompt_xattn",
    )(*[a for a, _ in ins])


def _prompt_ffn(x, layer, final, w):
    b, t, d = x.shape
    tt = FFN_T_TILE
    dff = w['ffn_w_down'].shape[1]
    hist = w['ffn_conv_w'].shape[1] - 1
    ins = [(x, _x_spec(tt, d)),
           (w['norm_ffn'], _resident(w['norm_ffn'].shape)),
           (w['ffn_w_up'], _resident(w['ffn_w_up'].shape, layer)),
           (w['ffn_conv_w'], _resident(w['ffn_conv_w'].shape, layer)),
           (w['ffn_conv_b'], _resident(w['ffn_conv_b'].shape)),
           (w['ffn_w_down'], _resident(w['ffn_w_down'].shape, layer)),
           (w['norm_final'], _resident(w['norm_final'].shape))]
    return pl.pallas_call(
        functools.partial(_p_ffn_kernel, layer=layer, final=final),
        grid=(b, t // tt),
        in_specs=[s for _, s in ins],
        out_specs=[_x_spec(tt, d), _state_spec(hist, dff)],
        out_shape=[_sds((b, t, d)), _sds((b, hist, dff))],
        scratch_shapes=[pltpu.VMEM((SUBLANES + tt, dff), F32)],
        compiler_params=_params(2),
        name="prompt_ffn",
    )(*[a for a, _ in ins])


def _sample_lru(x, h0, cst, layer, j, w):
    r, d = x.shape
    ins = [(x, _whole(x.shape)), (h0, _whole(h0.shape, j)), (cst, _whole(cst.shape, j)),
           (w['norm_mix'], _whole(w['norm_mix'].shape)),
           (w['lru_w_in'], _whole(w['lru_w_in'].shape, j)),
           (w['lru_conv_w'], _whole(w['lru_conv_w'].shape, j)),
           (w['lru_conv_b'], _whole(w['lru_conv_b'].shape)),
           (w['lru_wa'], _whole(w['lru_wa'].shape, j)),
           (w['lru_ba'], _whole(w['lru_ba'].shape)),
           (w['lru_wx'], _whole(w['lru_wx'].shape, j)),
           (w['lru_bx'], _whole(w['lru_bx'].shape)),
           (w['lru_lambda'], _whole(w['lru_lambda'].shape)),
           (w['lru_w_out'], _whole(w['lru_w_out'].shape, j)),
           (w['norm_xa'], _whole(w['norm_xa'].shape)),
           (w['xa_w_q'], _whole(w['xa_w_q'].shape, layer))]
    return pl.pallas_call(
        functools.partial(_s_lru_kernel, layer=layer, j=j),
        in_specs=[s for _, s in ins],
        out_specs=[_whole_out((r, d)), _whole_out((r, d)), _whole_out((r, d)),
                   _whole_out(cst.shape[1:])],
        out_shape=[_sds((r, d)), _sds((r, d)), _sds((r, d)), _sds(cst.shape[1:])],
        grid=(1,),
        compiler_params=_params(1),
        name="sample_lru",
    )(*[a for a, _ in ins])


def _sample_attn(q, k, v, layer):
    r, d = q.shape
    _, _, m, nh, dh = k.shape
    nb = SAMPLE_ATTN_BLOCK
    kv_spec = pl.BlockSpec((None, nb, m, nh, dh), lambda n: (layer, n, 0, 0, 0))
    row_spec = pl.BlockSpec((nb, nh, dh), lambda n: (n, 0, 0))
    o = pl.pallas_call(
        _s_attn_kernel,
        grid=(r // nb,),
        in_specs=[row_spec, kv_spec, kv_spec],
        out_specs=row_spec,
        out_shape=_sds((r, nh, dh)),
        compiler_params=_params(1),
        name="sample_attn",
    )(q.reshape(r, nh, dh), k, v)
    return o.reshape(r, d)


def _sample_ffn(x, o, st, layer, final, w):
    r, d = x.shape
    ins = [(x, _whole(x.shape)), (o, _whole(o.shape)),
           (w['xa_w_o'], _whole(w['xa_w_o'].shape, layer)),
           (w['norm_ffn'], _whole(w['norm_ffn'].shape)),
           (w['ffn_w_up'], _whole(w['ffn_w_up'].shape, layer)),
           (w['ffn_conv_w'], _whole(w['ffn_conv_w'].shape, layer)),
           (w['ffn_conv_b'], _whole(w['ffn_conv_b'].shape)),
           (w['ffn_w_down'], _whole(w['ffn_w_down'].shape, layer)),
           (st, _whole(st.shape, layer)),
           (w['norm_final'], _whole(w['norm_final'].shape))]
    return pl.pallas_call(
        functools.partial(_s_ffn_kernel, layer=layer, final=final),
        in_specs=[s for _, s in ins],
        out_specs=[_whole_out((r, d)), _whole_out(st.shape[1:])],
        out_shape=[_sds((r, d)), _sds(st.shape[1:])],
        grid=(1,),
        compiler_params=_params(1),
        name="sample_ffn",
    )(*[a for a, _ in ins])


def _sample_cmod(x, st, layer, j, w):
    r, d = x.shape
    ins = [(x, _whole(x.shape)),
           (w['norm_mix'], _whole(w['norm_mix'].shape)),
           (w['cm_w_pw1'], _whole(w['cm_w_pw1'].shape, j)),
           (w['cm_b_pw1'], _whole(w['cm_b_pw1'].shape))]
    g = pl.pallas_call(
        functools.partial(_s_cmod_pre_kernel, layer=layer, j=j),
        in_specs=[s for _, s in ins],
        out_specs=_whole_out((r, d)),
        out_shape=_sds((r, d)),
        grid=(1,),
        compiler_params=_params(1),
        name="sample_cmod_pre",
    )(*[a for a, _ in ins])

    rows = SAMPLE_CONV_ROWS
    sc = st.shape[-1]
    ins = [(st, pl.BlockSpec((None, rows, sc), lambda n: (j, n, 0))),
           (g, pl.BlockSpec((rows, d), lambda n: (n, 0))),
           (w['cm_dw_w'], _resident(w['cm_dw_w'].shape, j)),
           (w['cm_dw_b'], _resident(w['cm_dw_b'].shape)),
           (w['cm_ln_g'], _resident(w['cm_ln_g'].shape)),
           (w['cm_ln_b'], _resident(w['cm_ln_b'].shape))]
    c, new_st = pl.pallas_call(
        functools.partial(_s_cmod_conv_kernel, j=j),
        grid=(r // rows,),
        in_specs=[s for _, s in ins],
        out_specs=[pl.BlockSpec((rows, d), lambda n: (n, 0)),
                   pl.BlockSpec((rows, sc), lambda n: (n, 0))],
        out_shape=[_sds((r, d)), _sds((r, sc))],
        compiler_params=_params(1),
        name="sample_cmod_conv",
    )(*[a for a, _ in ins])

    ins = [(x, _whole(x.shape)), (c, _whole(c.shape)),
           (w['cm_w_pw2'], _whole(w['cm_w_pw2'].shape, j)),
           (w['norm_xa'], _whole(w['norm_xa'].shape)),
           (w['xa_w_q'], _whole(w['xa_w_q'].shape, layer))]
    x1, q = pl.pallas_call(
        functools.partial(_s_cmod_post_kernel, layer=layer),
        in_specs=[s for _, s in ins],
        out_specs=[_whole_out((r, d)), _whole_out((r, d))],
        out_shape=[_sds((r, d)), _sds((r, d))],
        grid=(1,),
        compiler_params=_params(1),
        name="sample_cmod_post",
    )(*[a for a, _ in ins])
    return x1, q, new_st


_MATMUL_WEIGHTS = ('lru_w_in', 'lru_wa', 'lru_wx', 'lru_w_out', 'cm_w_pw1', 'cm_w_pw2',
                   'xa_w_q', 'xa_w_kv', 'xa_w_o', 'ffn_w_up', 'ffn_w_down')


def kernel(x_prompt, x_sample, state_lru_h, state_lru_conv, state_cmod_conv, state_ffn_conv, cache_mem_k, cache_mem_v, mem_prompt, norm_mix, norm_xa, norm_ffn, norm_final, lru_w_in, lru_conv_w, lru_conv_b, lru_wa, lru_ba, lru_wx, lru_bx, lru_lambda, lru_w_out, cm_w_pw1, cm_b_pw1, cm_dw_w, cm_dw_b, cm_ln_g, cm_ln_b, cm_w_pw2, mem_norm, xa_w_q, xa_w_kv, xa_w_o, ffn_w_up, ffn_conv_w, ffn_conv_b, ffn_w_down):
    w = dict(norm_mix=norm_mix, norm_xa=norm_xa, norm_ffn=norm_ffn,
             norm_final=norm_final.reshape(1, -1),
             lru_w_in=lru_w_in, lru_conv_w=lru_conv_w, lru_conv_b=lru_conv_b, lru_wa=lru_wa,
             lru_ba=lru_ba, lru_wx=lru_wx, lru_bx=lru_bx, lru_lambda=lru_lambda,
             lru_w_out=lru_w_out, cm_w_pw1=cm_w_pw1, cm_b_pw1=cm_b_pw1, cm_dw_w=cm_dw_w,
             cm_dw_b=cm_dw_b, cm_ln_g=cm_ln_g, cm_ln_b=cm_ln_b, cm_w_pw2=cm_w_pw2,
             xa_w_q=xa_w_q, xa_w_kv=xa_w_kv, xa_w_o=xa_w_o,
             ffn_w_up=ffn_w_up, ffn_conv_w=ffn_conv_w, ffn_conv_b=ffn_conv_b,
             ffn_w_down=ffn_w_down)
    for name in _MATMUL_WEIGHTS:
        w[name] = w[name].astype(BF16)

    depth = norm_mix.shape[0]
    bsz, _, d = x_prompt.shape
    n_mem = mem_prompt.shape[1]
    heads = cache_mem_k.shape[3]

    k_p, v_p, kt_p, vb_p = _prompt_memkv(mem_prompt, mem_norm.reshape(depth, 1, d),
                                         w['xa_w_kv'], heads)
    x = x_prompt
    p_h, p_lconv, p_cconv, p_fconv = [], [], [], []
    for i in range(depth):
        j = i // 2
        if i % 2 == 0:
            x, h_last, cbuf = _prompt_lru(x, i, j, w)
            p_h.append(h_last)
            p_lconv.append(_unslab_history(cbuf, bsz))
        else:
            x, cbuf = _prompt_cmod(x, i, j, w)
            p_cconv.append(_unslab_history(cbuf, bsz))
        x = _prompt_xattn(x, i, kt_p, vb_p, w)
        x, fbuf = _prompt_ffn(x, i, i == depth - 1, w)
        p_fconv.append(fbuf)
    y_prompt = x

    r = x_sample.shape[0]
    xs = x_sample.reshape(r, d)
    lconv = state_lru_conv.reshape(state_lru_conv.shape[0], r, -1)
    cconv = state_cmod_conv.reshape(state_cmod_conv.shape[0], r, -1)
    fconv = state_ffn_conv.reshape(depth, r, -1)
    s_h, s_lconv, s_cconv, s_fconv = [], [], [], []
    for i in range(depth):
        j = i // 2
        if i % 2 == 0:
            xs, q, h_new, c_new = _sample_lru(xs, state_lru_h, lconv, i, j, w)
            s_h.append(h_new)
            s_lconv.append(c_new.reshape(r, -1, d))
        else:
            xs, q, c_new = _sample_cmod(xs, cconv, i, j, w)
            s_cconv.append(c_new.reshape(r, -1, d))
        o = _sample_attn(q, cache_mem_k, cache_mem_v, i)
        xs, f_new = _sample_ffn(xs, o, fconv, i, i == depth - 1, w)
        s_fconv.append(f_new.reshape(r, state_ffn_conv.shape[2], -1))
    y_sample = xs.reshape(x_sample.shape)

    return (y_prompt, y_sample,
            jnp.stack(p_h), jnp.stack(p_lconv), jnp.stack(p_cconv), jnp.stack(p_fconv),
            k_p, v_p,
            jnp.stack(s_h), jnp.stack(s_lconv), jnp.stack(s_cconv), jnp.stack(s_fconv))
```

```python
import functools

import jax
import jax.numpy as jnp
from jax import lax
from jax.experimental import pallas as pl
from jax.experimental.pallas import tpu as pltpu

F32 = jnp.float32
BF16 = jnp.bfloat16

EPS = 1e-6
RG_C = 8.0
N_RG_BLOCKS = 4
XA_HEADS = 4
SUBLANES = 8
LANES = 128
VMEM_LIMIT = 56 << 20

MIXER_TIME_ROWS = 64
XATTN_T_TILE = 512
FFN_T_TILE = 512
FFN_COL_TILE = 1536
CMOD_COL_TILE = 256
CMOD_ROW_TILE = 64
SAMPLE_ATTN_BLOCK = 4
SAMPLE_CONV_ROWS = 16


def _rms(x, g):
    return x * lax.rsqrt(jnp.mean(x * x, axis=-1, keepdims=True) + EPS) * g


def _layer_norm(x, g, b):
    mu = jnp.mean(x, axis=-1, keepdims=True)
    xc = x - mu
    var = jnp.mean(xc * xc, axis=-1, keepdims=True)
    return xc * lax.rsqrt(var + EPS) * g + b


def _mm(a, w):
    return jnp.dot(a.astype(BF16), w, preferred_element_type=F32)


def _softplus(z):
    return jnp.maximum(z, 0.0) + jnp.log1p(jnp.exp(-jnp.abs(z)))


def _block_diag(xb, w_ref):
    blk = xb.shape[-1] // N_RG_BLOCKS
    return jnp.concatenate(
        [jnp.dot(xb[:, n * blk:(n + 1) * blk], w_ref[n], preferred_element_type=F32)
         for n in range(N_RG_BLOCKS)], axis=-1)


def _lru_decay_input(rec, ga, gx, lam):
    log_a = -RG_C * jax.nn.sigmoid(ga) * _softplus(-lam)
    a = jnp.exp(log_a)
    u = jnp.sqrt(1.0 - a * a) * jax.nn.sigmoid(gx) * rec
    return a, u


def _lru_gates(rec, wa_ref, ba, wx_ref, bx, lam):
    rb = rec.astype(BF16)
    return _lru_decay_input(rec, _block_diag(rb, wa_ref) + ba, _block_diag(rb, wx_ref) + bx, lam)


def _glu(ab, d):
    return ab[:, :d] * jax.nn.sigmoid(ab[:, d:])


def _p_memkv_kernel(m_ref, g_ref, wkv_ref, k_ref, v_ref, kt_ref, vb_ref):
    d = m_ref.shape[-1]
    hn = _rms(m_ref[...], g_ref[...]).astype(BF16)
    k = jnp.dot(hn, wkv_ref[:, :d], preferred_element_type=F32)
    v = jnp.dot(hn, wkv_ref[:, d:], preferred_element_type=F32)
    dh = d // k_ref.shape[1]
    for h in range(k_ref.shape[1]):
        k_ref[:, h, :] = k[:, h * dh:(h + 1) * dh]
        v_ref[:, h, :] = v[:, h * dh:(h + 1) * dh]
    kt_ref[...] = k.T.astype(BF16)
    vb_ref[...] = v.astype(BF16)


def _p_lru_kernel(x_ref, nm_ref, win_ref, cw_ref, cb_ref, wa_ref, ba_ref, wx_ref, bx_ref,
                  lam_ref, wout_ref,
                  xo_ref, ho_ref, co_ref,
                  rec_t, y_t, h_s, *, layer, j):
    nb, tr, d = x_ref.shape
    rows = nb * tr
    kw = cw_ref.shape[0]
    hist = kw - 1
    hrows = hist * nb
    nl = d // LANES
    tc = pl.program_id(0)

    @pl.when(tc == 0)
    def _():
        rec_t[:, 0:hrows, :] = jnp.zeros((nl, hrows, LANES), F32)
        h_s[...] = jnp.zeros_like(h_s)

    x = x_ref[...].reshape(rows, d)
    hn = _rms(x, nm_ref[layer:layer + 1, :]).astype(BF16)
    cblk = d // N_RG_BLOCKS
    lanes_per_blk = cblk // LANES

    def in_proj(n):
        lo = n * cblk
        return (jnp.dot(hn, win_ref[:, lo:lo + cblk], preferred_element_type=F32),
                jnp.dot(hn, win_ref[:, d + lo:d + lo + cblk], preferred_element_type=F32))

    out = jnp.zeros((rows, d), F32)
    nxt = in_proj(0)
    for n in range(N_RG_BLOCKS):
        gate, rec = nxt
        if n + 1 < N_RG_BLOCKS:
            nxt = in_proj(n + 1)
        cs = slice(n * cblk, (n + 1) * cblk)
        slabs = []
        for lt in range(lanes_per_blk):
            l = n * lanes_per_blk + lt
            ls = slice(l * LANES, (l + 1) * LANES)
            for b in range(nb):
                rec_t[l, pl.ds(hrows + b, tr, stride=nb), :] = (
                    rec[b * tr:(b + 1) * tr, lt * LANES:(lt + 1) * LANES])
            acc = cb_ref[j:j + 1, ls] + cw_ref[hist:kw, ls] * rec_t[l, hrows:hrows + rows, :]
            for k in range(hist):
                acc = acc + cw_ref[k:k + 1, ls] * rec_t[l, k * nb:k * nb + rows, :]
            slabs.append(acc)
        conv = jnp.concatenate(slabs, axis=-1)
        cb16 = conv.astype(BF16)
        ga = jnp.dot(cb16, wa_ref[n], preferred_element_type=F32) + ba_ref[j:j + 1, cs]
        gx = jnp.dot(cb16, wx_ref[n], preferred_element_type=F32) + bx_ref[j:j + 1, cs]
        a, u = _lru_decay_input(conv, ga, gx, lam_ref[j:j + 1, cs])

        h = h_s[:, cs]
        for t in range(tr):
            h = a[t * nb:(t + 1) * nb, :] * h + u[t * nb:(t + 1) * nb, :]
            for lt in range(lanes_per_blk):
                y_t[n * lanes_per_blk + lt, t * nb:(t + 1) * nb, :] = (
                    h[:, lt * LANES:(lt + 1) * LANES])
        h_s[:, cs] = h
        ho_ref[:, cs] = h

        y = jnp.concatenate(
            [jnp.concatenate([y_t[n * lanes_per_blk + lt, pl.ds(b, tr, stride=nb), :]
                              for lt in range(lanes_per_blk)], axis=-1)
             for b in range(nb)], axis=0)
        out = out + jnp.dot((jax.nn.gelu(gate) * y).astype(BF16), wout_ref[cs, :],
                            preferred_element_type=F32)
    tail = rec_t[:, rows:rows + hrows, :]
    co_ref[...] = tail
    rec_t[:, 0:hrows, :] = tail
    xo_ref[...] = (x + out).reshape(nb, tr, d)


def _p_cmod_kernel(x_ref, nm_ref, pw1_ref, b1_ref, dw_ref, dwb_ref, lng_ref, lnb_ref, pw2_ref,
                   xo_ref, so_ref, g_t, c_t, *, layer, j):
    nb, tr, d = x_ref.shape
    rows = nb * tr
    kw = dw_ref.shape[0]
    hist = kw - 1
    hrows = hist * nb
    lanes_per_col = CMOD_COL_TILE // LANES
    tc = pl.program_id(0)

    @pl.when(tc == 0)
    def _():
        g_t[:, 0:hrows, :] = jnp.zeros((d // LANES, hrows, LANES), F32)

    x = x_ref[...].reshape(rows, d)
    hn = _rms(x, nm_ref[layer:layer + 1, :]).astype(BF16)
    def pw1(c):
        lo = c * CMOD_COL_TILE
        cs = slice(lo, lo + CMOD_COL_TILE)
        gs = slice(d + lo, d + lo + CMOD_COL_TILE)
        return (jnp.dot(hn, pw1_ref[:, cs], preferred_element_type=F32) + b1_ref[j:j + 1, cs],
                jnp.dot(hn, pw1_ref[:, gs], preferred_element_type=F32) + b1_ref[j:j + 1, gs])

    n_col = d // CMOD_COL_TILE
    nxt = pw1(0)
    for c in range(n_col):
        a, bg = nxt
        if c + 1 < n_col:
            nxt = pw1(c + 1)
        g = a * jax.nn.sigmoid(bg)
        for lt in range(lanes_per_col):
            l = c * lanes_per_col + lt
            ls = slice(l * LANES, (l + 1) * LANES)
            for b in range(nb):
                g_t[l, pl.ds(hrows + b, tr, stride=nb), :] = (
                    g[b * tr:(b + 1) * tr, lt * LANES:(lt + 1) * LANES])
            for rb in range(rows // CMOD_ROW_TILE):
                r0 = hrows + rb * CMOD_ROW_TILE
                acc = dwb_ref[j:j + 1, ls] + dw_ref[hist:kw, ls] * g_t[l, r0:r0 + CMOD_ROW_TILE, :]
                for s in range(1, kw):
                    acc = acc + (dw_ref[hist - s:kw - s, ls]
                                 * g_t[l, r0 - s * nb:r0 - s * nb + CMOD_ROW_TILE, :])
                c_t[l, rb * CMOD_ROW_TILE:(rb + 1) * CMOD_ROW_TILE, :] = acc
    tail = g_t[:, rows:rows + hrows, :]
    so_ref[...] = tail
    g_t[:, 0:hrows, :] = tail

    conv = jnp.concatenate(
        [jnp.concatenate([c_t[l, pl.ds(b, tr, stride=nb), :] for l in range(d // LANES)], axis=-1)
         for b in range(nb)], axis=0)
    c = jax.nn.silu(_layer_norm(conv, lng_ref[j:j + 1, :], lnb_ref[j:j + 1, :]))
    xo_ref[...] = (x + _mm(c, pw2_ref[...])).reshape(nb, tr, d)


def _p_xattn_kernel(x_ref, nm_ref, wq_ref, kt_ref, v_ref, wo_ref, xo_ref, *, layer):
    tt, d = x_ref.shape
    dh = d // XA_HEADS
    x = x_ref[...]
    q = _mm(_rms(x, nm_ref[layer:layer + 1, :]), wq_ref[...]).astype(BF16)
    cols = [slice(h * dh, (h + 1) * dh) for h in range(XA_HEADS)]
    scores = [jnp.dot(q[:, cs], kt_ref[cs, :], preferred_element_type=F32) * (dh ** -0.5)
              for cs in cols]
    heads = []
    for s, cs in zip(scores, cols):
        e = jnp.exp(s - jnp.max(s, axis=-1, keepdims=True))
        p = e / jnp.sum(e, axis=-1, keepdims=True)
        heads.append(jnp.dot(p.astype(BF16), v_ref[:, cs], preferred_element_type=F32))
    o = jnp.concatenate(heads, axis=-1)
    xo_ref[...] = x + _mm(o, wo_ref[...])


def _p_ffn_kernel(x_ref, nm_ref, wup_ref, cw_ref, cb_ref, wdn_ref, nf_ref,
                  xo_ref, so_ref, buf, *, layer, final):
    tt, d = x_ref.shape
    dff = wdn_ref.shape[0]
    kw = cw_ref.shape[0]
    hist = kw - 1
    tc = pl.program_id(1)

    @pl.when(tc == 0)
    def _():
        buf[0:SUBLANES, :] = jnp.zeros((SUBLANES, dff), F32)

    x = x_ref[...]
    hn = _rms(x, nm_ref[layer:layer + 1, :]).astype(BF16)
    def up(c):
        lo = c * FFN_COL_TILE
        return (jnp.dot(hn, wup_ref[:, lo:lo + FFN_COL_TILE], preferred_element_type=F32),
                jnp.dot(hn, wup_ref[:, dff + lo:dff + lo + FFN_COL_TILE],
                        preferred_element_type=F32))

    n_col = dff // FFN_COL_TILE
    acc = jnp.zeros((tt, d), F32)
    nxt = up(0)
    for c in range(n_col):
        cs = slice(c * FFN_COL_TILE, (c + 1) * FFN_COL_TILE)
        g, u = nxt
        if c + 1 < n_col:
            nxt = up(c + 1)
        buf[SUBLANES:SUBLANES + tt, cs] = g
        conv = cb_ref[layer:layer + 1, cs] + cw_ref[hist:kw, cs] * g
        for k in range(hist):
            off = SUBLANES - hist + k
            conv = conv + cw_ref[k:k + 1, cs] * buf[off:off + tt, cs]
        act = (jax.nn.gelu(conv) * u).astype(BF16)
        acc = acc + jnp.dot(act, wdn_ref[cs, :], preferred_element_type=F32)
    tail = buf[SUBLANES + tt - hist:SUBLANES + tt, :]
    so_ref[...] = tail
    buf[SUBLANES - hist:SUBLANES, :] = tail

    xn = x + acc
    if final:
        xn = _rms(xn, nf_ref[...])
    xo_ref[...] = xn


def _s_lru_kernel(x_ref, h0_ref, cst_ref, nm_ref, win_ref, cw_ref, cb_ref, wa_ref, ba_ref,
                  wx_ref, bx_ref, lam_ref, wout_ref, nxa_ref, wq_ref,
                  xo_ref, q_ref, ho_ref, co_ref, *, layer, j):
    d = x_ref.shape[-1]
    kw = cw_ref.shape[0]
    hist = kw - 1
    x = x_ref[...]
    hn = _rms(x, nm_ref[layer:layer + 1, :]).astype(BF16)
    gate = jnp.dot(hn, win_ref[:, :d], preferred_element_type=F32)
    rec = jnp.dot(hn, win_ref[:, d:], preferred_element_type=F32)

    conv = cb_ref[j:j + 1, :] + cw_ref[hist:kw, :] * rec
    for k in range(hist):
        conv = conv + cw_ref[k:k + 1, :] * cst_ref[:, k * d:(k + 1) * d]
    co_ref[:, :(hist - 1) * d] = cst_ref[:, d:]
    co_ref[:, (hist - 1) * d:] = rec

    a, u = _lru_gates(conv, wa_ref, ba_ref[j:j + 1, :], wx_ref, bx_ref[j:j + 1, :],
                      lam_ref[j:j + 1, :])
    h = a * h0_ref[...] + u
    ho_ref[...] = h
    x1 = x + _mm(jax.nn.gelu(gate) * h, wout_ref[...])
    xo_ref[...] = x1
    q_ref[...] = _mm(_rms(x1, nxa_ref[layer:layer + 1, :]), wq_ref[...])


def _s_attn_kernel(q_ref, k_ref, v_ref, o_ref):
    nb, _, dh = q_ref.shape
    for b in range(nb):
        q = q_ref[b] * (dh ** -0.5)
        s = jnp.sum(k_ref[b] * q[None], axis=-1, keepdims=True)
        e = jnp.exp(s - jnp.max(s, axis=0, keepdims=True))
        o_ref[b] = jnp.sum(e * v_ref[b], axis=0) / jnp.sum(e, axis=0)


def _s_ffn_kernel(x_ref, o_ref, wo_ref, nm_ref, wup_ref, cw_ref, cb_ref, wdn_ref, st_ref, nf_ref,
                  xo_ref, so_ref, *, layer, final):
    dff = wdn_ref.shape[0]
    kw = cw_ref.shape[0]
    hist = kw - 1
    x = x_ref[...] + _mm(o_ref[...], wo_ref[...])
    hn = _rms(x, nm_ref[layer:layer + 1, :]).astype(BF16)
    g = jnp.dot(hn, wup_ref[:, :dff], preferred_element_type=F32)
    u = jnp.dot(hn, wup_ref[:, dff:], preferred_element_type=F32)
    conv = cb_ref[layer:layer + 1, :] + cw_ref[hist:kw, :] * g
    for k in range(hist):
        conv = conv + cw_ref[k:k + 1, :] * st_ref[:, k, :]
    for k in range(hist - 1):
        so_ref[:, k, :] = st_ref[:, k + 1, :]
    so_ref[:, hist - 1, :] = g
    xn = x + _mm(jax.nn.gelu(conv) * u, wdn_ref[...])
    if final:
        xn = _rms(xn, nf_ref[...])
    xo_ref[...] = xn


def _s_cmod_pre_kernel(x_ref, nm_ref, pw1_ref, b1_ref, g_ref, *, layer, j):
    d = x_ref.shape[-1]
    hn = _rms(x_ref[...], nm_ref[layer:layer + 1, :])
    g_ref[...] = _glu(_mm(hn, pw1_ref[...]) + b1_ref[j:j + 1, :], d)


def _s_cmod_conv_kernel(st_ref, g_ref, dw_ref, dwb_ref, lng_ref, lnb_ref, c_ref, so_ref, *, j):
    rows, hist, _ = st_ref.shape
    kw = hist + 1

    def one_sequence(b, carry):
        g = g_ref[pl.ds(b, 1), :]
        c = (jnp.sum(dw_ref[0:hist, :] * st_ref[b], axis=0, keepdims=True)
             + dw_ref[hist:kw, :] * g + dwb_ref[j:j + 1, :])
        c_ref[pl.ds(b, 1), :] = jax.nn.silu(
            _layer_norm(c, lng_ref[j:j + 1, :], lnb_ref[j:j + 1, :]))
        so_ref[b, 0:hist - 1, :] = st_ref[b, 1:hist, :]
        so_ref[b, hist - 1:hist, :] = g
        return carry

    lax.fori_loop(0, rows, one_sequence, 0)


def _s_cmod_post_kernel(x_ref, c_ref, pw2_ref, nxa_ref, wq_ref, xo_ref, q_ref, *, layer):
    x1 = x_ref[...] + _mm(c_ref[...], pw2_ref[...])
    xo_ref[...] = x1
    q_ref[...] = _mm(_rms(x1, nxa_ref[layer:layer + 1, :]), wq_ref[...])


def _params(n_grid):
    return pltpu.CompilerParams(dimension_semantics=("arbitrary",) * n_grid,
                                vmem_limit_bytes=VMEM_LIMIT)


def _resident(shape, lead=None):
    if lead is None:
        nd = len(shape)
        return pl.BlockSpec(shape, lambda *_: (0,) * nd, pipeline_mode=pl.Buffered(1))
    nd = len(shape) - 1
    return pl.BlockSpec((None,) + tuple(shape[1:]), lambda *_: (lead,) + (0,) * nd,
                        pipeline_mode=pl.Buffered(1))


_whole = _resident


def _whole_out(shape):
    nd = len(shape)
    return pl.BlockSpec(shape, lambda *_: (0,) * nd)


def _sds(shape, dtype=F32):
    return jax.ShapeDtypeStruct(shape, dtype)


def _prompt_memkv(mem, mem_norm3, wkv, heads):
    depth = wkv.shape[0]
    b, m, d = mem.shape
    dh = d // heads
    return pl.pallas_call(
        _p_memkv_kernel,
        grid=(depth, b),
        in_specs=[pl.BlockSpec((None, m, d), lambda i, n: (n, 0, 0)),
                  pl.BlockSpec((None, 1, d), lambda i, n: (i, 0, 0)),
                  pl.BlockSpec((None, d, 2 * d), lambda i, n: (i, 0, 0))],
        out_specs=[pl.BlockSpec((None, None, m, heads, dh), lambda i, n: (i, n, 0, 0, 0)),
                   pl.BlockSpec((None, None, m, heads, dh), lambda i, n: (i, n, 0, 0, 0)),
                   pl.BlockSpec((None, None, d, m), lambda i, n: (i, n, 0, 0)),
                   pl.BlockSpec((None, None, m, d), lambda i, n: (i, n, 0, 0))],
        out_shape=[_sds((depth, b, m, heads, dh)), _sds((depth, b, m, heads, dh)),
                   _sds((depth, b, d, m), BF16), _sds((depth, b, m, d), BF16)],
        compiler_params=_params(2),
        name="prompt_memkv",
    )(mem, mem_norm3, wkv)


def _x_spec(tt, d):
    return pl.BlockSpec((None, tt, d), lambda n, t: (n, t, 0))


def _state_spec(rows, cols):
    return pl.BlockSpec((None, rows, cols), lambda n, t: (n, 0, 0))


def _xt_spec(b, tr, d):
    return pl.BlockSpec((b, tr, d), lambda t: (0, t, 0))


def _unslab_history(s, b):
    nl, hrows, lanes = s.shape
    return s.reshape(nl, hrows // b, b, lanes).transpose(2, 1, 0, 3).reshape(b, hrows // b, -1)


def _prompt_lru(x, layer, j, w):
    b, t, d = x.shape
    tr = MIXER_TIME_ROWS
    hist = w['lru_conv_w'].shape[1] - 1
    assert b == SUBLANES and tr % SUBLANES == 0 and tr >= hist and d % LANES == 0
    nl, hrows = d // LANES, hist * b
    ins = [(x, _xt_spec(b, tr, d)),
           (w['norm_mix'], _resident(w['norm_mix'].shape)),
           (w['lru_w_in'], _resident(w['lru_w_in'].shape, j)),
           (w['lru_conv_w'], _resident(w['lru_conv_w'].shape, j)),
           (w['lru_conv_b'], _resident(w['lru_conv_b'].shape)),
           (w['lru_wa'], _resident(w['lru_wa'].shape, j)),
           (w['lru_ba'], _resident(w['lru_ba'].shape)),
           (w['lru_wx'], _resident(w['lru_wx'].shape, j)),
           (w['lru_bx'], _resident(w['lru_bx'].shape)),
           (w['lru_lambda'], _resident(w['lru_lambda'].shape)),
           (w['lru_w_out'], _resident(w['lru_w_out'].shape, j))]
    return pl.pallas_call(
        functools.partial(_p_lru_kernel, layer=layer, j=j),
        grid=(t // tr,),
        in_specs=[s for _, s in ins],
        out_specs=[_xt_spec(b, tr, d), _whole_out((b, d)), _whole_out((nl, hrows, LANES))],
        out_shape=[_sds((b, t, d)), _sds((b, d)), _sds((nl, hrows, LANES))],
        scratch_shapes=[pltpu.VMEM((nl, hrows + b * tr, LANES), F32),
                        pltpu.VMEM((nl, b * tr, LANES), F32),
                        pltpu.VMEM((b, d), F32)],
        compiler_params=_params(1),
        name="prompt_lru",
    )(*[a for a, _ in ins])


def _prompt_cmod(x, layer, j, w):
    b, t, d = x.shape
    tr = MIXER_TIME_ROWS
    hist = w['cm_dw_w'].shape[1] - 1
    assert b == SUBLANES and tr % SUBLANES == 0 and tr >= hist and d % CMOD_COL_TILE == 0
    nl, hrows = d // LANES, hist * b
    ins = [(x, _xt_spec(b, tr, d)),
           (w['norm_mix'], _resident(w['norm_mix'].shape)),
           (w['cm_w_pw1'], _resident(w['cm_w_pw1'].shape, j)),
           (w['cm_b_pw1'], _resident(w['cm_b_pw1'].shape)),
           (w['cm_dw_w'], _resident(w['cm_dw_w'].shape, j)),
           (w['cm_dw_b'], _resident(w['cm_dw_b'].shape)),
           (w['cm_ln_g'], _resident(w['cm_ln_g'].shape)),
           (w['cm_ln_b'], _resident(w['cm_ln_b'].shape)),
           (w['cm_w_pw2'], _resident(w['cm_w_pw2'].shape, j))]
    return pl.pallas_call(
        functools.partial(_p_cmod_kernel, layer=layer, j=j),
        grid=(t // tr,),
        in_specs=[s for _, s in ins],
        out_specs=[_xt_spec(b, tr, d), _whole_out((nl, hrows, LANES))],
        out_shape=[_sds((b, t, d)), _sds((nl, hrows, LANES))],
        scratch_shapes=[pltpu.VMEM((nl, hrows + b * tr, LANES), F32),
                        pltpu.VMEM((nl, b * tr, LANES), F32)],
        compiler_params=_params(1),
        name="prompt_cmod",
    )(*[a for a, _ in ins])


def _prompt_xattn(x, layer, kt, vb, w):
    b, t, d = x.shape
    tt = XATTN_T_TILE
    m = vb.shape[2]
    ins = [(x, _x_spec(tt, d)),
           (w['norm_xa'], _resident(w['norm_xa'].shape)),
           (w['xa_w_q'], _resident(w['xa_w_q'].shape, layer)),
           (kt, pl.BlockSpec((None, None, d, m), lambda n, t_: (layer, n, 0, 0))),
           (vb, pl.BlockSpec((None, None, m, d), lambda n, t_: (layer, n, 0, 0))),
           (w['xa_w_o'], _resident(w['xa_w_o'].shape, layer))]
    return pl.pallas_call(
        functools.partial(_p_xattn_kernel, layer=layer),
        grid=(b, t // tt),
        in_specs=[s for _, s in ins],
        out_specs=_x_spec(tt, d),
        out_shape=_sds((b, t, d)),
        compiler_params=_params(2),
        name="prompt_xattn",
    )(*[a for a, _ in ins])


def _prompt_ffn(x, layer, final, w):
    b, t, d = x.shape
    tt = FFN_T_TILE
    dff = w['ffn_w_down'].shape[1]
    hist = w['ffn_conv_w'].shape[1] - 1
    ins = [(x, _x_spec(tt, d)),
           (w['norm_ffn'], _resident(w['norm_ffn'].shape)),
           (w['ffn_w_up'], _resident(w['ffn_w_up'].shape, layer)),
           (w['ffn_conv_w'], _resident(w['ffn_conv_w'].shape, layer)),
           (w['ffn_conv_b'], _resident(w['ffn_conv_b'].shape)),
           (w['ffn_w_down'], _resident(w['ffn_w_down'].shape, layer)),
           (w['norm_final'], _resident(w['norm_final'].shape))]
    return pl.pallas_call(
        functools.partial(_p_ffn_kernel, layer=layer, final=final),
        grid=(b, t // tt),
        in_specs=[s for _, s in ins],
        out_specs=[_x_spec(tt, d), _state_spec(hist, dff)],
        out_shape=[_sds((b, t, d)), _sds((b, hist, dff))],
        scratch_shapes=[pltpu.VMEM((SUBLANES + tt, dff), F32)],
        compiler_params=_params(2),
        name="prompt_ffn",
    )(*[a for a, _ in ins])


def _sample_lru(x, h0, cst, layer, j, w):
    r, d = x.shape
    ins = [(x, _whole(x.shape)), (h0, _whole(h0.shape, j)), (cst, _whole(cst.shape, j)),
           (w['norm_mix'], _whole(w['norm_mix'].shape)),
           (w['lru_w_in'], _whole(w['lru_w_in'].shape, j)),
           (w['lru_conv_w'], _whole(w['lru_conv_w'].shape, j)),
           (w['lru_conv_b'], _whole(w['lru_conv_b'].shape)),
           (w['lru_wa'], _whole(w['lru_wa'].shape, j)),
           (w['lru_ba'], _whole(w['lru_ba'].shape)),
           (w['lru_wx'], _whole(w['lru_wx'].shape, j)),
           (w['lru_bx'], _whole(w['lru_bx'].shape)),
           (w['lru_lambda'], _whole(w['lru_lambda'].shape)),
           (w['lru_w_out'], _whole(w['lru_w_out'].shape, j)),
           (w['norm_xa'], _whole(w['norm_xa'].shape)),
           (w['xa_w_q'], _whole(w['xa_w_q'].shape, layer))]
    return pl.pallas_call(
        functools.partial(_s_lru_kernel, layer=layer, j=j),
        in_specs=[s for _, s in ins],
        out_specs=[_whole_out((r, d)), _whole_out((r, d)), _whole_out((r, d)),
                   _whole_out(cst.shape[1:])],
        out_shape=[_sds((r, d)), _sds((r, d)), _sds((r, d)), _sds(cst.shape[1:])],
        grid=(1,),
        compiler_params=_params(1),
        name="sample_lru",
    )(*[a for a, _ in ins])


def _sample_attn(q, k, v, layer):
    r, d = q.shape
    _, _, m, nh, dh = k.shape
    nb = SAMPLE_ATTN_BLOCK
    kv_spec = pl.BlockSpec((None, nb, m, nh, dh), lambda n: (layer, n, 0, 0, 0))
    row_spec = pl.BlockSpec((nb, nh, dh), lambda n: (n, 0, 0))
    o = pl.pallas_call(
        _s_attn_kernel,
        grid=(r // nb,),
        in_specs=[row_spec, kv_spec, kv_spec],
        out_specs=row_spec,
        out_shape=_sds((r, nh, dh)),
        compiler_params=_params(1),
        name="sample_attn",
    )(q.reshape(r, nh, dh), k, v)
    return o.reshape(r, d)


def _sample_ffn(x, o, st, layer, final, w):
    r, d = x.shape
    ins = [(x, _whole(x.shape)), (o, _whole(o.shape)),
           (w['xa_w_o'], _whole(w['xa_w_o'].shape, layer)),
           (w['norm_ffn'], _whole(w['norm_ffn'].shape)),
           (w['ffn_w_up'], _whole(w['ffn_w_up'].shape, layer)),
           (w['ffn_conv_w'], _whole(w['ffn_conv_w'].shape, layer)),
           (w['ffn_conv_b'], _whole(w['ffn_conv_b'].shape)),
           (w['ffn_w_down'], _whole(w['ffn_w_down'].shape, layer)),
           (st, _whole(st.shape, layer)),
           (w['norm_final'], _whole(w['norm_final'].shape))]
    return pl.pallas_call(
        functools.partial(_s_ffn_kernel, layer=layer, final=final),
        in_specs=[s for _, s in ins],
        out_specs=[_whole_out((r, d)), _whole_out(st.shape[1:])],
        out_shape=[_sds((r, d)), _sds(st.shape[1:])],
        grid=(1,),
        compiler_params=_params(1),
        name="sample_ffn",
    )(*[a for a, _ in ins])


def _sample_cmod(x, st, layer, j, w):
    r, d = x.shape
    ins = [(x, _whole(x.shape)),
           (w['norm_mix'], _whole(w['norm_mix'].shape)),
           (w['cm_w_pw1'], _whole(w['cm_w_pw1'].shape, j)),
           (w['cm_b_pw1'], _whole(w['cm_b_pw1'].shape))]
    g = pl.pallas_call(
        functools.partial(_s_cmod_pre_kernel, layer=layer, j=j),
        in_specs=[s for _, s in ins],
        out_specs=_whole_out((r, d)),
        out_shape=_sds((r, d)),
        grid=(1,),
        compiler_params=_params(1),
        name="sample_cmod_pre",
    )(*[a for a, _ in ins])

    rows = SAMPLE_CONV_ROWS
    hist = st.shape[2]
    ins = [(st, pl.BlockSpec((None, rows, hist, d), lambda n: (j, n, 0, 0))),
           (g, pl.BlockSpec((rows, d), lambda n: (n, 0))),
           (w['cm_dw_w'], _resident(w['cm_dw_w'].shape, j)),
           (w['cm_dw_b'], _resident(w['cm_dw_b'].shape)),
           (w['cm_ln_g'], _resident(w['cm_ln_g'].shape)),
           (w['cm_ln_b'], _resident(w['cm_ln_b'].shape))]
    c, new_st = pl.pallas_call(
        functools.partial(_s_cmod_conv_kernel, j=j),
        grid=(r // rows,),
        in_specs=[s for _, s in ins],
        out_specs=[pl.BlockSpec((rows, d), lambda n: (n, 0)),
                   pl.BlockSpec((rows, hist, d), lambda n: (n, 0, 0))],
        out_shape=[_sds((r, d)), _sds((r, hist, d))],
        compiler_params=_params(1),
        name="sample_cmod_conv",
    )(*[a for a, _ in ins])

    ins = [(x, _whole(x.shape)), (c, _whole(c.shape)),
           (w['cm_w_pw2'], _whole(w['cm_w_pw2'].shape, j)),
           (w['norm_xa'], _whole(w['norm_xa'].shape)),
           (w['xa_w_q'], _whole(w['xa_w_q'].shape, layer))]
    x1, q = pl.pallas_call(
        functools.partial(_s_cmod_post_kernel, layer=layer),
        in_specs=[s for _, s in ins],
        out_specs=[_whole_out((r, d)), _whole_out((r, d))],
        out_shape=[_sds((r, d)), _sds((r, d))],
        grid=(1,),
        compiler_params=_params(1),
        name="sample_cmod_post",
    )(*[a for a, _ in ins])
    return x1, q, new_st


_MATMUL_WEIGHTS = ('lru_w_in', 'lru_wa', 'lru_wx', 'lru_w_out', 'cm_w_pw1', 'cm_w_pw2',
                   'xa_w_q', 'xa_w_kv', 'xa_w_o', 'ffn_w_up', 'ffn_w_down')


def kernel(x_prompt, x_sample, state_lru_h, state_lru_conv, state_cmod_conv, state_ffn_conv, cache_mem_k, cache_mem_v, mem_prompt, norm_mix, norm_xa, norm_ffn, norm_final, lru_w_in, lru_conv_w, lru_conv_b, lru_wa, lru_ba, lru_wx, lru_bx, lru_lambda, lru_w_out, cm_w_pw1, cm_b_pw1, cm_dw_w, cm_dw_b, cm_ln_g, cm_ln_b, cm_w_pw2, mem_norm, xa_w_q, xa_w_kv, xa_w_o, ffn_w_up, ffn_conv_w, ffn_conv_b, ffn_w_down):
    w = dict(norm_mix=norm_mix, norm_xa=norm_xa, norm_ffn=norm_ffn,
             norm_final=norm_final.reshape(1, -1),
             lru_w_in=lru_w_in, lru_conv_w=lru_conv_w, lru_conv_b=lru_conv_b, lru_wa=lru_wa,
             lru_ba=lru_ba, lru_wx=lru_wx, lru_bx=lru_bx, lru_lambda=lru_lambda,
             lru_w_out=lru_w_out, cm_w_pw1=cm_w_pw1, cm_b_pw1=cm_b_pw1, cm_dw_w=cm_dw_w,
             cm_dw_b=cm_dw_b, cm_ln_g=cm_ln_g, cm_ln_b=cm_ln_b, cm_w_pw2=cm_w_pw2,
             xa_w_q=xa_w_q, xa_w_kv=xa_w_kv, xa_w_o=xa_w_o,
             ffn_w_up=ffn_w_up, ffn_conv_w=ffn_conv_w, ffn_conv_b=ffn_conv_b,
             ffn_w_down=ffn_w_down)
    for name in _MATMUL_WEIGHTS:
        w[name] = w[name].astype(BF16)

    depth = norm_mix.shape[0]
    bsz, _, d = x_prompt.shape
    n_mem = mem_prompt.shape[1]
    heads = cache_mem_k.shape[3]

    k_p, v_p, kt_p, vb_p = _prompt_memkv(mem_prompt, mem_norm.reshape(depth, 1, d),
                                         w['xa_w_kv'], heads)
    x = x_prompt
    p_h, p_lconv, p_cconv, p_fconv = [], [], [], []
    for i in range(depth):
        j = i // 2
        if i % 2 == 0:
            x, h_last, cbuf = _prompt_lru(x, i, j, w)
            p_h.append(h_last)
            p_lconv.append(_unslab_history(cbuf, bsz))
        else:
            x, cbuf = _prompt_cmod(x, i, j, w)
            p_cconv.append(_unslab_history(cbuf, bsz))
        x = _prompt_xattn(x, i, kt_p, vb_p, w)
        x, fbuf = _prompt_ffn(x, i, i == depth - 1, w)
        p_fconv.append(fbuf)
    y_prompt = x

    r = x_sample.shape[0]
    xs = x_sample.reshape(r, d)
    lconv = state_lru_conv.reshape(state_lru_conv.shape[0], r, -1)
    s_h, s_lconv, s_cconv, s_fconv = [], [], [], []
    for i in range(depth):
        j = i // 2
        if i % 2 == 0:
            xs, q, h_new, c_new = _sample_lru(xs, state_lru_h, lconv, i, j, w)
            s_h.append(h_new)
            s_lconv.append(c_new.reshape(r, -1, d))
        else:
            xs, q, c_new = _sample_cmod(xs, state_cmod_conv, i, j, w)
            s_cconv.append(c_new)
        o = _sample_attn(q, cache_mem_k, cache_mem_v, i)
        xs, f_new = _sample_ffn(xs, o, state_ffn_conv, i, i == depth - 1, w)
        s_fconv.append(f_new)
    y_sample = xs.reshape(x_sample.shape)

    return (y_prompt, y_sample,
            jnp.stack(p_h), jnp.stack(p_lconv), jnp.stack(p_cconv), jnp.stack(p_fconv),
            k_p, v_p,
            jnp.stack(s_h), jnp.stack(s_lconv), jnp.stack(s_cconv), jnp.stack(s_fconv))
```

```python
import functools

import jax
import jax.numpy as jnp
from jax import lax
from jax.experimental import pallas as pl
from jax.experimental.pallas import tpu as pltpu

F32 = jnp.float32
BF16 = jnp.bfloat16

EPS = 1e-6
RG_C = 8.0
N_RG_BLOCKS = 4
XA_HEADS = 4
SUBLANES = 8
LANES = 128
VMEM_LIMIT = 56 << 20
FFN_VMEM_LIMIT = 60 << 20

LRU_TIME_ROWS = 64
CMOD_TIME_ROWS = 64
XATTN_T_TILE = 1024
FFN_T_TILE = 512
FFN_COL_TILE = 1536
CMOD_COL_TILE = 256
CMOD_ROW_TILE = 64
SAMPLE_CONV_ROWS = 32


def _rms(x, g):
    return x * lax.rsqrt(jnp.mean(x * x, axis=-1, keepdims=True) + EPS) * g


def _layer_norm(x, g, b):
    mu = jnp.mean(x, axis=-1, keepdims=True)
    xc = x - mu
    var = jnp.mean(xc * xc, axis=-1, keepdims=True)
    return xc * lax.rsqrt(var + EPS) * g + b


def _mm(a, w):
    return jnp.dot(a.astype(BF16), w, preferred_element_type=F32)


def _softplus(z):
    return jnp.maximum(z, 0.0) + jnp.log1p(jnp.exp(-jnp.abs(z)))


def _block_diag(xb, w_ref):
    blk = xb.shape[-1] // N_RG_BLOCKS
    return jnp.concatenate(
        [jnp.dot(xb[:, n * blk:(n + 1) * blk], w_ref[n], preferred_element_type=F32)
         for n in range(N_RG_BLOCKS)], axis=-1)


def _lru_decay_input(rec, ga, gx, lam):
    log_a = -RG_C * jax.nn.sigmoid(ga) * _softplus(-lam)
    a = jnp.exp(log_a)
    u = jnp.sqrt(1.0 - a * a) * jax.nn.sigmoid(gx) * rec
    return a, u


def _lru_gates(rec, wa_ref, ba, wx_ref, bx, lam):
    rb = rec.astype(BF16)
    return _lru_decay_input(rec, _block_diag(rb, wa_ref) + ba, _block_diag(rb, wx_ref) + bx, lam)


def _glu(ab, d):
    return ab[:, :d] * jax.nn.sigmoid(ab[:, d:])


def _p_memkv_kernel(m_ref, g_ref, wkv_ref, k_ref, v_ref, kt_ref, vb_ref):
    d = m_ref.shape[-1]
    hn = _rms(m_ref[...], g_ref[...]).astype(BF16)
    k = jnp.dot(hn, wkv_ref[:, :d], preferred_element_type=F32)
    v = jnp.dot(hn, wkv_ref[:, d:], preferred_element_type=F32)
    dh = d // k_ref.shape[1]
    for h in range(k_ref.shape[1]):
        k_ref[:, h, :] = k[:, h * dh:(h + 1) * dh]
        v_ref[:, h, :] = v[:, h * dh:(h + 1) * dh]
    kt_ref[...] = k.T.astype(BF16)
    vb_ref[...] = v.astype(BF16)


def _p_lru_kernel(x_ref, nm_ref, win_ref, cw_ref, cb_ref, wa_ref, ba_ref, wx_ref, bx_ref,
                  lam_ref, wout_ref,
                  xo_ref, ho_ref, co_ref,
                  rec_t, y_t, h_s, *, layer, j):
    nb, tr, d = x_ref.shape
    rows = nb * tr
    kw = cw_ref.shape[0]
    hist = kw - 1
    hrows = hist * nb
    nl = d // LANES
    tc = pl.program_id(0)

    @pl.when(tc == 0)
    def _():
        rec_t[:, 0:hrows, :] = jnp.zeros((nl, hrows, LANES), F32)
        h_s[...] = jnp.zeros_like(h_s)

    x = x_ref[...].reshape(rows, d)
    hn = _rms(x, nm_ref[layer:layer + 1, :]).astype(BF16)
    cblk = d // N_RG_BLOCKS
    lanes_per_blk = cblk // LANES

    def in_proj(n):
        lo = n * cblk
        return (jnp.dot(hn, win_ref[:, lo:lo + cblk], preferred_element_type=F32),
                jnp.dot(hn, win_ref[:, d + lo:d + lo + cblk], preferred_element_type=F32))

    out = jnp.zeros((rows, d), F32)
    nxt = in_proj(0)
    for n in range(N_RG_BLOCKS):
        gate, rec = nxt
        if n + 1 < N_RG_BLOCKS:
            nxt = in_proj(n + 1)
        cs = slice(n * cblk, (n + 1) * cblk)
        slabs = []
        for lt in range(lanes_per_blk):
            l = n * lanes_per_blk + lt
            ls = slice(l * LANES, (l + 1) * LANES)
            for b in range(nb):
                rec_t[l, pl.ds(hrows + b, tr, stride=nb), :] = (
                    rec[b * tr:(b + 1) * tr, lt * LANES:(lt + 1) * LANES])
            acc = cb_ref[j:j + 1, ls] + cw_ref[hist:kw, ls] * rec_t[l, hrows:hrows + rows, :]
            for k in range(hist):
                acc = acc + cw_ref[k:k + 1, ls] * rec_t[l, k * nb:k * nb + rows, :]
            slabs.append(acc)
        conv = jnp.concatenate(slabs, axis=-1)
        cb16 = conv.astype(BF16)
        ga = jnp.dot(cb16, wa_ref[n], preferred_element_type=F32) + ba_ref[j:j + 1, cs]
        gx = jnp.dot(cb16, wx_ref[n], preferred_element_type=F32) + bx_ref[j:j + 1, cs]
        a, u = _lru_decay_input(conv, ga, gx, lam_ref[j:j + 1, cs])

        h = h_s[:, cs]
        for t in range(tr):
            h = a[t * nb:(t + 1) * nb, :] * h + u[t * nb:(t + 1) * nb, :]
            for lt in range(lanes_per_blk):
                y_t[n * lanes_per_blk + lt, t * nb:(t + 1) * nb, :] = (
                    h[:, lt * LANES:(lt + 1) * LANES])
        h_s[:, cs] = h
        ho_ref[:, cs] = h

        y = jnp.concatenate(
            [jnp.concatenate([y_t[n * lanes_per_blk + lt, pl.ds(b, tr, stride=nb), :]
                              for lt in range(lanes_per_blk)], axis=-1)
             for b in range(nb)], axis=0)
        out = out + jnp.dot((jax.nn.gelu(gate) * y).astype(BF16), wout_ref[cs, :],
                            preferred_element_type=F32)
    tail = rec_t[:, rows:rows + hrows, :]
    co_ref[...] = jnp.concatenate([tail[l] for l in range(nl)], axis=-1)
    rec_t[:, 0:hrows, :] = tail
    xo_ref[...] = (x + out).reshape(nb, tr, d)


def _p_cmod_kernel(x_ref, nm_ref, pw1_ref, b1_ref, dw_ref, dwb_ref, lng_ref, lnb_ref, pw2_ref,
                   xo_ref, so_ref, g_t, c_t, *, layer, j):
    nb, tr, d = x_ref.shape
    rows = nb * tr
    kw = dw_ref.shape[0]
    hist = kw - 1
    hrows = hist * nb
    lanes_per_col = CMOD_COL_TILE // LANES
    tc = pl.program_id(0)

    @pl.when(tc == 0)
    def _():
        g_t[:, 0:hrows, :] = jnp.zeros((d // LANES, hrows, LANES), F32)

    x = x_ref[...].reshape(rows, d)
    hn = _rms(x, nm_ref[layer:layer + 1, :]).astype(BF16)
    def pw1(c):
        lo = c * CMOD_COL_TILE
        cs = slice(lo, lo + CMOD_COL_TILE)
        gs = slice(d + lo, d + lo + CMOD_COL_TILE)
        return (jnp.dot(hn, pw1_ref[:, cs], preferred_element_type=F32) + b1_ref[j:j + 1, cs],
                jnp.dot(hn, pw1_ref[:, gs], preferred_element_type=F32) + b1_ref[j:j + 1, gs])

    n_col = d // CMOD_COL_TILE
    nxt = pw1(0)
    for c in range(n_col):
        a, bg = nxt
        if c + 1 < n_col:
            nxt = pw1(c + 1)
        g = a * jax.nn.sigmoid(bg)
        for lt in range(lanes_per_col):
            l = c * lanes_per_col + lt
            ls = slice(l * LANES, (l + 1) * LANES)
            for b in range(nb):
                g_t[l, pl.ds(hrows + b, tr, stride=nb), :] = (
                    g[b * tr:(b + 1) * tr, lt * LANES:(lt + 1) * LANES])
            for rb in range(rows // CMOD_ROW_TILE):
                r0 = hrows + rb * CMOD_ROW_TILE
                acc = dwb_ref[j:j + 1, ls] + dw_ref[hist:kw, ls] * g_t[l, r0:r0 + CMOD_ROW_TILE, :]
                for s in range(1, kw):
                    acc = acc + (dw_ref[hist - s:kw - s, ls]
                                 * g_t[l, r0 - s * nb:r0 - s * nb + CMOD_ROW_TILE, :])
                c_t[l, rb * CMOD_ROW_TILE:(rb + 1) * CMOD_ROW_TILE, :] = acc
    tail = g_t[:, rows:rows + hrows, :]
    so_ref[...] = jnp.concatenate([tail[l] for l in range(d // LANES)], axis=-1)
    g_t[:, 0:hrows, :] = tail

    conv = jnp.concatenate(
        [jnp.concatenate([c_t[l, pl.ds(b, tr, stride=nb), :] for l in range(d // LANES)], axis=-1)
         for b in range(nb)], axis=0)
    c = jax.nn.silu(_layer_norm(conv, lng_ref[j:j + 1, :], lnb_ref[j:j + 1, :]))
    xo_ref[...] = (x + _mm(c, pw2_ref[...])).reshape(nb, tr, d)


def _p_xattn_kernel(x_ref, nm_ref, wq_ref, kt_ref, v_ref, wo_ref, xo_ref, *, layer):
    tt, d = x_ref.shape
    dh = d // XA_HEADS
    x = x_ref[...]
    q = _mm(_rms(x, nm_ref[layer:layer + 1, :]), wq_ref[...]).astype(BF16)
    cols = [slice(h * dh, (h + 1) * dh) for h in range(XA_HEADS)]
    scores = [jnp.dot(q[:, cs], kt_ref[cs, :], preferred_element_type=F32) * (dh ** -0.5)
              for cs in cols]
    heads = []
    for s, cs in zip(scores, cols):
        e = jnp.exp(s - jnp.max(s, axis=-1, keepdims=True))
        p = e / jnp.sum(e, axis=-1, keepdims=True)
        heads.append(jnp.dot(p.astype(BF16), v_ref[:, cs], preferred_element_type=F32))
    o = jnp.concatenate(heads, axis=-1)
    xo_ref[...] = x + _mm(o, wo_ref[...])


def _sample_attend(q_ref, k_ref, v_ref, o_ref, seqs):
    dh = q_ref.shape[-1]
    for b in seqs:
        q = q_ref[b] * (dh ** -0.5)
        s = jnp.sum(k_ref[b] * q[None], axis=-1, keepdims=True)
        e = jnp.exp(s - jnp.max(s, axis=0, keepdims=True))
        o_ref[b] = jnp.sum(e * v_ref[b], axis=0) / jnp.sum(e, axis=0)


def _p_ffn_kernel(x_ref, nm_ref, wup_ref, cw_ref, cb_ref, wdn_ref, nf_ref, q_ref, k_ref, v_ref,
                  xo_ref, so_ref, ao_ref, buf, *, layer, final):
    tt, d = x_ref.shape
    dff = wdn_ref.shape[0]
    kw = cw_ref.shape[0]
    hist = kw - 1
    tc = pl.program_id(1)

    @pl.when(tc == 0)
    def _():
        buf[0:SUBLANES, :] = jnp.zeros((SUBLANES, dff), F32)

    x = x_ref[...]
    hn = _rms(x, nm_ref[layer:layer + 1, :]).astype(BF16)
    def up(c):
        lo = c * FFN_COL_TILE
        return (jnp.dot(hn, wup_ref[:, lo:lo + FFN_COL_TILE], preferred_element_type=F32),
                jnp.dot(hn, wup_ref[:, dff + lo:dff + lo + FFN_COL_TILE],
                        preferred_element_type=F32))

    n_col = dff // FFN_COL_TILE
    n_seq = q_ref.shape[0]
    acc = jnp.zeros((tt, d), F32)
    nxt = up(0)
    for c in range(n_col):
        cs = slice(c * FFN_COL_TILE, (c + 1) * FFN_COL_TILE)
        g, u = nxt
        if c + 1 < n_col:
            nxt = up(c + 1)
        buf[SUBLANES:SUBLANES + tt, cs] = g
        conv = cb_ref[layer:layer + 1, cs] + cw_ref[hist:kw, cs] * g
        for k in range(hist):
            off = SUBLANES - hist + k
            conv = conv + cw_ref[k:k + 1, cs] * buf[off:off + tt, cs]
        act = (jax.nn.gelu(conv) * u).astype(BF16)
        acc = acc + jnp.dot(act, wdn_ref[cs, :], preferred_element_type=F32)
        _sample_attend(q_ref, k_ref, v_ref, ao_ref,
                       range(c * n_seq // n_col, (c + 1) * n_seq // n_col))
    tail = buf[SUBLANES + tt - hist:SUBLANES + tt, :]
    so_ref[...] = tail
    buf[SUBLANES - hist:SUBLANES, :] = tail

    xn = x + acc
    if final:
        xn = _rms(xn, nf_ref[...])
    xo_ref[...] = xn


def _s_lru_kernel(x_ref, h0_ref, cst_ref, nm_ref, win_ref, cw_ref, cb_ref, wa_ref, ba_ref,
                  wx_ref, bx_ref, lam_ref, wout_ref, nxa_ref, wq_ref,
                  xo_ref, q_ref, ho_ref, co_ref, *, layer, j):
    d = x_ref.shape[-1]
    kw = cw_ref.shape[0]
    hist = kw - 1
    x = x_ref[...]
    hn = _rms(x, nm_ref[layer:layer + 1, :]).astype(BF16)
    gate = jnp.dot(hn, win_ref[:, :d], preferred_element_type=F32)
    rec = jnp.dot(hn, win_ref[:, d:], preferred_element_type=F32)

    conv = cb_ref[j:j + 1, :] + cw_ref[hist:kw, :] * rec
    for k in range(hist):
        conv = conv + cw_ref[k:k + 1, :] * cst_ref[k]
    for k in range(hist - 1):
        co_ref[k] = cst_ref[k + 1]
    co_ref[hist - 1] = rec

    a, u = _lru_gates(conv, wa_ref, ba_ref[j:j + 1, :], wx_ref, bx_ref[j:j + 1, :],
                      lam_ref[j:j + 1, :])
    h = a * h0_ref[...] + u
    ho_ref[...] = h
    x1 = x + _mm(jax.nn.gelu(gate) * h, wout_ref[...])
    xo_ref[...] = x1
    q_ref[...] = _mm(_rms(x1, nxa_ref[layer:layer + 1, :]), wq_ref[...])


def _s_ffn_kernel(x_ref, o_ref, wo_ref, nm_ref, wup_ref, cw_ref, cb_ref, wdn_ref, st_ref, nf_ref,
                  xo_ref, so_ref, *, layer, final):
    dff = wdn_ref.shape[0]
    kw = cw_ref.shape[0]
    hist = kw - 1
    x = x_ref[...] + _mm(o_ref[...], wo_ref[...])
    hn = _rms(x, nm_ref[layer:layer + 1, :]).astype(BF16)
    g = jnp.dot(hn, wup_ref[:, :dff], preferred_element_type=F32)
    u = jnp.dot(hn, wup_ref[:, dff:], preferred_element_type=F32)
    conv = cb_ref[layer:layer + 1, :] + cw_ref[hist:kw, :] * g
    for k in range(hist):
        conv = conv + cw_ref[k:k + 1, :] * st_ref[:, k, :]
    for k in range(hist - 1):
        so_ref[:, k, :] = st_ref[:, k + 1, :]
    so_ref[:, hist - 1, :] = g
    xn = x + _mm(jax.nn.gelu(conv) * u, wdn_ref[...])
    if final:
        xn = _rms(xn, nf_ref[...])
    xo_ref[...] = xn


def _s_cmod_pre_kernel(x_ref, nm_ref, pw1_ref, b1_ref, g_ref, *, layer, j):
    d = x_ref.shape[-1]
    hn = _rms(x_ref[...], nm_ref[layer:layer + 1, :])
    g_ref[...] = _glu(_mm(hn, pw1_ref[...]) + b1_ref[j:j + 1, :], d)


def _s_cmod_conv_kernel(st_ref, g_ref, dw_ref, dwb_ref, lng_ref, lnb_ref, c_ref, so_ref, *, j):
    hist = st_ref.shape[0]
    kw = hist + 1
    g = g_ref[...]
    c = dwb_ref[j:j + 1, :] + dw_ref[hist:kw, :] * g
    for k in range(hist):
        c = c + dw_ref[k:k + 1, :] * st_ref[k]
    for k in range(hist - 1):
        so_ref[k] = st_ref[k + 1]
    so_ref[hist - 1] = g
    c_ref[...] = jax.nn.silu(_layer_norm(c, lng_ref[j:j + 1, :], lnb_ref[j:j + 1, :]))


def _s_cmod_post_kernel(x_ref, c_ref, pw2_ref, nxa_ref, wq_ref, xo_ref, q_ref, *, layer):
    x1 = x_ref[...] + _mm(c_ref[...], pw2_ref[...])
    xo_ref[...] = x1
    q_ref[...] = _mm(_rms(x1, nxa_ref[layer:layer + 1, :]), wq_ref[...])


def _params(n_grid):
    return pltpu.CompilerParams(dimension_semantics=("arbitrary",) * n_grid,
                                vmem_limit_bytes=VMEM_LIMIT)


def _resident(shape, lead=None):
    if lead is None:
        nd = len(shape)
        return pl.BlockSpec(shape, lambda *_: (0,) * nd, pipeline_mode=pl.Buffered(1))
    nd = len(shape) - 1
    return pl.BlockSpec((None,) + tuple(shape[1:]), lambda *_: (lead,) + (0,) * nd,
                        pipeline_mode=pl.Buffered(1))


_whole = _resident


def _whole_out(shape):
    nd = len(shape)
    return pl.BlockSpec(shape, lambda *_: (0,) * nd)


def _sds(shape, dtype=F32):
    return jax.ShapeDtypeStruct(shape, dtype)


def _prompt_memkv(mem, mem_norm3, wkv, heads):
    depth = wkv.shape[0]
    b, m, d = mem.shape
    dh = d // heads
    return pl.pallas_call(
        _p_memkv_kernel,
        grid=(depth, b),
        in_specs=[pl.BlockSpec((None, m, d), lambda i, n: (n, 0, 0)),
                  pl.BlockSpec((None, 1, d), lambda i, n: (i, 0, 0)),
                  pl.BlockSpec((None, d, 2 * d), lambda i, n: (i, 0, 0))],
        out_specs=[pl.BlockSpec((None, None, m, heads, dh), lambda i, n: (i, n, 0, 0, 0)),
                   pl.BlockSpec((None, None, m, heads, dh), lambda i, n: (i, n, 0, 0, 0)),
                   pl.BlockSpec((None, None, d, m), lambda i, n: (i, n, 0, 0)),
                   pl.BlockSpec((None, None, m, d), lambda i, n: (i, n, 0, 0))],
        out_shape=[_sds((depth, b, m, heads, dh)), _sds((depth, b, m, heads, dh)),
                   _sds((depth, b, d, m), BF16), _sds((depth, b, m, d), BF16)],
        compiler_params=_params(2),
        name="prompt_memkv",
    )(mem, mem_norm3, wkv)


def _x_spec(tt, d):
    return pl.BlockSpec((None, tt, d), lambda n, t: (n, t, 0))


def _state_spec(rows, cols):
    return pl.BlockSpec((None, rows, cols), lambda n, t: (n, 0, 0))


def _xt_spec(b, tr, d):
    return pl.BlockSpec((b, tr, d), lambda t: (0, t, 0))


def _steps_to_sequences(s, b):
    hrows, d = s.shape
    return s.reshape(hrows // b, b, d).transpose(1, 0, 2)


def _prompt_lru(x, layer, j, w):
    b, t, d = x.shape
    tr = LRU_TIME_ROWS
    hist = w['lru_conv_w'].shape[1] - 1
    assert b == SUBLANES and tr % SUBLANES == 0 and tr >= hist and d % LANES == 0
    nl, hrows = d // LANES, hist * b
    ins = [(x, _xt_spec(b, tr, d)),
           (w['norm_mix'], _resident(w['norm_mix'].shape)),
           (w['lru_w_in'], _resident(w['lru_w_in'].shape, j)),
           (w['lru_conv_w'], _resident(w['lru_conv_w'].shape, j)),
           (w['lru_conv_b'], _resident(w['lru_conv_b'].shape)),
           (w['lru_wa'], _resident(w['lru_wa'].shape, j)),
           (w['lru_ba'], _resident(w['lru_ba'].shape)),
           (w['lru_wx'], _resident(w['lru_wx'].shape, j)),
           (w['lru_bx'], _resident(w['lru_bx'].shape)),
           (w['lru_lambda'], _resident(w['lru_lambda'].shape)),
           (w['lru_w_out'], _resident(w['lru_w_out'].shape, j))]
    return pl.pallas_call(
        functools.partial(_p_lru_kernel, layer=layer, j=j),
        grid=(t // tr,),
        in_specs=[s for _, s in ins],
        out_specs=[_xt_spec(b, tr, d), _whole_out((b, d)), _whole_out((hrows, d))],
        out_shape=[_sds((b, t, d)), _sds((b, d)), _sds((hrows, d))],
        scratch_shapes=[pltpu.VMEM((nl, hrows + b * tr, LANES), F32),
                        pltpu.VMEM((nl, b * tr, LANES), F32),
                        pltpu.VMEM((b, d), F32)],
        compiler_params=_params(1),
        name="prompt_lru",
    )(*[a for a, _ in ins])


def _prompt_cmod(x, layer, j, w):
    b, t, d = x.shape
    tr = CMOD_TIME_ROWS
    hist = w['cm_dw_w'].shape[1] - 1
    assert b == SUBLANES and tr % SUBLANES == 0 and tr >= hist and d % CMOD_COL_TILE == 0
    nl, hrows = d // LANES, hist * b
    ins = [(x, _xt_spec(b, tr, d)),
           (w['norm_mix'], _resident(w['norm_mix'].shape)),
           (w['cm_w_pw1'], _resident(w['cm_w_pw1'].shape, j)),
           (w['cm_b_pw1'], _resident(w['cm_b_pw1'].shape)),
           (w['cm_dw_w'], _resident(w['cm_dw_w'].shape, j)),
           (w['cm_dw_b'], _resident(w['cm_dw_b'].shape)),
           (w['cm_ln_g'], _resident(w['cm_ln_g'].shape)),
           (w['cm_ln_b'], _resident(w['cm_ln_b'].shape)),
           (w['cm_w_pw2'], _resident(w['cm_w_pw2'].shape, j))]
    return pl.pallas_call(
        functools.partial(_p_cmod_kernel, layer=layer, j=j),
        grid=(t // tr,),
        in_specs=[s for _, s in ins],
        out_specs=[_xt_spec(b, tr, d), _whole_out((hrows, d))],
        out_shape=[_sds((b, t, d)), _sds((hrows, d))],
        scratch_shapes=[pltpu.VMEM((nl, hrows + b * tr, LANES), F32),
                        pltpu.VMEM((nl, b * tr, LANES), F32)],
        compiler_params=_params(1),
        name="prompt_cmod",
    )(*[a for a, _ in ins])


def _prompt_xattn(x, layer, kt, vb, w):
    b, t, d = x.shape
    tt = XATTN_T_TILE
    m = vb.shape[2]
    ins = [(x, _x_spec(tt, d)),
           (w['norm_xa'], _resident(w['norm_xa'].shape)),
           (w['xa_w_q'], _resident(w['xa_w_q'].shape, layer)),
           (kt, pl.BlockSpec((None, None, d, m), lambda n, t_: (layer, n, 0, 0))),
           (vb, pl.BlockSpec((None, None, m, d), lambda n, t_: (layer, n, 0, 0))),
           (w['xa_w_o'], _resident(w['xa_w_o'].shape, layer))]
    return pl.pallas_call(
        functools.partial(_p_xattn_kernel, layer=layer),
        grid=(b, t // tt),
        in_specs=[s for _, s in ins],
        out_specs=_x_spec(tt, d),
        out_shape=_sds((b, t, d)),
        compiler_params=_params(2),
        name="prompt_xattn",
    )(*[a for a, _ in ins])


def _prompt_ffn_sample_attn(x, layer, final, q, cache_k, cache_v, w):
    b, t, d = x.shape
    tt = FFN_T_TILE
    n_t = t // tt
    dff = w['ffn_w_down'].shape[1]
    hist = w['ffn_conv_w'].shape[1] - 1
    r = q.shape[0]
    _, _, m, nh, dh = cache_k.shape
    assert r % (b * n_t) == 0
    nb = r // (b * n_t)
    kv_spec = pl.BlockSpec((None, nb, m, nh, dh), lambda n, s: (layer, n * n_t + s, 0, 0, 0))
    row_spec = pl.BlockSpec((nb, nh, dh), lambda n, s: (n * n_t + s, 0, 0))
    ins = [(x, _x_spec(tt, d)),
           (w['norm_ffn'], _resident(w['norm_ffn'].shape)),
           (w['ffn_w_up'], _resident(w['ffn_w_up'].shape, layer)),
           (w['ffn_conv_w'], _resident(w['ffn_conv_w'].shape, layer)),
           (w['ffn_conv_b'], _resident(w['ffn_conv_b'].shape)),
           (w['ffn_w_down'], _resident(w['ffn_w_down'].shape, layer)),
           (w['norm_final'], _resident(w['norm_final'].shape)),
           (q.reshape(r, nh, dh), row_spec), (cache_k, kv_spec), (cache_v, kv_spec)]
    xo, fbuf, o = pl.pallas_call(
        functools.partial(_p_ffn_kernel, layer=layer, final=final),
        grid=(b, n_t),
        in_specs=[s for _, s in ins],
        out_specs=[_x_spec(tt, d), _state_spec(hist, dff), row_spec],
        out_shape=[_sds((b, t, d)), _sds((b, hist, dff)), _sds((r, nh, dh))],
        scratch_shapes=[pltpu.VMEM((SUBLANES + tt, dff), F32)],
        compiler_params=pltpu.CompilerParams(dimension_semantics=("arbitrary", "arbitrary"),
                                             vmem_limit_bytes=FFN_VMEM_LIMIT),
        name="prompt_ffn_sample_attn",
    )(*[a for a, _ in ins])
    return xo, fbuf, o.reshape(r, d)


def _sample_lru(x, h0, cst, layer, j, w):
    r, d = x.shape
    ins = [(x, _whole(x.shape)), (h0, _whole(h0.shape, j)), (cst, _whole(cst.shape, j)),
           (w['norm_mix'], _whole(w['norm_mix'].shape)),
           (w['lru_w_in'], _whole(w['lru_w_in'].shape, j)),
           (w['lru_conv_w'], _whole(w['lru_conv_w'].shape, j)),
           (w['lru_conv_b'], _whole(w['lru_conv_b'].shape)),
           (w['lru_wa'], _whole(w['lru_wa'].shape, j)),
           (w['lru_ba'], _whole(w['lru_ba'].shape)),
           (w['lru_wx'], _whole(w['lru_wx'].shape, j)),
           (w['lru_bx'], _whole(w['lru_bx'].shape)),
           (w['lru_lambda'], _whole(w['lru_lambda'].shape)),
           (w['lru_w_out'], _whole(w['lru_w_out'].shape, j)),
           (w['norm_xa'], _whole(w['norm_xa'].shape)),
           (w['xa_w_q'], _whole(w['xa_w_q'].shape, layer))]
    return pl.pallas_call(
        functools.partial(_s_lru_kernel, layer=layer, j=j),
        in_specs=[s for _, s in ins],
        out_specs=[_whole_out((r, d)), _whole_out((r, d)), _whole_out((r, d)),
                   _whole_out(cst.shape[1:])],
        out_shape=[_sds((r, d)), _sds((r, d)), _sds((r, d)), _sds(cst.shape[1:])],
        grid=(1,),
        compiler_params=_params(1),
        name="sample_lru",
    )(*[a for a, _ in ins])


def _sample_ffn(x, o, st, layer, final, w):
    r, d = x.shape
    ins = [(x, _whole(x.shape)), (o, _whole(o.shape)),
           (w['xa_w_o'], _whole(w['xa_w_o'].shape, layer)),
           (w['norm_ffn'], _whole(w['norm_ffn'].shape)),
           (w['ffn_w_up'], _whole(w['ffn_w_up'].shape, layer)),
           (w['ffn_conv_w'], _whole(w['ffn_conv_w'].shape, layer)),
           (w['ffn_conv_b'], _whole(w['ffn_conv_b'].shape)),
           (w['ffn_w_down'], _whole(w['ffn_w_down'].shape, layer)),
           (st, _whole(st.shape, layer)),
           (w['norm_final'], _whole(w['norm_final'].shape))]
    return pl.pallas_call(
        functools.partial(_s_ffn_kernel, layer=layer, final=final),
        in_specs=[s for _, s in ins],
        out_specs=[_whole_out((r, d)), _whole_out(st.shape[1:])],
        out_shape=[_sds((r, d)), _sds(st.shape[1:])],
        grid=(1,),
        compiler_params=_params(1),
        name="sample_ffn",
    )(*[a for a, _ in ins])


def _sample_cmod(x, st, layer, j, w):
    r, d = x.shape
    ins = [(x, _whole(x.shape)),
           (w['norm_mix'], _whole(w['norm_mix'].shape)),
           (w['cm_w_pw1'], _whole(w['cm_w_pw1'].shape, j)),
           (w['cm_b_pw1'], _whole(w['cm_b_pw1'].shape))]
    g = pl.pallas_call(
        functools.partial(_s_cmod_pre_kernel, layer=layer, j=j),
        in_specs=[s for _, s in ins],
        out_specs=_whole_out((r, d)),
        out_shape=_sds((r, d)),
        grid=(1,),
        compiler_params=_params(1),
        name="sample_cmod_pre",
    )(*[a for a, _ in ins])

    rows = SAMPLE_CONV_ROWS
    hist = st.shape[1]
    ins = [(st, pl.BlockSpec((None, hist, rows, d), lambda n: (j, 0, n, 0))),
           (g, pl.BlockSpec((rows, d), lambda n: (n, 0))),
           (w['cm_dw_w'], _resident(w['cm_dw_w'].shape, j)),
           (w['cm_dw_b'], _resident(w['cm_dw_b'].shape)),
           (w['cm_ln_g'], _resident(w['cm_ln_g'].shape)),
           (w['cm_ln_b'], _resident(w['cm_ln_b'].shape))]
    c, new_st = pl.pallas_call(
        functools.partial(_s_cmod_conv_kernel, j=j),
        grid=(r // rows,),
        in_specs=[s for _, s in ins],
        out_specs=[pl.BlockSpec((rows, d), lambda n: (n, 0)),
                   pl.BlockSpec((hist, rows, d), lambda n: (0, n, 0))],
        out_shape=[_sds((r, d)), _sds((hist, r, d))],
        compiler_params=_params(1),
        name="sample_cmod_conv",
    )(*[a for a, _ in ins])

    ins = [(x, _whole(x.shape)), (c, _whole(c.shape)),
           (w['cm_w_pw2'], _whole(w['cm_w_pw2'].shape, j)),
           (w['norm_xa'], _whole(w['norm_xa'].shape)),
           (w['xa_w_q'], _whole(w['xa_w_q'].shape, layer))]
    x1, q = pl.pallas_call(
        functools.partial(_s_cmod_post_kernel, layer=layer),
        in_specs=[s for _, s in ins],
        out_specs=[_whole_out((r, d)), _whole_out((r, d))],
        out_shape=[_sds((r, d)), _sds((r, d))],
        grid=(1,),
        compiler_params=_params(1),
        name="sample_cmod_post",
    )(*[a for a, _ in ins])
    return x1, q, new_st


_MATMUL_WEIGHTS = ('lru_w_in', 'lru_wa', 'lru_wx', 'lru_w_out', 'cm_w_pw1', 'cm_w_pw2',
                   'xa_w_q', 'xa_w_kv', 'xa_w_o', 'ffn_w_up', 'ffn_w_down')


def kernel(x_prompt, x_sample, state_lru_h, state_lru_conv, state_cmod_conv, state_ffn_conv, cache_mem_k, cache_mem_v, mem_prompt, norm_mix, norm_xa, norm_ffn, norm_final, lru_w_in, lru_conv_w, lru_conv_b, lru_wa, lru_ba, lru_wx, lru_bx, lru_lambda, lru_w_out, cm_w_pw1, cm_b_pw1, cm_dw_w, cm_dw_b, cm_ln_g, cm_ln_b, cm_w_pw2, mem_norm, xa_w_q, xa_w_kv, xa_w_o, ffn_w_up, ffn_conv_w, ffn_conv_b, ffn_w_down):
    w = dict(norm_mix=norm_mix, norm_xa=norm_xa, norm_ffn=norm_ffn,
             norm_final=norm_final.reshape(1, -1),
             lru_w_in=lru_w_in, lru_conv_w=lru_conv_w, lru_conv_b=lru_conv_b, lru_wa=lru_wa,
             lru_ba=lru_ba, lru_wx=lru_wx, lru_bx=lru_bx, lru_lambda=lru_lambda,
             lru_w_out=lru_w_out, cm_w_pw1=cm_w_pw1, cm_b_pw1=cm_b_pw1, cm_dw_w=cm_dw_w,
             cm_dw_b=cm_dw_b, cm_ln_g=cm_ln_g, cm_ln_b=cm_ln_b, cm_w_pw2=cm_w_pw2,
             xa_w_q=xa_w_q, xa_w_kv=xa_w_kv, xa_w_o=xa_w_o,
             ffn_w_up=ffn_w_up, ffn_conv_w=ffn_conv_w, ffn_conv_b=ffn_conv_b,
             ffn_w_down=ffn_w_down)
    for name in _MATMUL_WEIGHTS:
        w[name] = w[name].astype(BF16)

    depth = norm_mix.shape[0]
    bsz, _, d = x_prompt.shape
    n_mem = mem_prompt.shape[1]
    heads = cache_mem_k.shape[3]

    k_p, v_p, kt_p, vb_p = _prompt_memkv(mem_prompt, mem_norm.reshape(depth, 1, d),
                                         w['xa_w_kv'], heads)
    x = x_prompt
    r = x_sample.shape[0]
    xs = x_sample.reshape(r, d)
    lconv = state_lru_conv.transpose(0, 2, 1, 3)
    cconv = state_cmod_conv.transpose(0, 2, 1, 3)
    p_h, p_lconv, p_cconv, p_fconv = [], [], [], []
    s_h, s_lconv, s_cconv, s_fconv = [], [], [], []
    for i in range(depth):
        j = i // 2
        if i % 2 == 0:
            x, h_last, cbuf = _prompt_lru(x, i, j, w)
            p_h.append(h_last)
            p_lconv.append(_steps_to_sequences(cbuf, bsz))
            xs, q, h_new, c_new = _sample_lru(xs, state_lru_h, lconv, i, j, w)
            s_h.append(h_new)
            s_lconv.append(c_new.transpose(1, 0, 2))
        else:
            x, cbuf = _prompt_cmod(x, i, j, w)
            p_cconv.append(_steps_to_sequences(cbuf, bsz))
            xs, q, c_new = _sample_cmod(xs, cconv, i, j, w)
            s_cconv.append(c_new.transpose(1, 0, 2))
        x = _prompt_xattn(x, i, kt_p, vb_p, w)
        x, fbuf, o = _prompt_ffn_sample_attn(x, i, i == depth - 1, q, cache_mem_k, cache_mem_v, w)
        p_fconv.append(fbuf)
        xs, f_new = _sample_ffn(xs, o, state_ffn_conv, i, i == depth - 1, w)
        s_fconv.append(f_new)
    y_prompt = x
    y_sample = xs.reshape(x_sample.shape)

    return (y_prompt, y_sample,
            jnp.stack(p_h), jnp.stack(p_lconv), jnp.stack(p_cconv), jnp.stack(p_fconv),
            k_p, v_p,
            jnp.stack(s_h), jnp.stack(s_lconv), jnp.stack(s_cconv), jnp.stack(s_fconv))
```

```python
import functools

import jax
import jax.numpy as jnp
from jax import lax
from jax.experimental import pallas as pl
from jax.experimental.pallas import tpu as pltpu

F32 = jnp.float32
BF16 = jnp.bfloat16

EPS = 1e-6
RG_C = 8.0
N_RG_BLOCKS = 4
XA_HEADS = 4
SUBLANES = 8
LANES = 128
BF16_SUBLANES = 16
VMEM_LIMIT = 56 << 20
FFN_VMEM_LIMIT = 60 << 20

LRU_TIME_ROWS = 64
CMOD_TIME_ROWS = 64
XATTN_T_TILE = 1024
FFN_T_TILE = 512
FFN_COL_TILE = 1536
CMOD_COL_TILE = 256
CMOD_ROW_TILE = 64
SAMPLE_CONV_ROWS = 32


def _rms(x, g):
    return x * lax.rsqrt(jnp.mean(x * x, axis=-1, keepdims=True) + EPS) * g


def _layer_norm(x, g, b):
    mu = jnp.mean(x, axis=-1, keepdims=True)
    xc = x - mu
    var = jnp.mean(xc * xc, axis=-1, keepdims=True)
    return xc * lax.rsqrt(var + EPS) * g + b


def _mm(a, w):
    return jnp.dot(a.astype(BF16), w, preferred_element_type=F32)


def _softplus(z):
    return jnp.maximum(z, 0.0) + jnp.log1p(jnp.exp(-jnp.abs(z)))


def _block_diag(xb, w_ref):
    blk = xb.shape[-1] // N_RG_BLOCKS
    return jnp.concatenate(
        [jnp.dot(xb[:, n * blk:(n + 1) * blk], w_ref[n], preferred_element_type=F32)
         for n in range(N_RG_BLOCKS)], axis=-1)


def _lru_decay_input(rec, ga, gx, lam):
    log_a = -RG_C * jax.nn.sigmoid(ga) * _softplus(-lam)
    a = jnp.exp(log_a)
    u = jnp.sqrt(1.0 - a * a) * jax.nn.sigmoid(gx) * rec
    return a, u


def _lru_gates(rec, wa_ref, ba, wx_ref, bx, lam):
    rb = rec.astype(BF16)
    return _lru_decay_input(rec, _block_diag(rb, wa_ref) + ba, _block_diag(rb, wx_ref) + bx, lam)


def _glu(ab, d):
    return ab[:, :d] * jax.nn.sigmoid(ab[:, d:])


def _cast_blocks(srcs, dsts):
    for src, dst in zip(srcs, dsts):
        dst[...] = src[...].astype(BF16)


def _p_memkv_kernel(m_ref, g_ref, wkv_ref, *rest, n_cast):
    cast_in = rest[:n_cast]
    k_ref, v_ref, kt_ref, vb_ref = rest[n_cast:n_cast + 4]
    cast_out = rest[n_cast + 4:2 * n_cast + 4]
    wkv_s, = rest[2 * n_cast + 4:]
    _cast_blocks(cast_in, cast_out)

    @pl.when(pl.program_id(1) == 0)
    def _():
        wkv_s[...] = wkv_ref[...].astype(BF16)

    d = m_ref.shape[-1]
    hn = _rms(m_ref[...], g_ref[...]).astype(BF16)
    k = jnp.dot(hn, wkv_s[:, :d], preferred_element_type=F32)
    v = jnp.dot(hn, wkv_s[:, d:], preferred_element_type=F32)
    dh = d // k_ref.shape[1]
    for h in range(k_ref.shape[1]):
        k_ref[:, h, :] = k[:, h * dh:(h + 1) * dh]
        v_ref[:, h, :] = v[:, h * dh:(h + 1) * dh]
    kt_ref[...] = k.T.astype(BF16)
    vb_ref[...] = v.astype(BF16)


def _p_lru_kernel(x_ref, nm_ref, win_ref, cw_ref, cb_ref, wa_ref, ba_ref, wx_ref, bx_ref,
                  lam_ref, wout_ref, *rest, layer, j, n_cast):
    cast_in = rest[:n_cast]
    xo_ref, ho_ref, co_ref = rest[n_cast:n_cast + 3]
    cast_out = rest[n_cast + 3:2 * n_cast + 3]
    rec_t, y_t, h_s = rest[2 * n_cast + 3:]
    _cast_blocks(cast_in, cast_out)

    nb, tr, d = x_ref.shape
    rows = nb * tr
    kw = cw_ref.shape[0]
    hist = kw - 1
    hrows = hist * nb
    nl = d // LANES
    tc = pl.program_id(0)

    @pl.when(tc == 0)
    def _():
        rec_t[:, 0:hrows, :] = jnp.zeros((nl, hrows, LANES), F32)
        h_s[...] = jnp.zeros_like(h_s)

    x = x_ref[...].reshape(rows, d)
    hn = _rms(x, nm_ref[layer:layer + 1, :]).astype(BF16)
    cblk = d // N_RG_BLOCKS
    lanes_per_blk = cblk // LANES

    def in_proj(n):
        lo = n * cblk
        return (jnp.dot(hn, win_ref[:, lo:lo + cblk], preferred_element_type=F32),
                jnp.dot(hn, win_ref[:, d + lo:d + lo + cblk], preferred_element_type=F32))

    out = jnp.zeros((rows, d), F32)
    nxt = in_proj(0)
    for n in range(N_RG_BLOCKS):
        gate, rec = nxt
        if n + 1 < N_RG_BLOCKS:
            nxt = in_proj(n + 1)
        cs = slice(n * cblk, (n + 1) * cblk)
        slabs = []
        for lt in range(lanes_per_blk):
            l = n * lanes_per_blk + lt
            ls = slice(l * LANES, (l + 1) * LANES)
            for b in range(nb):
                rec_t[l, pl.ds(hrows + b, tr, stride=nb), :] = (
                    rec[b * tr:(b + 1) * tr, lt * LANES:(lt + 1) * LANES])
            acc = cb_ref[j:j + 1, ls] + cw_ref[hist:kw, ls] * rec_t[l, hrows:hrows + rows, :]
            for k in range(hist):
                acc = acc + cw_ref[k:k + 1, ls] * rec_t[l, k * nb:k * nb + rows, :]
            slabs.append(acc)
        conv = jnp.concatenate(slabs, axis=-1)
        cb16 = conv.astype(BF16)
        ga = jnp.dot(cb16, wa_ref[n], preferred_element_type=F32) + ba_ref[j:j + 1, cs]
        gx = jnp.dot(cb16, wx_ref[n], preferred_element_type=F32) + bx_ref[j:j + 1, cs]
        a, u = _lru_decay_input(conv, ga, gx, lam_ref[j:j + 1, cs])

        h = h_s[:, cs]
        for t in range(tr):
            h = a[t * nb:(t + 1) * nb, :] * h + u[t * nb:(t + 1) * nb, :]
            for lt in range(lanes_per_blk):
                y_t[n * lanes_per_blk + lt, t * nb:(t + 1) * nb, :] = (
                    h[:, lt * LANES:(lt + 1) * LANES])
        h_s[:, cs] = h
        ho_ref[:, cs] = h

        y = jnp.concatenate(
            [jnp.concatenate([y_t[n * lanes_per_blk + lt, pl.ds(b, tr, stride=nb), :]
                              for lt in range(lanes_per_blk)], axis=-1)
             for b in range(nb)], axis=0)
        out = out + jnp.dot((jax.nn.gelu(gate) * y).astype(BF16), wout_ref[cs, :],
                            preferred_element_type=F32)
    tail = rec_t[:, rows:rows + hrows, :]
    co_ref[...] = jnp.concatenate([tail[l] for l in range(nl)], axis=-1)
    rec_t[:, 0:hrows, :] = tail
    xo_ref[...] = (x + out).reshape(nb, tr, d)


def _p_cmod_kernel(x_ref, nm_ref, pw1_ref, b1_ref, dw_ref, dwb_ref, lng_ref, lnb_ref, pw2_ref,
                   xo_ref, so_ref, g_t, c_t, *, layer, j):
    nb, tr, d = x_ref.shape
    rows = nb * tr
    kw = dw_ref.shape[0]
    hist = kw - 1
    hrows = hist * nb
    lanes_per_col = CMOD_COL_TILE // LANES
    tc = pl.program_id(0)

    @pl.when(tc == 0)
    def _():
        g_t[:, 0:hrows, :] = jnp.zeros((d // LANES, hrows, LANES), F32)

    x = x_ref[...].reshape(rows, d)
    hn = _rms(x, nm_ref[layer:layer + 1, :]).astype(BF16)
    def pw1(c):
        lo = c * CMOD_COL_TILE
        cs = slice(lo, lo + CMOD_COL_TILE)
        gs = slice(d + lo, d + lo + CMOD_COL_TILE)
        return (jnp.dot(hn, pw1_ref[:, cs], preferred_element_type=F32) + b1_ref[j:j + 1, cs],
                jnp.dot(hn, pw1_ref[:, gs], preferred_element_type=F32) + b1_ref[j:j + 1, gs])

    n_col = d // CMOD_COL_TILE
    nxt = pw1(0)
    for c in range(n_col):
        a, bg = nxt
        if c + 1 < n_col:
            nxt = pw1(c + 1)
        g = a * jax.nn.sigmoid(bg)
        for lt in range(lanes_per_col):
            l = c * lanes_per_col + lt
            ls = slice(l * LANES, (l + 1) * LANES)
            for b in range(nb):
                g_t[l, pl.ds(hrows + b, tr, stride=nb), :] = (
                    g[b * tr:(b + 1) * tr, lt * LANES:(lt + 1) * LANES])
            for rb in range(rows // CMOD_ROW_TILE):
                r0 = hrows + rb * CMOD_ROW_TILE
                acc = dwb_ref[j:j + 1, ls] + dw_ref[hist:kw, ls] * g_t[l, r0:r0 + CMOD_ROW_TILE, :]
                for s in range(1, kw):
                    acc = acc + (dw_ref[hist - s:kw - s, ls]
                                 * g_t[l, r0 - s * nb:r0 - s * nb + CMOD_ROW_TILE, :])
                c_t[l, rb * CMOD_ROW_TILE:(rb + 1) * CMOD_ROW_TILE, :] = acc
    tail = g_t[:, rows:rows + hrows, :]
    so_ref[...] = jnp.concatenate([tail[l] for l in range(d // LANES)], axis=-1)
    g_t[:, 0:hrows, :] = tail

    conv = jnp.concatenate(
        [jnp.concatenate([c_t[l, pl.ds(b, tr, stride=nb), :] for l in range(d // LANES)], axis=-1)
         for b in range(nb)], axis=0)
    c = jax.nn.silu(_layer_norm(conv, lng_ref[j:j + 1, :], lnb_ref[j:j + 1, :]))
    xo_ref[...] = (x + _mm(c, pw2_ref[...])).reshape(nb, tr, d)


def _p_xattn_kernel(x_ref, nm_ref, wq_ref, kt_ref, v_ref, wo_ref, xo_ref, *, layer):
    tt, d = x_ref.shape
    dh = d // XA_HEADS
    x = x_ref[...]
    q = _mm(_rms(x, nm_ref[layer:layer + 1, :]), wq_ref[...]).astype(BF16)
    cols = [slice(h * dh, (h + 1) * dh) for h in range(XA_HEADS)]
    scores = [jnp.dot(q[:, cs], kt_ref[cs, :], preferred_element_type=F32) * (dh ** -0.5)
              for cs in cols]
    heads = []
    for s, cs in zip(scores, cols):
        e = jnp.exp(s - jnp.max(s, axis=-1, keepdims=True))
        p = e / jnp.sum(e, axis=-1, keepdims=True)
        heads.append(jnp.dot(p.astype(BF16), v_ref[:, cs], preferred_element_type=F32))
    o = jnp.concatenate(heads, axis=-1)
    xo_ref[...] = x + _mm(o, wo_ref[...])


def _sample_attend(q_ref, k_ref, v_ref, o_ref, seqs):
    dh = q_ref.shape[-1]
    for b in seqs:
        q = q_ref[b] * (dh ** -0.5)
        s = jnp.sum(k_ref[b] * q[None], axis=-1, keepdims=True)
        e = jnp.exp(s - jnp.max(s, axis=0, keepdims=True))
        o_ref[b] = jnp.sum(e * v_ref[b], axis=0) / jnp.sum(e, axis=0)


def _p_ffn_kernel(x_ref, nm_ref, wup_ref, cw_ref, cb_ref, wdn_ref, nf_ref, q_ref, k_ref, v_ref,
                  xo_ref, so_ref, ao_ref, buf, *, layer, final):
    tt, d = x_ref.shape
    dff = wdn_ref.shape[0]
    kw = cw_ref.shape[0]
    hist = kw - 1
    tc = pl.program_id(1)

    @pl.when(tc == 0)
    def _():
        buf[0:SUBLANES, :] = jnp.zeros((SUBLANES, dff), F32)

    x = x_ref[...]
    hn = _rms(x, nm_ref[layer:layer + 1, :]).astype(BF16)
    def up(c):
        lo = c * FFN_COL_TILE
        return (jnp.dot(hn, wup_ref[:, lo:lo + FFN_COL_TILE], preferred_element_type=F32),
                jnp.dot(hn, wup_ref[:, dff + lo:dff + lo + FFN_COL_TILE],
                        preferred_element_type=F32))

    n_col = dff // FFN_COL_TILE
    n_seq = q_ref.shape[0]
    acc = jnp.zeros((tt, d), F32)
    nxt = up(0)
    for c in range(n_col):
        cs = slice(c * FFN_COL_TILE, (c + 1) * FFN_COL_TILE)
        g, u = nxt
        if c + 1 < n_col:
            nxt = up(c + 1)
        buf[SUBLANES:SUBLANES + tt, cs] = g
        conv = cb_ref[layer:layer + 1, cs] + cw_ref[hist:kw, cs] * g
        for k in range(hist):
            off = SUBLANES - hist + k
            conv = conv + cw_ref[k:k + 1, cs] * buf[off:off + tt, cs]
        act = (jax.nn.gelu(conv) * u).astype(BF16)
        acc = acc + jnp.dot(act, wdn_ref[cs, :], preferred_element_type=F32)
        _sample_attend(q_ref, k_ref, v_ref, ao_ref,
                       range(c * n_seq // n_col, (c + 1) * n_seq // n_col))
    tail = buf[SUBLANES + tt - hist:SUBLANES + tt, :]
    so_ref[...] = tail
    buf[SUBLANES - hist:SUBLANES, :] = tail

    xn = x + acc
    if final:
        xn = _rms(xn, nf_ref[...])
    xo_ref[...] = xn


def _s_lru_kernel(x_ref, h0_ref, cst_ref, nm_ref, win_ref, cw_ref, cb_ref, wa_ref, ba_ref,
                  wx_ref, bx_ref, lam_ref, wout_ref, nxa_ref, wq_ref,
                  xo_ref, q_ref, ho_ref, co_ref, *, layer, j):
    d = x_ref.shape[-1]
    kw = cw_ref.shape[0]
    hist = kw - 1
    x = x_ref[...]
    hn = _rms(x, nm_ref[layer:layer + 1, :]).astype(BF16)
    gate = jnp.dot(hn, win_ref[:, :d], preferred_element_type=F32)
    rec = jnp.dot(hn, win_ref[:, d:], preferred_element_type=F32)

    conv = cb_ref[j:j + 1, :] + cw_ref[hist:kw, :] * rec
    for k in range(hist):
        conv = conv + cw_ref[k:k + 1, :] * cst_ref[k]
    for k in range(hist - 1):
        co_ref[k] = cst_ref[k + 1]
    co_ref[hist - 1] = rec

    a, u = _lru_gates(conv, wa_ref, ba_ref[j:j + 1, :], wx_ref, bx_ref[j:j + 1, :],
                      lam_ref[j:j + 1, :])
    h = a * h0_ref[...] + u
    ho_ref[...] = h
    x1 = x + _mm(jax.nn.gelu(gate) * h, wout_ref[...])
    xo_ref[...] = x1
    q_ref[...] = _mm(_rms(x1, nxa_ref[layer:layer + 1, :]), wq_ref[...])


def _s_ffn_kernel(x_ref, o_ref, wo_ref, nm_ref, wup_ref, cw_ref, cb_ref, wdn_ref, st_ref, nf_ref,
                  xo_ref, so_ref, *, layer, final):
    dff = wdn_ref.shape[0]
    kw = cw_ref.shape[0]
    hist = kw - 1
    x = x_ref[...] + _mm(o_ref[...], wo_ref[...])
    hn = _rms(x, nm_ref[layer:layer + 1, :]).astype(BF16)
    g = jnp.dot(hn, wup_ref[:, :dff], preferred_element_type=F32)
    u = jnp.dot(hn, wup_ref[:, dff:], preferred_element_type=F32)
    conv = cb_ref[layer:layer + 1, :] + cw_ref[hist:kw, :] * g
    for k in range(hist):
        conv = conv + cw_ref[k:k + 1, :] * st_ref[:, k, :]
    for k in range(hist - 1):
        so_ref[:, k, :] = st_ref[:, k + 1, :]
    so_ref[:, hist - 1, :] = g
    xn = x + _mm(jax.nn.gelu(conv) * u, wdn_ref[...])
    if final:
        xn = _rms(xn, nf_ref[...])
    xo_ref[...] = xn


def _s_cmod_pre_kernel(x_ref, nm_ref, pw1_ref, b1_ref, g_ref, *, layer, j):
    d = x_ref.shape[-1]
    hn = _rms(x_ref[...], nm_ref[layer:layer + 1, :])
    g_ref[...] = _glu(_mm(hn, pw1_ref[...]) + b1_ref[j:j + 1, :], d)


def _s_cmod_conv_kernel(st_ref, g_ref, dw_ref, dwb_ref, lng_ref, lnb_ref, c_ref, so_ref, *, j):
    hist = st_ref.shape[0]
    kw = hist + 1
    g = g_ref[...]
    c = dwb_ref[j:j + 1, :] + dw_ref[hist:kw, :] * g
    for k in range(hist):
        c = c + dw_ref[k:k + 1, :] * st_ref[k]
    for k in range(hist - 1):
        so_ref[k] = st_ref[k + 1]
    so_ref[hist - 1] = g
    c_ref[...] = jax.nn.silu(_layer_norm(c, lng_ref[j:j + 1, :], lnb_ref[j:j + 1, :]))


def _s_cmod_post_kernel(x_ref, c_ref, pw2_ref, nxa_ref, wq_ref, xo_ref, q_ref, *, layer):
    x1 = x_ref[...] + _mm(c_ref[...], pw2_ref[...])
    xo_ref[...] = x1
    q_ref[...] = _mm(_rms(x1, nxa_ref[layer:layer + 1, :]), wq_ref[...])


def _params(n_grid):
    return pltpu.CompilerParams(dimension_semantics=("arbitrary",) * n_grid,
                                vmem_limit_bytes=VMEM_LIMIT)


def _resident(shape, lead=None):
    if lead is None:
        nd = len(shape)
        return pl.BlockSpec(shape, lambda *_: (0,) * nd, pipeline_mode=pl.Buffered(1))
    nd = len(shape) - 1
    return pl.BlockSpec((None,) + tuple(shape[1:]), lambda *_: (lead,) + (0,) * nd,
                        pipeline_mode=pl.Buffered(1))


_whole = _resident


def _whole_out(shape):
    nd = len(shape)
    return pl.BlockSpec(shape, lambda *_: (0,) * nd)


def _sds(shape, dtype=F32):
    return jax.ShapeDtypeStruct(shape, dtype)


def _flat_cast_blocks(to_cast, steps, step_index):
    flat = [a.reshape(-1, a.shape[-1]) for a in to_cast]
    for a in flat:
        assert a.shape[0] % (steps * BF16_SUBLANES) == 0
    specs = [pl.BlockSpec((a.shape[0] // steps, a.shape[1]), lambda *g: (step_index(*g), 0))
             for a in flat]
    return flat, specs


def _prompt_memkv(mem, mem_norm3, wkv, heads, to_cast):
    depth = wkv.shape[0]
    b, m, d = mem.shape
    dh = d // heads
    flat, cast_specs = _flat_cast_blocks(to_cast, depth * b, lambda i, n: i * b + n)
    outs = pl.pallas_call(
        functools.partial(_p_memkv_kernel, n_cast=len(flat)),
        grid=(depth, b),
        in_specs=[pl.BlockSpec((None, m, d), lambda i, n: (n, 0, 0)),
                  pl.BlockSpec((None, 1, d), lambda i, n: (i, 0, 0)),
                  pl.BlockSpec((None, d, 2 * d), lambda i, n: (i, 0, 0))] + cast_specs,
        out_specs=[pl.BlockSpec((None, None, m, heads, dh), lambda i, n: (i, n, 0, 0, 0)),
                   pl.BlockSpec((None, None, m, heads, dh), lambda i, n: (i, n, 0, 0, 0)),
                   pl.BlockSpec((None, None, d, m), lambda i, n: (i, n, 0, 0)),
                   pl.BlockSpec((None, None, m, d), lambda i, n: (i, n, 0, 0))] + cast_specs,
        out_shape=[_sds((depth, b, m, heads, dh)), _sds((depth, b, m, heads, dh)),
                   _sds((depth, b, d, m), BF16), _sds((depth, b, m, d), BF16)]
                  + [_sds(a.shape, BF16) for a in flat],
        scratch_shapes=[pltpu.VMEM((d, 2 * d), BF16)],
        compiler_params=_params(2),
        name="prompt_memkv",
    )(mem, mem_norm3, wkv, *flat)
    return outs[:4], [o.reshape(a.shape) for o, a in zip(outs[4:], to_cast)]


def _x_spec(tt, d):
    return pl.BlockSpec((None, tt, d), lambda n, t: (n, t, 0))


def _state_spec(rows, cols):
    return pl.BlockSpec((None, rows, cols), lambda n, t: (n, 0, 0))


def _xt_spec(b, tr, d):
    return pl.BlockSpec((b, tr, d), lambda t: (0, t, 0))


def _steps_to_sequences(s, b):
    hrows, d = s.shape
    return s.reshape(hrows // b, b, d).transpose(1, 0, 2)


def _prompt_lru(x, layer, j, w, to_cast):
    b, t, d = x.shape
    tr = LRU_TIME_ROWS
    hist = w['lru_conv_w'].shape[1] - 1
    assert b == SUBLANES and tr % SUBLANES == 0 and tr >= hist and d % LANES == 0
    nl, hrows = d // LANES, hist * b
    steps = t // tr
    flat, cast_specs = _flat_cast_blocks(to_cast, steps, lambda s: s)
    ins = [(x, _xt_spec(b, tr, d)),
           (w['norm_mix'], _resident(w['norm_mix'].shape)),
           (w['lru_w_in'], _resident(w['lru_w_in'].shape, j)),
           (w['lru_conv_w'], _resident(w['lru_conv_w'].shape, j)),
           (w['lru_conv_b'], _resident(w['lru_conv_b'].shape)),
           (w['lru_wa'], _resident(w['lru_wa'].shape, j)),
           (w['lru_ba'], _resident(w['lru_ba'].shape)),
           (w['lru_wx'], _resident(w['lru_wx'].shape, j)),
           (w['lru_bx'], _resident(w['lru_bx'].shape)),
           (w['lru_lambda'], _resident(w['lru_lambda'].shape)),
           (w['lru_w_out'], _resident(w['lru_w_out'].shape, j))]
    ins += list(zip(flat, cast_specs))
    outs = pl.pallas_call(
        functools.partial(_p_lru_kernel, layer=layer, j=j, n_cast=len(flat)),
        grid=(steps,),
        in_specs=[s for _, s in ins],
        out_specs=[_xt_spec(b, tr, d), _whole_out((b, d)), _whole_out((hrows, d))] + cast_specs,
        out_shape=([_sds((b, t, d)), _sds((b, d)), _sds((hrows, d))]
                   + [_sds(a.shape, BF16) for a in flat]),
        scratch_shapes=[pltpu.VMEM((nl, hrows + b * tr, LANES), F32),
                        pltpu.VMEM((nl, b * tr, LANES), F32),
                        pltpu.VMEM((b, d), F32)],
        compiler_params=_params(1),
        name="prompt_lru",
    )(*[a for a, _ in ins])
    casted = [o.reshape(a.shape) for o, a in zip(outs[3:], to_cast)]
    return outs[0], outs[1], outs[2], casted


def _prompt_cmod(x, layer, j, w):
    b, t, d = x.shape
    tr = CMOD_TIME_ROWS
    hist = w['cm_dw_w'].shape[1] - 1
    assert b == SUBLANES and tr % SUBLANES == 0 and tr >= hist and d % CMOD_COL_TILE == 0
    nl, hrows = d // LANES, hist * b
    ins = [(x, _xt_spec(b, tr, d)),
           (w['norm_mix'], _resident(w['norm_mix'].shape)),
           (w['cm_w_pw1'], _resident(w['cm_w_pw1'].shape, j)),
           (w['cm_b_pw1'], _resident(w['cm_b_pw1'].shape)),
           (w['cm_dw_w'], _resident(w['cm_dw_w'].shape, j)),
           (w['cm_dw_b'], _resident(w['cm_dw_b'].shape)),
           (w['cm_ln_g'], _resident(w['cm_ln_g'].shape)),
           (w['cm_ln_b'], _resident(w['cm_ln_b'].shape)),
           (w['cm_w_pw2'], _resident(w['cm_w_pw2'].shape, j))]
    return pl.pallas_call(
        functools.partial(_p_cmod_kernel, layer=layer, j=j),
        grid=(t // tr,),
        in_specs=[s for _, s in ins],
        out_specs=[_xt_spec(b, tr, d), _whole_out((hrows, d))],
        out_shape=[_sds((b, t, d)), _sds((hrows, d))],
        scratch_shapes=[pltpu.VMEM((nl, hrows + b * tr, LANES), F32),
                        pltpu.VMEM((nl, b * tr, LANES), F32)],
        compiler_params=_params(1),
        name="prompt_cmod",
    )(*[a for a, _ in ins])


def _prompt_xattn(x, layer, kt, vb, w):
    b, t, d = x.shape
    tt = XATTN_T_TILE
    m = vb.shape[2]
    ins = [(x, _x_spec(tt, d)),
           (w['norm_xa'], _resident(w['norm_xa'].shape)),
           (w['xa_w_q'], _resident(w['xa_w_q'].shape, layer)),
           (kt, pl.BlockSpec((None, None, d, m), lambda n, t_: (layer, n, 0, 0))),
           (vb, pl.BlockSpec((None, None, m, d), lambda n, t_: (layer, n, 0, 0))),
           (w['xa_w_o'], _resident(w['xa_w_o'].shape, layer))]
    return pl.pallas_call(
        functools.partial(_p_xattn_kernel, layer=layer),
        grid=(b, t // tt),
        in_specs=[s for _, s in ins],
        out_specs=_x_spec(tt, d),
        out_shape=_sds((b, t, d)),
        compiler_params=_params(2),
        name="prompt_xattn",
    )(*[a for a, _ in ins])


def _prompt_ffn_sample_attn(x, layer, final, q, cache_k, cache_v, w):
    b, t, d = x.shape
    tt = FFN_T_TILE
    n_t = t // tt
    dff = w['ffn_w_down'].shape[1]
    hist = w['ffn_conv_w'].shape[1] - 1
    r = q.shape[0]
    _, _, m, nh, dh = cache_k.shape
    assert r % (b * n_t) == 0
    nb = r // (b * n_t)
    kv_spec = pl.BlockSpec((None, nb, m, nh, dh), lambda n, s: (layer, n * n_t + s, 0, 0, 0))
    row_spec = pl.BlockSpec((nb, nh, dh), lambda n, s: (n * n_t + s, 0, 0))
    ins = [(x, _x_spec(tt, d)),
           (w['norm_ffn'], _resident(w['norm_ffn'].shape)),
           (w['ffn_w_up'], _resident(w['ffn_w_up'].shape, layer)),
           (w['ffn_conv_w'], _resident(w['ffn_conv_w'].shape, layer)),
           (w['ffn_conv_b'], _resident(w['ffn_conv_b'].shape)),
           (w['ffn_w_down'], _resident(w['ffn_w_down'].shape, layer)),
           (w['norm_final'], _resident(w['norm_final'].shape)),
           (q.reshape(r, nh, dh), row_spec), (cache_k, kv_spec), (cache_v, kv_spec)]
    xo, fbuf, o = pl.pallas_call(
        functools.partial(_p_ffn_kernel, layer=layer, final=final),
        grid=(b, n_t),
        in_specs=[s for _, s in ins],
        out_specs=[_x_spec(tt, d), _state_spec(hist, dff), row_spec],
        out_shape=[_sds((b, t, d)), _sds((b, hist, dff)), _sds((r, nh, dh))],
        scratch_shapes=[pltpu.VMEM((SUBLANES + tt, dff), F32)],
        compiler_params=pltpu.CompilerParams(dimension_semantics=("arbitrary", "arbitrary"),
                                             vmem_limit_bytes=FFN_VMEM_LIMIT),
        name="prompt_ffn_sample_attn",
    )(*[a for a, _ in ins])
    return xo, fbuf, o.reshape(r, d)


def _sample_lru(x, h0, cst, layer, j, w):
    r, d = x.shape
    ins = [(x, _whole(x.shape)), (h0, _whole(h0.shape, j)), (cst, _whole(cst.shape, j)),
           (w['norm_mix'], _whole(w['norm_mix'].shape)),
           (w['lru_w_in'], _whole(w['lru_w_in'].shape, j)),
           (w['lru_conv_w'], _whole(w['lru_conv_w'].shape, j)),
           (w['lru_conv_b'], _whole(w['lru_conv_b'].shape)),
           (w['lru_wa'], _whole(w['lru_wa'].shape, j)),
           (w['lru_ba'], _whole(w['lru_ba'].shape)),
           (w['lru_wx'], _whole(w['lru_wx'].shape, j)),
           (w['lru_bx'], _whole(w['lru_bx'].shape)),
           (w['lru_lambda'], _whole(w['lru_lambda'].shape)),
           (w['lru_w_out'], _whole(w['lru_w_out'].shape, j)),
           (w['norm_xa'], _whole(w['norm_xa'].shape)),
           (w['xa_w_q'], _whole(w['xa_w_q'].shape, layer))]
    return pl.pallas_call(
        functools.partial(_s_lru_kernel, layer=layer, j=j),
        in_specs=[s for _, s in ins],
        out_specs=[_whole_out((r, d)), _whole_out((r, d)), _whole_out((r, d)),
                   _whole_out(cst.shape[1:])],
        out_shape=[_sds((r, d)), _sds((r, d)), _sds((r, d)), _sds(cst.shape[1:])],
        grid=(1,),
        compiler_params=_params(1),
        name="sample_lru",
    )(*[a for a, _ in ins])


def _sample_ffn(x, o, st, layer, final, w):
    r, d = x.shape
    ins = [(x, _whole(x.shape)), (o, _whole(o.shape)),
           (w['xa_w_o'], _whole(w['xa_w_o'].shape, layer)),
           (w['norm_ffn'], _whole(w['norm_ffn'].shape)),
           (w['ffn_w_up'], _whole(w['ffn_w_up'].shape, layer)),
           (w['ffn_conv_w'], _whole(w['ffn_conv_w'].shape, layer)),
           (w['ffn_conv_b'], _whole(w['ffn_conv_b'].shape)),
           (w['ffn_w_down'], _whole(w['ffn_w_down'].shape, layer)),
           (st, _whole(st.shape, layer)),
           (w['norm_final'], _whole(w['norm_final'].shape))]
    return pl.pallas_call(
        functools.partial(_s_ffn_kernel, layer=layer, final=final),
        in_specs=[s for _, s in ins],
        out_specs=[_whole_out((r, d)), _whole_out(st.shape[1:])],
        out_shape=[_sds((r, d)), _sds(st.shape[1:])],
        grid=(1,),
        compiler_params=_params(1),
        name="sample_ffn",
    )(*[a for a, _ in ins])


def _sample_cmod(x, st, layer, j, w):
    r, d = x.shape
    ins = [(x, _whole(x.shape)),
           (w['norm_mix'], _whole(w['norm_mix'].shape)),
           (w['cm_w_pw1'], _whole(w['cm_w_pw1'].shape, j)),
           (w['cm_b_pw1'], _whole(w['cm_b_pw1'].shape))]
    g = pl.pallas_call(
        functools.partial(_s_cmod_pre_kernel, layer=layer, j=j),
        in_specs=[s for _, s in ins],
        out_specs=_whole_out((r, d)),
        out_shape=_sds((r, d)),
        grid=(1,),
        compiler_params=_params(1),
        name="sample_cmod_pre",
    )(*[a for a, _ in ins])

    rows = SAMPLE_CONV_ROWS
    hist = st.shape[1]
    ins = [(st, pl.BlockSpec((None, hist, rows, d), lambda n: (j, 0, n, 0))),
           (g, pl.BlockSpec((rows, d), lambda n: (n, 0))),
           (w['cm_dw_w'], _resident(w['cm_dw_w'].shape, j)),
           (w['cm_dw_b'], _resident(w['cm_dw_b'].shape)),
           (w['cm_ln_g'], _resident(w['cm_ln_g'].shape)),
           (w['cm_ln_b'], _resident(w['cm_ln_b'].shape))]
    c, new_st = pl.pallas_call(
        functools.partial(_s_cmod_conv_kernel, j=j),
        grid=(r // rows,),
        in_specs=[s for _, s in ins],
        out_specs=[pl.BlockSpec((rows, d), lambda n: (n, 0)),
                   pl.BlockSpec((hist, rows, d), lambda n: (0, n, 0))],
        out_shape=[_sds((r, d)), _sds((hist, r, d))],
        compiler_params=_params(1),
        name="sample_cmod_conv",
    )(*[a for a, _ in ins])

    ins = [(x, _whole(x.shape)), (c, _whole(c.shape)),
           (w['cm_w_pw2'], _whole(w['cm_w_pw2'].shape, j)),
           (w['norm_xa'], _whole(w['norm_xa'].shape)),
           (w['xa_w_q'], _whole(w['xa_w_q'].shape, layer))]
    x1, q = pl.pallas_call(
        functools.partial(_s_cmod_post_kernel, layer=layer),
        in_specs=[s for _, s in ins],
        out_specs=[_whole_out((r, d)), _whole_out((r, d))],
        out_shape=[_sds((r, d)), _sds((r, d))],
        grid=(1,),
        compiler_params=_params(1),
        name="sample_cmod_post",
    )(*[a for a, _ in ins])
    return x1, q, new_st


_MIXER0_MATMUL_WEIGHTS = ('lru_w_in', 'lru_wa', 'lru_wx', 'lru_w_out')
_LATE_MATMUL_WEIGHTS = ('cm_w_pw1', 'cm_w_pw2', 'xa_w_q', 'xa_w_o', 'ffn_w_up', 'ffn_w_down')


def kernel(x_prompt, x_sample, state_lru_h, state_lru_conv, state_cmod_conv, state_ffn_conv, cache_mem_k, cache_mem_v, mem_prompt, norm_mix, norm_xa, norm_ffn, norm_final, lru_w_in, lru_conv_w, lru_conv_b, lru_wa, lru_ba, lru_wx, lru_bx, lru_lambda, lru_w_out, cm_w_pw1, cm_b_pw1, cm_dw_w, cm_dw_b, cm_ln_g, cm_ln_b, cm_w_pw2, mem_norm, xa_w_q, xa_w_kv, xa_w_o, ffn_w_up, ffn_conv_w, ffn_conv_b, ffn_w_down):
    w = dict(norm_mix=norm_mix, norm_xa=norm_xa, norm_ffn=norm_ffn,
             norm_final=norm_final.reshape(1, -1),
             lru_w_in=lru_w_in, lru_conv_w=lru_conv_w, lru_conv_b=lru_conv_b, lru_wa=lru_wa,
             lru_ba=lru_ba, lru_wx=lru_wx, lru_bx=lru_bx, lru_lambda=lru_lambda,
             lru_w_out=lru_w_out, cm_w_pw1=cm_w_pw1, cm_b_pw1=cm_b_pw1, cm_dw_w=cm_dw_w,
             cm_dw_b=cm_dw_b, cm_ln_g=cm_ln_g, cm_ln_b=cm_ln_b, cm_w_pw2=cm_w_pw2,
             xa_w_q=xa_w_q, xa_w_kv=xa_w_kv, xa_w_o=xa_w_o,
             ffn_w_up=ffn_w_up, ffn_conv_w=ffn_conv_w, ffn_conv_b=ffn_conv_b,
             ffn_w_down=ffn_w_down)

    depth = norm_mix.shape[0]
    bsz, _, d = x_prompt.shape
    n_mem = mem_prompt.shape[1]
    heads = cache_mem_k.shape[3]

    (k_p, v_p, kt_p, vb_p), casted = _prompt_memkv(
        mem_prompt, mem_norm.reshape(depth, 1, d), w['xa_w_kv'], heads,
        [w[n] for n in _MIXER0_MATMUL_WEIGHTS])
    w.update(zip(_MIXER0_MATMUL_WEIGHTS, casted))
    x = x_prompt
    r = x_sample.shape[0]
    xs = x_sample.reshape(r, d)
    lconv = state_lru_conv.transpose(0, 2, 1, 3)
    cconv = state_cmod_conv.transpose(0, 2, 1, 3)
    p_h, p_lconv, p_cconv, p_fconv = [], [], [], []
    s_h, s_lconv, s_cconv, s_fconv = [], [], [], []
    for i in range(depth):
        j = i // 2
        if i % 2 == 0:
            late = [n for n in _LATE_MATMUL_WEIGHTS if w[n].dtype != BF16]
            x, h_last, cbuf, casted = _prompt_lru(x, i, j, w, [w[n] for n in late])
            w.update(zip(late, casted))
            p_h.append(h_last)
            p_lconv.append(_steps_to_sequences(cbuf, bsz))
            xs, q, h_new, c_new = _sample_lru(xs, state_lru_h, lconv, i, j, w)
            s_h.append(h_new)
            s_lconv.append(c_new.transpose(1, 0, 2))
        else:
            x, cbuf = _prompt_cmod(x, i, j, w)
            p_cconv.append(_steps_to_sequences(cbuf, bsz))
            xs, q, c_new = _sample_cmod(xs, cconv, i, j, w)
            s_cconv.append(c_new.transpose(1, 0, 2))
        x = _prompt_xattn(x, i, kt_p, vb_p, w)
        x, fbuf, o = _prompt_ffn_sample_attn(x, i, i == depth - 1, q, cache_mem_k, cache_mem_v, w)
        p_fconv.append(fbuf)
        xs, f_new = _sample_ffn(xs, o, state_ffn_conv, i, i == depth - 1, w)
        s_fconv.append(f_new)
    y_prompt = x
    y_sample = xs.reshape(x_sample.shape)

    return (y_prompt, y_sample,
            jnp.stack(p_h), jnp.stack(p_lconv), jnp.stack(p_cconv), jnp.stack(p_fconv),
            k_p, v_p,
            jnp.stack(s_h), jnp.stack(s_lconv), jnp.stack(s_cconv), jnp.stack(s_fconv))
```

```python
import functools

import jax
import jax.numpy as jnp
from jax import lax
from jax.experimental import pallas as pl
from jax.experimental.pallas import tpu as pltpu

F32 = jnp.float32
BF16 = jnp.bfloat16

EPS = 1e-6
RG_C = 8.0
N_RG_BLOCKS = 4
XA_HEADS = 4
SUBLANES = 8
LANES = 128
BF16_SUBLANES = 16
VMEM_LIMIT = 56 << 20
FFN_VMEM_LIMIT = 60 << 20

LRU_TIME_ROWS = 64
CMOD_TIME_ROWS = 64
XATTN_T_TILE = 1024
FFN_T_TILE = 512
FFN_COL_TILE = 1536
CMOD_COL_TILE = 256
CMOD_ROW_TILE = 64
SAMPLE_CONV_ROWS = 32


def _rms(x, g):
    return x * lax.rsqrt(jnp.mean(x * x, axis=-1, keepdims=True) + EPS) * g


def _layer_norm(x, g, b):
    mu = jnp.mean(x, axis=-1, keepdims=True)
    xc = x - mu
    var = jnp.mean(xc * xc, axis=-1, keepdims=True)
    return xc * lax.rsqrt(var + EPS) * g + b


def _mm(a, w):
    return jnp.dot(a.astype(BF16), w, preferred_element_type=F32)


_GELU_C = 0.7978845608028654


def _gelu(x):
    inner = x * (_GELU_C + (_GELU_C * 0.044715) * (x * x))
    return x * (0.5 + 0.5 * jnp.tanh(inner))


def _softplus(z):
    return jnp.maximum(z, 0.0) + jnp.log1p(jnp.exp(-jnp.abs(z)))


def _block_diag(xb, w_ref):
    blk = xb.shape[-1] // N_RG_BLOCKS
    return jnp.concatenate(
        [jnp.dot(xb[:, n * blk:(n + 1) * blk], w_ref[n], preferred_element_type=F32)
         for n in range(N_RG_BLOCKS)], axis=-1)


def _lru_decay_input(rec, ga, gx, lam):
    log_a = -RG_C * jax.nn.sigmoid(ga) * _softplus(-lam)
    a = jnp.exp(log_a)
    u = jnp.sqrt(1.0 - a * a) * jax.nn.sigmoid(gx) * rec
    return a, u


def _lru_gates(rec, wa_ref, ba, wx_ref, bx, lam):
    rb = rec.astype(BF16)
    return _lru_decay_input(rec, _block_diag(rb, wa_ref) + ba, _block_diag(rb, wx_ref) + bx, lam)


def _glu(ab, d):
    return ab[:, :d] * jax.nn.sigmoid(ab[:, d:])


def _cast_blocks(srcs, dsts):
    for src, dst in zip(srcs, dsts):
        dst[...] = src[...].astype(BF16)


def _p_memkv_kernel(m_ref, g_ref, wkv_ref, *rest, n_cast):
    cast_in = rest[:n_cast]
    k_ref, v_ref, kt_ref, vb_ref = rest[n_cast:n_cast + 4]
    cast_out = rest[n_cast + 4:2 * n_cast + 4]
    wkv_s, = rest[2 * n_cast + 4:]
    _cast_blocks(cast_in, cast_out)

    @pl.when(pl.program_id(1) == 0)
    def _():
        wkv_s[...] = wkv_ref[...].astype(BF16)

    d = m_ref.shape[-1]
    hn = _rms(m_ref[...], g_ref[...]).astype(BF16)
    k = jnp.dot(hn, wkv_s[:, :d], preferred_element_type=F32)
    v = jnp.dot(hn, wkv_s[:, d:], preferred_element_type=F32)
    dh = d // k_ref.shape[1]
    for h in range(k_ref.shape[1]):
        k_ref[:, h, :] = k[:, h * dh:(h + 1) * dh]
        v_ref[:, h, :] = v[:, h * dh:(h + 1) * dh]
    kt_ref[...] = k.T.astype(BF16)
    vb_ref[...] = v.astype(BF16)


def _p_lru_kernel(x_ref, nm_ref, win_ref, cw_ref, cb_ref, wa_ref, ba_ref, wx_ref, bx_ref,
                  lam_ref, wout_ref, *rest, layer, j, n_cast):
    cast_in = rest[:n_cast]
    xo_ref, ho_ref, co_ref = rest[n_cast:n_cast + 3]
    cast_out = rest[n_cast + 3:2 * n_cast + 3]
    rec_t, y_t, h_s = rest[2 * n_cast + 3:]
    _cast_blocks(cast_in, cast_out)

    nb, tr, d = x_ref.shape
    rows = nb * tr
    kw = cw_ref.shape[0]
    hist = kw - 1
    hrows = hist * nb
    nl = d // LANES
    tc = pl.program_id(0)

    @pl.when(tc == 0)
    def _():
        rec_t[:, 0:hrows, :] = jnp.zeros((nl, hrows, LANES), F32)
        h_s[...] = jnp.zeros_like(h_s)

    x = x_ref[...].reshape(rows, d)
    hn = _rms(x, nm_ref[layer:layer + 1, :]).astype(BF16)
    cblk = d // N_RG_BLOCKS
    lanes_per_blk = cblk // LANES

    def in_proj(n):
        lo = n * cblk
        return (jnp.dot(hn, win_ref[:, lo:lo + cblk], preferred_element_type=F32),
                jnp.dot(hn, win_ref[:, d + lo:d + lo + cblk], preferred_element_type=F32))

    out = None
    nxt = in_proj(0)
    for n in range(N_RG_BLOCKS):
        gate, rec = nxt
        if n + 1 < N_RG_BLOCKS:
            nxt = in_proj(n + 1)
        cs = slice(n * cblk, (n + 1) * cblk)
        slabs = []
        for lt in range(lanes_per_blk):
            l = n * lanes_per_blk + lt
            ls = slice(l * LANES, (l + 1) * LANES)
            for b in range(nb):
                rec_t[l, pl.ds(hrows + b, tr, stride=nb), :] = (
                    rec[b * tr:(b + 1) * tr, lt * LANES:(lt + 1) * LANES])
            acc = cb_ref[j:j + 1, ls] + cw_ref[hist:kw, ls] * rec_t[l, hrows:hrows + rows, :]
            for k in range(hist):
                acc = acc + cw_ref[k:k + 1, ls] * rec_t[l, k * nb:k * nb + rows, :]
            slabs.append(acc)
        conv = jnp.concatenate(slabs, axis=-1)
        cb16 = conv.astype(BF16)
        ga = jnp.dot(cb16, wa_ref[n], preferred_element_type=F32) + ba_ref[j:j + 1, cs]
        gx = jnp.dot(cb16, wx_ref[n], preferred_element_type=F32) + bx_ref[j:j + 1, cs]
        a, u = _lru_decay_input(conv, ga, gx, lam_ref[j:j + 1, cs])

        h = h_s[:, cs]
        for t in range(tr):
            h = a[t * nb:(t + 1) * nb, :] * h + u[t * nb:(t + 1) * nb, :]
            for lt in range(lanes_per_blk):
                y_t[n * lanes_per_blk + lt, t * nb:(t + 1) * nb, :] = (
                    h[:, lt * LANES:(lt + 1) * LANES])
        h_s[:, cs] = h
        ho_ref[:, cs] = h

        y = jnp.concatenate(
            [jnp.concatenate([y_t[n * lanes_per_blk + lt, pl.ds(b, tr, stride=nb), :]
                              for lt in range(lanes_per_blk)], axis=-1)
             for b in range(nb)], axis=0)
        part = jnp.dot((_gelu(gate) * y).astype(BF16), wout_ref[cs, :],
                       preferred_element_type=F32)
        out = part if out is None else out + part
    tail = rec_t[:, rows:rows + hrows, :]
    co_ref[...] = jnp.concatenate([tail[l] for l in range(nl)], axis=-1)
    rec_t[:, 0:hrows, :] = tail
    xo_ref[...] = (x + out).reshape(nb, tr, d)


def _p_cmod_kernel(x_ref, nm_ref, pw1_ref, b1_ref, dw_ref, dwb_ref, lng_ref, lnb_ref, pw2_ref,
                   xo_ref, so_ref, g_t, c_t, *, layer, j):
    nb, tr, d = x_ref.shape
    rows = nb * tr
    kw = dw_ref.shape[0]
    hist = kw - 1
    hrows = hist * nb
    lanes_per_col = CMOD_COL_TILE // LANES
    tc = pl.program_id(0)

    @pl.when(tc == 0)
    def _():
        g_t[:, 0:hrows, :] = jnp.zeros((d // LANES, hrows, LANES), F32)

    x = x_ref[...].reshape(rows, d)
    hn = _rms(x, nm_ref[layer:layer + 1, :]).astype(BF16)
    def pw1(c):
        lo = c * CMOD_COL_TILE
        cs = slice(lo, lo + CMOD_COL_TILE)
        gs = slice(d + lo, d + lo + CMOD_COL_TILE)
        return (jnp.dot(hn, pw1_ref[:, cs], preferred_element_type=F32) + b1_ref[j:j + 1, cs],
                jnp.dot(hn, pw1_ref[:, gs], preferred_element_type=F32) + b1_ref[j:j + 1, gs])

    n_col = d // CMOD_COL_TILE
    nxt = pw1(0)
    for c in range(n_col):
        a, bg = nxt
        if c + 1 < n_col:
            nxt = pw1(c + 1)
        g = a * jax.nn.sigmoid(bg)
        for lt in range(lanes_per_col):
            l = c * lanes_per_col + lt
            ls = slice(l * LANES, (l + 1) * LANES)
            for b in range(nb):
                g_t[l, pl.ds(hrows + b, tr, stride=nb), :] = (
                    g[b * tr:(b + 1) * tr, lt * LANES:(lt + 1) * LANES])
            for rb in range(rows // CMOD_ROW_TILE):
                r0 = hrows + rb * CMOD_ROW_TILE
                acc = dwb_ref[j:j + 1, ls] + dw_ref[hist:kw, ls] * g_t[l, r0:r0 + CMOD_ROW_TILE, :]
                for s in range(1, kw):
                    acc = acc + (dw_ref[hist - s:kw - s, ls]
                                 * g_t[l, r0 - s * nb:r0 - s * nb + CMOD_ROW_TILE, :])
                c_t[l, rb * CMOD_ROW_TILE:(rb + 1) * CMOD_ROW_TILE, :] = acc
    tail = g_t[:, rows:rows + hrows, :]
    so_ref[...] = jnp.concatenate([tail[l] for l in range(d // LANES)], axis=-1)
    g_t[:, 0:hrows, :] = tail

    conv = jnp.concatenate(
        [jnp.concatenate([c_t[l, pl.ds(b, tr, stride=nb), :] for l in range(d // LANES)], axis=-1)
         for b in range(nb)], axis=0)
    c = jax.nn.silu(_layer_norm(conv, lng_ref[j:j + 1, :], lnb_ref[j:j + 1, :]))
    xo_ref[...] = (x + _mm(c, pw2_ref[...])).reshape(nb, tr, d)


def _p_xattn_kernel(x_ref, nm_ref, wq_ref, kt_ref, v_ref, wo_ref, xo_ref, *, layer):
    tt, d = x_ref.shape
    dh = d // XA_HEADS
    x = x_ref[...]
    q = _mm(_rms(x, nm_ref[layer:layer + 1, :]), wq_ref[...]).astype(BF16)
    cols = [slice(h * dh, (h + 1) * dh) for h in range(XA_HEADS)]
    scores = [jnp.dot(q[:, cs], kt_ref[cs, :], preferred_element_type=F32) * (dh ** -0.5)
              for cs in cols]
    heads = []
    for s, cs in zip(scores, cols):
        e = jnp.exp(s - jnp.max(s, axis=-1, keepdims=True))
        p = e / jnp.sum(e, axis=-1, keepdims=True)
        heads.append(jnp.dot(p.astype(BF16), v_ref[:, cs], preferred_element_type=F32))
    o = jnp.concatenate(heads, axis=-1)
    xo_ref[...] = x + _mm(o, wo_ref[...])


def _sample_attend(q_ref, k_ref, v_ref, o_ref, seqs):
    dh = q_ref.shape[-1]
    for b in seqs:
        q = q_ref[b] * (dh ** -0.5)
        s = jnp.sum(k_ref[b] * q[None], axis=-1, keepdims=True)
        e = jnp.exp(s - jnp.max(s, axis=0, keepdims=True))
        o_ref[b] = jnp.sum(e * v_ref[b], axis=0) / jnp.sum(e, axis=0)


def _p_ffn_kernel(x_ref, nm_ref, wup_ref, cw_ref, cb_ref, wdn_ref, nf_ref, q_ref, k_ref, v_ref,
                  xo_ref, so_ref, ao_ref, buf, *, layer, final):
    tt, d = x_ref.shape
    dff = wdn_ref.shape[0]
    kw = cw_ref.shape[0]
    hist = kw - 1
    tc = pl.program_id(1)

    @pl.when(tc == 0)
    def _():
        buf[0:SUBLANES, :] = jnp.zeros((SUBLANES, dff), F32)

    x = x_ref[...]
    hn = _rms(x, nm_ref[layer:layer + 1, :]).astype(BF16)
    def up(c):
        lo = c * FFN_COL_TILE
        return (jnp.dot(hn, wup_ref[:, lo:lo + FFN_COL_TILE], preferred_element_type=F32),
                jnp.dot(hn, wup_ref[:, dff + lo:dff + lo + FFN_COL_TILE],
                        preferred_element_type=F32))

    n_col = dff // FFN_COL_TILE
    n_seq = q_ref.shape[0]
    acc = None
    nxt = up(0)
    for c in range(n_col):
        cs = slice(c * FFN_COL_TILE, (c + 1) * FFN_COL_TILE)
        g, u = nxt
        if c + 1 < n_col:
            nxt = up(c + 1)
        _sample_attend(q_ref, k_ref, v_ref, ao_ref,
                       range(c * n_seq // n_col, (c + 1) * n_seq // n_col))
        buf[SUBLANES:SUBLANES + tt, cs] = g
        conv = cb_ref[layer:layer + 1, cs] + cw_ref[hist:kw, cs] * g
        for k in range(hist):
            off = SUBLANES - hist + k
            conv = conv + cw_ref[k:k + 1, cs] * buf[off:off + tt, cs]
        act = (_gelu(conv) * u).astype(BF16)
        part = jnp.dot(act, wdn_ref[cs, :], preferred_element_type=F32)
        acc = part if acc is None else acc + part
    tail = buf[SUBLANES + tt - hist:SUBLANES + tt, :]
    so_ref[...] = tail
    buf[SUBLANES - hist:SUBLANES, :] = tail

    xn = x + acc
    if final:
        xn = _rms(xn, nf_ref[...])
    xo_ref[...] = xn


def _s_lru_kernel(x_ref, h0_ref, cst_ref, nm_ref, win_ref, cw_ref, cb_ref, wa_ref, ba_ref,
                  wx_ref, bx_ref, lam_ref, wout_ref, nxa_ref, wq_ref,
                  xo_ref, q_ref, ho_ref, co_ref, *, layer, j):
    d = x_ref.shape[-1]
    kw = cw_ref.shape[0]
    hist = kw - 1
    x = x_ref[...]
    hn = _rms(x, nm_ref[layer:layer + 1, :]).astype(BF16)
    gate = jnp.dot(hn, win_ref[:, :d], preferred_element_type=F32)
    rec = jnp.dot(hn, win_ref[:, d:], preferred_element_type=F32)

    conv = cb_ref[j:j + 1, :] + cw_ref[hist:kw, :] * rec
    for k in range(hist):
        conv = conv + cw_ref[k:k + 1, :] * cst_ref[k]
    for k in range(hist - 1):
        co_ref[k] = cst_ref[k + 1]
    co_ref[hist - 1] = rec

    a, u = _lru_gates(conv, wa_ref, ba_ref[j:j + 1, :], wx_ref, bx_ref[j:j + 1, :],
                      lam_ref[j:j + 1, :])
    h = a * h0_ref[...] + u
    ho_ref[...] = h
    x1 = x + _mm(_gelu(gate) * h, wout_ref[...])
    xo_ref[...] = x1
    q_ref[...] = _mm(_rms(x1, nxa_ref[layer:layer + 1, :]), wq_ref[...])


def _s_ffn_kernel(x_ref, o_ref, wo_ref, nm_ref, wup_ref, cw_ref, cb_ref, wdn_ref, st_ref, nf_ref,
                  xo_ref, so_ref, *, layer, final):
    dff = wdn_ref.shape[0]
    kw = cw_ref.shape[0]
    hist = kw - 1
    x = x_ref[...] + _mm(o_ref[...], wo_ref[...])
    hn = _rms(x, nm_ref[layer:layer + 1, :]).astype(BF16)
    g = jnp.dot(hn, wup_ref[:, :dff], preferred_element_type=F32)
    u = jnp.dot(hn, wup_ref[:, dff:], preferred_element_type=F32)
    conv = cb_ref[layer:layer + 1, :] + cw_ref[hist:kw, :] * g
    for k in range(hist):
        conv = conv + cw_ref[k:k + 1, :] * st_ref[:, k, :]
    for k in range(hist - 1):
        so_ref[:, k, :] = st_ref[:, k + 1, :]
    so_ref[:, hist - 1, :] = g
    xn = x + _mm(_gelu(conv) * u, wdn_ref[...])
    if final:
        xn = _rms(xn, nf_ref[...])
    xo_ref[...] = xn


def _s_cmod_pre_kernel(x_ref, nm_ref, pw1_ref, b1_ref, g_ref, *, layer, j):
    d = x_ref.shape[-1]
    hn = _rms(x_ref[...], nm_ref[layer:layer + 1, :])
    g_ref[...] = _glu(_mm(hn, pw1_ref[...]) + b1_ref[j:j + 1, :], d)


def _s_cmod_conv_kernel(st_ref, g_ref, dw_ref, dwb_ref, lng_ref, lnb_ref, c_ref, so_ref, *, j):
    hist = st_ref.shape[0]
    kw = hist + 1
    g = g_ref[...]
    c = dwb_ref[j:j + 1, :] + dw_ref[hist:kw, :] * g
    for k in range(hist):
        c = c + dw_ref[k:k + 1, :] * st_ref[k]
    for k in range(hist - 1):
        so_ref[k] = st_ref[k + 1]
    so_ref[hist - 1] = g
    c_ref[...] = jax.nn.silu(_layer_norm(c, lng_ref[j:j + 1, :], lnb_ref[j:j + 1, :]))


def _s_cmod_post_kernel(x_ref, c_ref, pw2_ref, nxa_ref, wq_ref, xo_ref, q_ref, *, layer):
    x1 = x_ref[...] + _mm(c_ref[...], pw2_ref[...])
    xo_ref[...] = x1
    q_ref[...] = _mm(_rms(x1, nxa_ref[layer:layer + 1, :]), wq_ref[...])


def _params(n_grid):
    return pltpu.CompilerParams(dimension_semantics=("arbitrary",) * n_grid,
                                vmem_limit_bytes=VMEM_LIMIT)


def _resident(shape, lead=None):
    if lead is None:
        nd = len(shape)
        return pl.BlockSpec(shape, lambda *_: (0,) * nd, pipeline_mode=pl.Buffered(1))
    nd = len(shape) - 1
    return pl.BlockSpec((None,) + tuple(shape[1:]), lambda *_: (lead,) + (0,) * nd,
                        pipeline_mode=pl.Buffered(1))


_whole = _resident


def _whole_out(shape):
    nd = len(shape)
    return pl.BlockSpec(shape, lambda *_: (0,) * nd)


def _sds(shape, dtype=F32):
    return jax.ShapeDtypeStruct(shape, dtype)


def _flat_cast_blocks(to_cast, steps, step_index):
    flat = [a.reshape(-1, a.shape[-1]) for a in to_cast]
    for a in flat:
        assert a.shape[0] % (steps * BF16_SUBLANES) == 0
    specs = [pl.BlockSpec((a.shape[0] // steps, a.shape[1]), lambda *g: (step_index(*g), 0))
             for a in flat]
    return flat, specs


def _prompt_memkv(mem, mem_norm3, wkv, heads, to_cast):
    depth = wkv.shape[0]
    b, m, d = mem.shape
    dh = d // heads
    flat, cast_specs = _flat_cast_blocks(to_cast, depth * b, lambda i, n: i * b + n)
    outs = pl.pallas_call(
        functools.partial(_p_memkv_kernel, n_cast=len(flat)),
        grid=(depth, b),
        in_specs=[pl.BlockSpec((None, m, d), lambda i, n: (n, 0, 0)),
                  pl.BlockSpec((None, 1, d), lambda i, n: (i, 0, 0)),
                  pl.BlockSpec((None, d, 2 * d), lambda i, n: (i, 0, 0))] + cast_specs,
        out_specs=[pl.BlockSpec((None, None, m, heads, dh), lambda i, n: (i, n, 0, 0, 0)),
                   pl.BlockSpec((None, None, m, heads, dh), lambda i, n: (i, n, 0, 0, 0)),
                   pl.BlockSpec((None, None, d, m), lambda i, n: (i, n, 0, 0)),
                   pl.BlockSpec((None, None, m, d), lambda i, n: (i, n, 0, 0))] + cast_specs,
        out_shape=[_sds((depth, b, m, heads, dh)), _sds((depth, b, m, heads, dh)),
                   _sds((depth, b, d, m), BF16), _sds((depth, b, m, d), BF16)]
                  + [_sds(a.shape, BF16) for a in flat],
        scratch_shapes=[pltpu.VMEM((d, 2 * d), BF16)],
        compiler_params=_params(2),
        name="prompt_memkv",
    )(mem, mem_norm3, wkv, *flat)
    return outs[:4], [o.reshape(a.shape) for o, a in zip(outs[4:], to_cast)]


def _x_spec(tt, d):
    return pl.BlockSpec((None, tt, d), lambda n, t: (n, t, 0))


def _state_spec(rows, cols):
    return pl.BlockSpec((None, rows, cols), lambda n, t: (n, 0, 0))


def _xt_spec(b, tr, d):
    return pl.BlockSpec((b, tr, d), lambda t: (0, t, 0))


def _steps_to_sequences(s, b):
    hrows, d = s.shape
    return s.reshape(hrows // b, b, d).transpose(1, 0, 2)


def _prompt_lru(x, layer, j, w, to_cast):
    b, t, d = x.shape
    tr = LRU_TIME_ROWS
    hist = w['lru_conv_w'].shape[1] - 1
    assert b == SUBLANES and tr % SUBLANES == 0 and tr >= hist and d % LANES == 0
    nl, hrows = d // LANES, hist * b
    steps = t // tr
    flat, cast_specs = _flat_cast_blocks(to_cast, steps, lambda s: s)
    ins = [(x, _xt_spec(b, tr, d)),
           (w['norm_mix'], _resident(w['norm_mix'].shape)),
           (w['lru_w_in'], _resident(w['lru_w_in'].shape, j)),
           (w['lru_conv_w'], _resident(w['lru_conv_w'].shape, j)),
           (w['lru_conv_b'], _resident(w['lru_conv_b'].shape)),
           (w['lru_wa'], _resident(w['lru_wa'].shape, j)),
           (w['lru_ba'], _resident(w['lru_ba'].shape)),
           (w['lru_wx'], _resident(w['lru_wx'].shape, j)),
           (w['lru_bx'], _resident(w['lru_bx'].shape)),
           (w['lru_lambda'], _resident(w['lru_lambda'].shape)),
           (w['lru_w_out'], _resident(w['lru_w_out'].shape, j))]
    ins += list(zip(flat, cast_specs))
    outs = pl.pallas_call(
        functools.partial(_p_lru_kernel, layer=layer, j=j, n_cast=len(flat)),
        grid=(steps,),
        in_specs=[s for _, s in ins],
        out_specs=[_xt_spec(b, tr, d), _whole_out((b, d)), _whole_out((hrows, d))] + cast_specs,
        out_shape=([_sds((b, t, d)), _sds((b, d)), _sds((hrows, d))]
                   + [_sds(a.shape, BF16) for a in flat]),
        scratch_shapes=[pltpu.VMEM((nl, hrows + b * tr, LANES), F32),
                        pltpu.VMEM((nl, b * tr, LANES), F32),
                        pltpu.VMEM((b, d), F32)],
        compiler_params=_params(1),
        name="prompt_lru",
    )(*[a for a, _ in ins])
    casted = [o.reshape(a.shape) for o, a in zip(outs[3:], to_cast)]
    return outs[0], outs[1], outs[2], casted


def _prompt_cmod(x, layer, j, w):
    b, t, d = x.shape
    tr = CMOD_TIME_ROWS
    hist = w['cm_dw_w'].shape[1] - 1
    assert b == SUBLANES and tr % SUBLANES == 0 and tr >= hist and d % CMOD_COL_TILE == 0
    nl, hrows = d // LANES, hist * b
    ins = [(x, _xt_spec(b, tr, d)),
           (w['norm_mix'], _resident(w['norm_mix'].shape)),
           (w['cm_w_pw1'], _resident(w['cm_w_pw1'].shape, j)),
           (w['cm_b_pw1'], _resident(w['cm_b_pw1'].shape)),
           (w['cm_dw_w'], _resident(w['cm_dw_w'].shape, j)),
           (w['cm_dw_b'], _resident(w['cm_dw_b'].shape)),
           (w['cm_ln_g'], _resident(w['cm_ln_g'].shape)),
           (w['cm_ln_b'], _resident(w['cm_ln_b'].shape)),
           (w['cm_w_pw2'], _resident(w['cm_w_pw2'].shape, j))]
    return pl.pallas_call(
        functools.partial(_p_cmod_kernel, layer=layer, j=j),
        grid=(t // tr,),
        in_specs=[s for _, s in ins],
        out_specs=[_xt_spec(b, tr, d), _whole_out((hrows, d))],
        out_shape=[_sds((b, t, d)), _sds((hrows, d))],
        scratch_shapes=[pltpu.VMEM((nl, hrows + b * tr, LANES), F32),
                        pltpu.VMEM((nl, b * tr, LANES), F32)],
        compiler_params=_params(1),
        name="prompt_cmod",
    )(*[a for a, _ in ins])


def _prompt_xattn(x, layer, kt, vb, w):
    b, t, d = x.shape
    tt = XATTN_T_TILE
    m = vb.shape[2]
    ins = [(x, _x_spec(tt, d)),
           (w['norm_xa'], _resident(w['norm_xa'].shape)),
           (w['xa_w_q'], _resident(w['xa_w_q'].shape, layer)),
           (kt, pl.BlockSpec((None, None, d, m), lambda n, t_: (layer, n, 0, 0))),
           (vb, pl.BlockSpec((None, None, m, d), lambda n, t_: (layer, n, 0, 0))),
           (w['xa_w_o'], _resident(w['xa_w_o'].shape, layer))]
    return pl.pallas_call(
        functools.partial(_p_xattn_kernel, layer=layer),
        grid=(b, t // tt),
        in_specs=[s for _, s in ins],
        out_specs=_x_spec(tt, d),
        out_shape=_sds((b, t, d)),
        compiler_params=_params(2),
        name="prompt_xattn",
    )(*[a for a, _ in ins])


def _prompt_ffn_sample_attn(x, layer, final, q, cache_k, cache_v, w):
    b, t, d = x.shape
    tt = FFN_T_TILE
    n_t = t // tt
    dff = w['ffn_w_down'].shape[1]
    hist = w['ffn_conv_w'].shape[1] - 1
    r = q.shape[0]
    _, _, m, nh, dh = cache_k.shape
    assert r % (b * n_t) == 0
    nb = r // (b * n_t)
    kv_spec = pl.BlockSpec((None, nb, m, nh, dh), lambda n, s: (layer, n * n_t + s, 0, 0, 0))
    row_spec = pl.BlockSpec((nb, nh, dh), lambda n, s: (n * n_t + s, 0, 0))
    ins = [(x, _x_spec(tt, d)),
           (w['norm_ffn'], _resident(w['norm_ffn'].shape)),
           (w['ffn_w_up'], _resident(w['ffn_w_up'].shape, layer)),
           (w['ffn_conv_w'], _resident(w['ffn_conv_w'].shape, layer)),
           (w['ffn_conv_b'], _resident(w['ffn_conv_b'].shape)),
           (w['ffn_w_down'], _resident(w['ffn_w_down'].shape, layer)),
           (w['norm_final'], _resident(w['norm_final'].shape)),
           (q.reshape(r, nh, dh), row_spec), (cache_k, kv_spec), (cache_v, kv_spec)]
    xo, fbuf, o = pl.pallas_call(
        functools.partial(_p_ffn_kernel, layer=layer, final=final),
        grid=(b, n_t),
        in_specs=[s for _, s in ins],
        out_specs=[_x_spec(tt, d), _state_spec(hist, dff), row_spec],
        out_shape=[_sds((b, t, d)), _sds((b, hist, dff)), _sds((r, nh, dh))],
        scratch_shapes=[pltpu.VMEM((SUBLANES + tt, dff), F32)],
        compiler_params=pltpu.CompilerParams(dimension_semantics=("arbitrary", "arbitrary"),
                                             vmem_limit_bytes=FFN_VMEM_LIMIT),
        name="prompt_ffn_sample_attn",
    )(*[a for a, _ in ins])
    return xo, fbuf, o.reshape(r, d)


def _sample_lru(x, h0, cst, layer, j, w):
    r, d = x.shape
    ins = [(x, _whole(x.shape)), (h0, _whole(h0.shape, j)), (cst, _whole(cst.shape, j)),
           (w['norm_mix'], _whole(w['norm_mix'].shape)),
           (w['lru_w_in'], _whole(w['lru_w_in'].shape, j)),
           (w['lru_conv_w'], _whole(w['lru_conv_w'].shape, j)),
           (w['lru_conv_b'], _whole(w['lru_conv_b'].shape)),
           (w['lru_wa'], _whole(w['lru_wa'].shape, j)),
           (w['lru_ba'], _whole(w['lru_ba'].shape)),
           (w['lru_wx'], _whole(w['lru_wx'].shape, j)),
           (w['lru_bx'], _whole(w['lru_bx'].shape)),
           (w['lru_lambda'], _whole(w['lru_lambda'].shape)),
           (w['lru_w_out'], _whole(w['lru_w_out'].shape, j)),
           (w['norm_xa'], _whole(w['norm_xa'].shape)),
           (w['xa_w_q'], _whole(w['xa_w_q'].shape, layer))]
    return pl.pallas_call(
        functools.partial(_s_lru_kernel, layer=layer, j=j),
        in_specs=[s for _, s in ins],
        out_specs=[_whole_out((r, d)), _whole_out((r, d)), _whole_out((r, d)),
                   _whole_out(cst.shape[1:])],
        out_shape=[_sds((r, d)), _sds((r, d)), _sds((r, d)), _sds(cst.shape[1:])],
        grid=(1,),
        compiler_params=_params(1),
        name="sample_lru",
    )(*[a for a, _ in ins])


def _sample_ffn(x, o, st, layer, final, w):
    r, d = x.shape
    ins = [(x, _whole(x.shape)), (o, _whole(o.shape)),
           (w['xa_w_o'], _whole(w['xa_w_o'].shape, layer)),
           (w['norm_ffn'], _whole(w['norm_ffn'].shape)),
           (w['ffn_w_up'], _whole(w['ffn_w_up'].shape, layer)),
           (w['ffn_conv_w'], _whole(w['ffn_conv_w'].shape, layer)),
           (w['ffn_conv_b'], _whole(w['ffn_conv_b'].shape)),
           (w['ffn_w_down'], _whole(w['ffn_w_down'].shape, layer)),
           (st, _whole(st.shape, layer)),
           (w['norm_final'], _whole(w['norm_final'].shape))]
    return pl.pallas_call(
        functools.partial(_s_ffn_kernel, layer=layer, final=final),
        in_specs=[s for _, s in ins],
        out_specs=[_whole_out((r, d)), _whole_out(st.shape[1:])],
        out_shape=[_sds((r, d)), _sds(st.shape[1:])],
        grid=(1,),
        compiler_params=_params(1),
        name="sample_ffn",
    )(*[a for a, _ in ins])


def _sample_cmod(x, st, layer, j, w):
    r, d = x.shape
    ins = [(x, _whole(x.shape)),
           (w['norm_mix'], _whole(w['norm_mix'].shape)),
           (w['cm_w_pw1'], _whole(w['cm_w_pw1'].shape, j)),
           (w['cm_b_pw1'], _whole(w['cm_b_pw1'].shape))]
    g = pl.pallas_call(
        functools.partial(_s_cmod_pre_kernel, layer=layer, j=j),
        in_specs=[s for _, s in ins],
        out_specs=_whole_out((r, d)),
        out_shape=_sds((r, d)),
        grid=(1,),
        compiler_params=_params(1),
        name="sample_cmod_pre",
    )(*[a for a, _ in ins])

    rows = SAMPLE_CONV_ROWS
    hist = st.shape[1]
    ins = [(st, pl.BlockSpec((None, hist, rows, d), lambda n: (j, 0, n, 0))),
           (g, pl.BlockSpec((rows, d), lambda n: (n, 0))),
           (w['cm_dw_w'], _resident(w['cm_dw_w'].shape, j)),
           (w['cm_dw_b'], _resident(w['cm_dw_b'].shape)),
           (w['cm_ln_g'], _resident(w['cm_ln_g'].shape)),
           (w['cm_ln_b'], _resident(w['cm_ln_b'].shape))]
    c, new_st = pl.pallas_call(
        functools.partial(_s_cmod_conv_kernel, j=j),
        grid=(r // rows,),
        in_specs=[s for _, s in ins],
        out_specs=[pl.BlockSpec((rows, d), lambda n: (n, 0)),
                   pl.BlockSpec((hist, rows, d), lambda n: (0, n, 0))],
        out_shape=[_sds((r, d)), _sds((hist, r, d))],
        compiler_params=_params(1),
        name="sample_cmod_conv",
    )(*[a for a, _ in ins])

    ins = [(x, _whole(x.shape)), (c, _whole(c.shape)),
           (w['cm_w_pw2'], _whole(w['cm_w_pw2'].shape, j)),
           (w['norm_xa'], _whole(w['norm_xa'].shape)),
           (w['xa_w_q'], _whole(w['xa_w_q'].shape, layer))]
    x1, q = pl.pallas_call(
        functools.partial(_s_cmod_post_kernel, layer=layer),
        in_specs=[s for _, s in ins],
        out_specs=[_whole_out((r, d)), _whole_out((r, d))],
        out_shape=[_sds((r, d)), _sds((r, d))],
        grid=(1,),
        compiler_params=_params(1),
        name="sample_cmod_post",
    )(*[a for a, _ in ins])
    return x1, q, new_st


_MIXER0_MATMUL_WEIGHTS = ('lru_w_in', 'lru_wa', 'lru_wx', 'lru_w_out')
_LATE_MATMUL_WEIGHTS = ('cm_w_pw1', 'cm_w_pw2', 'xa_w_q', 'xa_w_o', 'ffn_w_up', 'ffn_w_down')


def kernel(x_prompt, x_sample, state_lru_h, state_lru_conv, state_cmod_conv, state_ffn_conv, cache_mem_k, cache_mem_v, mem_prompt, norm_mix, norm_xa, norm_ffn, norm_final, lru_w_in, lru_conv_w, lru_conv_b, lru_wa, lru_ba, lru_wx, lru_bx, lru_lambda, lru_w_out, cm_w_pw1, cm_b_pw1, cm_dw_w, cm_dw_b, cm_ln_g, cm_ln_b, cm_w_pw2, mem_norm, xa_w_q, xa_w_kv, xa_w_o, ffn_w_up, ffn_conv_w, ffn_conv_b, ffn_w_down):
    w = dict(norm_mix=norm_mix, norm_xa=norm_xa, norm_ffn=norm_ffn,
             norm_final=norm_final.reshape(1, -1),
             lru_w_in=lru_w_in, lru_conv_w=lru_conv_w, lru_conv_b=lru_conv_b, lru_wa=lru_wa,
             lru_ba=lru_ba, lru_wx=lru_wx, lru_bx=lru_bx, lru_lambda=lru_lambda,
             lru_w_out=lru_w_out, cm_w_pw1=cm_w_pw1, cm_b_pw1=cm_b_pw1, cm_dw_w=cm_dw_w,
             cm_dw_b=cm_dw_b, cm_ln_g=cm_ln_g, cm_ln_b=cm_ln_b, cm_w_pw2=cm_w_pw2,
             xa_w_q=xa_w_q, xa_w_kv=xa_w_kv, xa_w_o=xa_w_o,
             ffn_w_up=ffn_w_up, ffn_conv_w=ffn_conv_w, ffn_conv_b=ffn_conv_b,
             ffn_w_down=ffn_w_down)

    depth = norm_mix.shape[0]
    bsz, _, d = x_prompt.shape
    n_mem = mem_prompt.shape[1]
    heads = cache_mem_k.shape[3]

    (k_p, v_p, kt_p, vb_p), casted = _prompt_memkv(
        mem_prompt, mem_norm.reshape(depth, 1, d), w['xa_w_kv'], heads,
        [w[n] for n in _MIXER0_MATMUL_WEIGHTS])
    w.update(zip(_MIXER0_MATMUL_WEIGHTS, casted))
    x = x_prompt
    r = x_sample.shape[0]
    xs = x_sample.reshape(r, d)
    lconv = state_lru_conv.transpose(0, 2, 1, 3)
    cconv = state_cmod_conv.transpose(0, 2, 1, 3)
    p_h, p_lconv, p_cconv, p_fconv = [], [], [], []
    s_h, s_lconv, s_cconv, s_fconv = [], [], [], []
    for i in range(depth):
        j = i // 2
        if i % 2 == 0:
            late = [n for n in _LATE_MATMUL_WEIGHTS if w[n].dtype != BF16]
            x, h_last, cbuf, casted = _prompt_lru(x, i, j, w, [w[n] for n in late])
            w.update(zip(late, casted))
            p_h.append(h_last)
            p_lconv.append(_steps_to_sequences(cbuf, bsz))
            xs, q, h_new, c_new = _sample_lru(xs, state_lru_h, lconv, i, j, w)
            s_h.append(h_new)
            s_lconv.append(c_new.transpose(1, 0, 2))
        else:
            x, cbuf = _prompt_cmod(x, i, j, w)
            p_cconv.append(_steps_to_sequences(cbuf, bsz))
            xs, q, c_new = _sample_cmod(xs, cconv, i, j, w)
            s_cconv.append(c_new.transpose(1, 0, 2))
        x = _prompt_xattn(x, i, kt_p, vb_p, w)
        x, fbuf, o = _prompt_ffn_sample_attn(x, i, i == depth - 1, q, cache_mem_k, cache_mem_v, w)
        p_fconv.append(fbuf)
        xs, f_new = _sample_ffn(xs, o, state_ffn_conv, i, i == depth - 1, w)
        s_fconv.append(f_new)
    y_prompt = x
    y_sample = xs.reshape(x_sample.shape)

    return (y_prompt, y_sample,
            jnp.stack(p_h), jnp.stack(p_lconv), jnp.stack(p_cconv), jnp.stack(p_fconv),
            k_p, v_p,
            jnp.stack(s_h), jnp.stack(s_lconv), jnp.stack(s_cconv), jnp.stack(s_fconv))
```

```python
import functools

import jax
import jax.numpy as jnp
from jax import lax
from jax.experimental import pallas as pl
from jax.experimental.pallas import tpu as pltpu

F32 = jnp.float32
BF16 = jnp.bfloat16

EPS = 1e-6
RG_C = 8.0
N_RG_BLOCKS = 4
XA_HEADS = 4
SUBLANES = 8
LANES = 128
BF16_SUBLANES = 16
VMEM_LIMIT = 56 << 20
FFN_VMEM_LIMIT = 60 << 20

LRU_TIME_ROWS = 64
CMOD_TIME_ROWS = 64
XATTN_T_TILE = 1024
FFN_T_TILE = 512
FFN_COL_TILE = 1536
CMOD_COL_TILE = 256
CMOD_ROW_TILE = 512
SAMPLE_CONV_ROWS = 32


def _rms(x, g):
    return x * lax.rsqrt(jnp.mean(x * x, axis=-1, keepdims=True) + EPS) * g


def _layer_norm(x, g, b):
    mu = jnp.mean(x, axis=-1, keepdims=True)
    xc = x - mu
    var = jnp.mean(xc * xc, axis=-1, keepdims=True)
    return xc * lax.rsqrt(var + EPS) * g + b


def _mm(a, w):
    return jnp.dot(a.astype(BF16), w, preferred_element_type=F32)


_GELU_C = 0.7978845608028654


def _gelu(x):
    inner = x * (_GELU_C + (_GELU_C * 0.044715) * (x * x))
    return x * (0.5 + 0.5 * jnp.tanh(inner))


def _softplus(z):
    return jnp.maximum(z, 0.0) + jnp.log1p(jnp.exp(-jnp.abs(z)))


def _block_diag(xb, w_ref):
    blk = xb.shape[-1] // N_RG_BLOCKS
    return jnp.concatenate(
        [jnp.dot(xb[:, n * blk:(n + 1) * blk], w_ref[n], preferred_element_type=F32)
         for n in range(N_RG_BLOCKS)], axis=-1)


def _lru_decay_input(rec, ga, gx, lam):
    log_a = -RG_C * jax.nn.sigmoid(ga) * _softplus(-lam)
    a = jnp.exp(log_a)
    u = jnp.sqrt(1.0 - a * a) * jax.nn.sigmoid(gx) * rec
    return a, u


def _lru_gates(rec, wa_ref, ba, wx_ref, bx, lam):
    rb = rec.astype(BF16)
    return _lru_decay_input(rec, _block_diag(rb, wa_ref) + ba, _block_diag(rb, wx_ref) + bx, lam)


def _glu(ab, d):
    return ab[:, :d] * jax.nn.sigmoid(ab[:, d:])


def _cast_blocks(srcs, dsts):
    for src, dst in zip(srcs, dsts):
        dst[...] = src[...].astype(BF16)


def _p_memkv_kernel(m_ref, g_ref, wkv_ref, *rest, n_cast):
    cast_in = rest[:n_cast]
    k_ref, v_ref, kt_ref, vb_ref = rest[n_cast:n_cast + 4]
    cast_out = rest[n_cast + 4:2 * n_cast + 4]
    wkv_s, = rest[2 * n_cast + 4:]
    _cast_blocks(cast_in, cast_out)

    @pl.when(pl.program_id(1) == 0)
    def _():
        wkv_s[...] = wkv_ref[...].astype(BF16)

    d = m_ref.shape[-1]
    hn = _rms(m_ref[...], g_ref[...]).astype(BF16)
    k = jnp.dot(hn, wkv_s[:, :d], preferred_element_type=F32)
    v = jnp.dot(hn, wkv_s[:, d:], preferred_element_type=F32)
    dh = d // k_ref.shape[1]
    for h in range(k_ref.shape[1]):
        k_ref[:, h, :] = k[:, h * dh:(h + 1) * dh]
        v_ref[:, h, :] = v[:, h * dh:(h + 1) * dh]
    kt_ref[...] = k.T.astype(BF16)
    vb_ref[...] = v.astype(BF16)


def _p_lru_kernel(x_ref, nm_ref, win_ref, cw_ref, cb_ref, wa_ref, ba_ref, wx_ref, bx_ref,
                  lam_ref, wout_ref, *rest, layer, j, n_cast):
    cast_in = rest[:n_cast]
    xo_ref, ho_ref, co_ref = rest[n_cast:n_cast + 3]
    cast_out = rest[n_cast + 3:2 * n_cast + 3]
    rec_t, y_t, h_s = rest[2 * n_cast + 3:]
    _cast_blocks(cast_in, cast_out)

    nb, tr, d = x_ref.shape
    rows = nb * tr
    kw = cw_ref.shape[0]
    hist = kw - 1
    hrows = hist * nb
    nl = d // LANES
    tc = pl.program_id(0)

    @pl.when(tc == 0)
    def _():
        rec_t[:, 0:hrows, :] = jnp.zeros((nl, hrows, LANES), F32)
        h_s[...] = jnp.zeros_like(h_s)

    x = x_ref[...].reshape(rows, d)
    hn = _rms(x, nm_ref[layer:layer + 1, :]).astype(BF16)
    cblk = d // N_RG_BLOCKS
    lanes_per_blk = cblk // LANES

    def in_proj(n):
        lo = n * cblk
        return (jnp.dot(hn, win_ref[:, lo:lo + cblk], preferred_element_type=F32),
                jnp.dot(hn, win_ref[:, d + lo:d + lo + cblk], preferred_element_type=F32))

    out = None
    nxt = in_proj(0)
    for n in range(N_RG_BLOCKS):
        gate, rec = nxt
        if n + 1 < N_RG_BLOCKS:
            nxt = in_proj(n + 1)
        cs = slice(n * cblk, (n + 1) * cblk)
        slabs = []
        for lt in range(lanes_per_blk):
            l = n * lanes_per_blk + lt
            ls = slice(l * LANES, (l + 1) * LANES)
            for b in range(nb):
                rec_t[l, pl.ds(hrows + b, tr, stride=nb), :] = (
                    rec[b * tr:(b + 1) * tr, lt * LANES:(lt + 1) * LANES])
            acc = cb_ref[j:j + 1, ls] + cw_ref[hist:kw, ls] * rec_t[l, hrows:hrows + rows, :]
            for k in range(hist):
                acc = acc + cw_ref[k:k + 1, ls] * rec_t[l, k * nb:k * nb + rows, :]
            slabs.append(acc)
        conv = jnp.concatenate(slabs, axis=-1)
        cb16 = conv.astype(BF16)
        ga = jnp.dot(cb16, wa_ref[n], preferred_element_type=F32) + ba_ref[j:j + 1, cs]
        gx = jnp.dot(cb16, wx_ref[n], preferred_element_type=F32) + bx_ref[j:j + 1, cs]
        a, u = _lru_decay_input(conv, ga, gx, lam_ref[j:j + 1, cs])

        h = h_s[:, cs]
        for t in range(tr):
            h = a[t * nb:(t + 1) * nb, :] * h + u[t * nb:(t + 1) * nb, :]
            for lt in range(lanes_per_blk):
                y_t[n * lanes_per_blk + lt, t * nb:(t + 1) * nb, :] = (
                    h[:, lt * LANES:(lt + 1) * LANES])
        h_s[:, cs] = h
        ho_ref[:, cs] = h

        y = jnp.concatenate(
            [jnp.concatenate([y_t[n * lanes_per_blk + lt, pl.ds(b, tr, stride=nb), :]
                              for lt in range(lanes_per_blk)], axis=-1)
             for b in range(nb)], axis=0)
        part = jnp.dot((_gelu(gate) * y).astype(BF16), wout_ref[cs, :],
                       preferred_element_type=F32)
        out = part if out is None else out + part
    tail = rec_t[:, rows:rows + hrows, :]
    co_ref[...] = jnp.concatenate([tail[l] for l in range(nl)], axis=-1)
    rec_t[:, 0:hrows, :] = tail
    xo_ref[...] = (x + out).reshape(nb, tr, d)


def _p_cmod_kernel(x_ref, nm_ref, pw1_ref, b1_ref, dw_ref, dwb_ref, lng_ref, lnb_ref, pw2_ref,
                   xo_ref, so_ref, g_t, c_t, *, layer, j):
    nb, tr, d = x_ref.shape
    rows = nb * tr
    kw = dw_ref.shape[0]
    hist = kw - 1
    hrows = hist * nb
    lanes_per_col = CMOD_COL_TILE // LANES
    tc = pl.program_id(0)

    @pl.when(tc == 0)
    def _():
        g_t[:, 0:hrows, :] = jnp.zeros((d // LANES, hrows, LANES), F32)

    x = x_ref[...].reshape(rows, d)
    hn = _rms(x, nm_ref[layer:layer + 1, :]).astype(BF16)
    def pw1(c):
        lo = c * CMOD_COL_TILE
        cs = slice(lo, lo + CMOD_COL_TILE)
        gs = slice(d + lo, d + lo + CMOD_COL_TILE)
        return (jnp.dot(hn, pw1_ref[:, cs], preferred_element_type=F32) + b1_ref[j:j + 1, cs],
                jnp.dot(hn, pw1_ref[:, gs], preferred_element_type=F32) + b1_ref[j:j + 1, gs])

    n_col = d // CMOD_COL_TILE
    nxt = pw1(0)
    for c in range(n_col):
        a, bg = nxt
        if c + 1 < n_col:
            nxt = pw1(c + 1)
        g = a * jax.nn.sigmoid(bg)
        for lt in range(lanes_per_col):
            l = c * lanes_per_col + lt
            ls = slice(l * LANES, (l + 1) * LANES)
            for b in range(nb):
                g_t[l, pl.ds(hrows + b, tr, stride=nb), :] = (
                    g[b * tr:(b + 1) * tr, lt * LANES:(lt + 1) * LANES])
            for rb in range(rows // CMOD_ROW_TILE):
                r0 = hrows + rb * CMOD_ROW_TILE
                acc = dwb_ref[j:j + 1, ls] + dw_ref[hist:kw, ls] * g_t[l, r0:r0 + CMOD_ROW_TILE, :]
                for s in range(1, kw):
                    acc = acc + (dw_ref[hist - s:kw - s, ls]
                                 * g_t[l, r0 - s * nb:r0 - s * nb + CMOD_ROW_TILE, :])
                c_t[l, rb * CMOD_ROW_TILE:(rb + 1) * CMOD_ROW_TILE, :] = acc
    tail = g_t[:, rows:rows + hrows, :]
    so_ref[...] = jnp.concatenate([tail[l] for l in range(d // LANES)], axis=-1)
    g_t[:, 0:hrows, :] = tail

    conv = jnp.concatenate(
        [jnp.concatenate([c_t[l, pl.ds(b, tr, stride=nb), :] for l in range(d // LANES)], axis=-1)
         for b in range(nb)], axis=0)
    c = jax.nn.silu(_layer_norm(conv, lng_ref[j:j + 1, :], lnb_ref[j:j + 1, :]))
    xo_ref[...] = (x + _mm(c, pw2_ref[...])).reshape(nb, tr, d)


def _p_xattn_kernel(x_ref, nm_ref, wq_ref, kt_ref, v_ref, wo_ref, xo_ref, *, layer):
    tt, d = x_ref.shape
    dh = d // XA_HEADS
    x = x_ref[...]
    q = _mm(_rms(x, nm_ref[layer:layer + 1, :]), wq_ref[...]).astype(BF16)
    cols = [slice(h * dh, (h + 1) * dh) for h in range(XA_HEADS)]
    scores = [jnp.dot(q[:, cs], kt_ref[cs, :], preferred_element_type=F32) * (dh ** -0.5)
              for cs in cols]
    heads = []
    for s, cs in zip(scores, cols):
        e = jnp.exp(s - jnp.max(s, axis=-1, keepdims=True))
        p = e / jnp.sum(e, axis=-1, keepdims=True)
        heads.append(jnp.dot(p.astype(BF16), v_ref[:, cs], preferred_element_type=F32))
    o = jnp.concatenate(heads, axis=-1)
    xo_ref[...] = x + _mm(o, wo_ref[...])


def _sample_attend(q_ref, k_ref, v_ref, o_ref, seqs, heads):
    rows, lanes = q_ref.shape[-2:]
    dh = lanes * (rows // heads)
    for b in seqs:
        q = q_ref[b] * (dh ** -0.5)
        prod = k_ref[b] * q[None]
        shift = heads
        while shift < rows:
            prod = prod + pltpu.roll(prod, shift, axis=1)
            shift *= 2
        s = jnp.sum(prod, axis=-1, keepdims=True)
        e = jnp.exp(s - jnp.max(s, axis=0, keepdims=True))
        o_ref[b] = jnp.sum(e * v_ref[b], axis=0) / jnp.sum(e, axis=0)


def _p_ffn_kernel(x_ref, nm_ref, wup_ref, cw_ref, cb_ref, wdn_ref, nf_ref, q_ref, k_ref, v_ref,
                  xo_ref, so_ref, ao_ref, buf, *, layer, final):
    tt, d = x_ref.shape
    dff = wdn_ref.shape[0]
    kw = cw_ref.shape[0]
    hist = kw - 1
    tc = pl.program_id(1)

    @pl.when(tc == 0)
    def _():
        buf[0:SUBLANES, :] = jnp.zeros((SUBLANES, dff), F32)

    x = x_ref[...]
    hn = _rms(x, nm_ref[layer:layer + 1, :]).astype(BF16)
    def up(c):
        lo = c * FFN_COL_TILE
        return (jnp.dot(hn, wup_ref[:, lo:lo + FFN_COL_TILE], preferred_element_type=F32),
                jnp.dot(hn, wup_ref[:, dff + lo:dff + lo + FFN_COL_TILE],
                        preferred_element_type=F32))

    n_col = dff // FFN_COL_TILE
    n_seq = q_ref.shape[0]
    acc = None
    nxt = up(0)
    for c in range(n_col):
        cs = slice(c * FFN_COL_TILE, (c + 1) * FFN_COL_TILE)
        g, u = nxt
        if c + 1 < n_col:
            nxt = up(c + 1)
        _sample_attend(q_ref, k_ref, v_ref, ao_ref,
                       range(c * n_seq // n_col, (c + 1) * n_seq // n_col), XA_HEADS)
        buf[SUBLANES:SUBLANES + tt, cs] = g
        conv = cb_ref[layer:layer + 1, cs] + cw_ref[hist:kw, cs] * g
        for k in range(hist):
            off = SUBLANES - hist + k
            conv = conv + cw_ref[k:k + 1, cs] * buf[off:off + tt, cs]
        act = (_gelu(conv) * u).astype(BF16)
        part = jnp.dot(act, wdn_ref[cs, :], preferred_element_type=F32)
        acc = part if acc is None else acc + part
    tail = buf[SUBLANES + tt - hist:SUBLANES + tt, :]
    so_ref[...] = tail
    buf[SUBLANES - hist:SUBLANES, :] = tail

    xn = x + acc
    if final:
        xn = _rms(xn, nf_ref[...])
    xo_ref[...] = xn


def _s_lru_kernel(x_ref, h0_ref, cst_ref, nm_ref, win_ref, cw_ref, cb_ref, wa_ref, ba_ref,
                  wx_ref, bx_ref, lam_ref, wout_ref, nxa_ref, wq_ref,
                  xo_ref, q_ref, ho_ref, co_ref, *, layer, j):
    d = x_ref.shape[-1]
    kw = cw_ref.shape[0]
    hist = kw - 1
    x = x_ref[...]
    hn = _rms(x, nm_ref[layer:layer + 1, :]).astype(BF16)
    gate = jnp.dot(hn, win_ref[:, :d], preferred_element_type=F32)
    rec = jnp.dot(hn, win_ref[:, d:], preferred_element_type=F32)

    conv = cb_ref[j:j + 1, :] + cw_ref[hist:kw, :] * rec
    for k in range(hist):
        conv = conv + cw_ref[k:k + 1, :] * cst_ref[k]
    for k in range(hist - 1):
        co_ref[k] = cst_ref[k + 1]
    co_ref[hist - 1] = rec

    a, u = _lru_gates(conv, wa_ref, ba_ref[j:j + 1, :], wx_ref, bx_ref[j:j + 1, :],
                      lam_ref[j:j + 1, :])
    h = a * h0_ref[...] + u
    ho_ref[...] = h
    x1 = x + _mm(_gelu(gate) * h, wout_ref[...])
    xo_ref[...] = x1
    q_ref[...] = _mm(_rms(x1, nxa_ref[layer:layer + 1, :]), wq_ref[...])


def _s_ffn_kernel(x_ref, o_ref, wo_ref, nm_ref, wup_ref, cw_ref, cb_ref, wdn_ref, st_ref, nf_ref,
                  xo_ref, so_ref, *, layer, final):
    dff = wdn_ref.shape[0]
    kw = cw_ref.shape[0]
    hist = kw - 1
    x = x_ref[...] + _mm(o_ref[...], wo_ref[...])
    hn = _rms(x, nm_ref[layer:layer + 1, :]).astype(BF16)
    g = jnp.dot(hn, wup_ref[:, :dff], preferred_element_type=F32)
    u = jnp.dot(hn, wup_ref[:, dff:], preferred_element_type=F32)
    conv = cb_ref[layer:layer + 1, :] + cw_ref[hist:kw, :] * g
    for k in range(hist):
        conv = conv + cw_ref[k:k + 1, :] * st_ref[:, k, :]
    for k in range(hist - 1):
        so_ref[:, k, :] = st_ref[:, k + 1, :]
    so_ref[:, hist - 1, :] = g
    xn = x + _mm(_gelu(conv) * u, wdn_ref[...])
    if final:
        xn = _rms(xn, nf_ref[...])
    xo_ref[...] = xn


def _s_cmod_pre_kernel(x_ref, nm_ref, pw1_ref, b1_ref, g_ref, *, layer, j):
    d = x_ref.shape[-1]
    hn = _rms(x_ref[...], nm_ref[layer:layer + 1, :])
    g_ref[...] = _glu(_mm(hn, pw1_ref[...]) + b1_ref[j:j + 1, :], d)


def _s_cmod_conv_kernel(st_ref, g_ref, dw_ref, dwb_ref, lng_ref, lnb_ref, c_ref, so_ref, *, j):
    hist = st_ref.shape[0]
    kw = hist + 1
    g = g_ref[...]
    c = dwb_ref[j:j + 1, :] + dw_ref[hist:kw, :] * g
    for k in range(hist):
        c = c + dw_ref[k:k + 1, :] * st_ref[k]
    for k in range(hist - 1):
        so_ref[k] = st_ref[k + 1]
    so_ref[hist - 1] = g
    c_ref[...] = jax.nn.silu(_layer_norm(c, lng_ref[j:j + 1, :], lnb_ref[j:j + 1, :]))


def _s_cmod_post_kernel(x_ref, c_ref, pw2_ref, nxa_ref, wq_ref, xo_ref, q_ref, *, layer):
    x1 = x_ref[...] + _mm(c_ref[...], pw2_ref[...])
    xo_ref[...] = x1
    q_ref[...] = _mm(_rms(x1, nxa_ref[layer:layer + 1, :]), wq_ref[...])


def _params(n_grid):
    return pltpu.CompilerParams(dimension_semantics=("arbitrary",) * n_grid,
                                vmem_limit_bytes=VMEM_LIMIT)


def _resident(shape, lead=None):
    if lead is None:
        nd = len(shape)
        return pl.BlockSpec(shape, lambda *_: (0,) * nd, pipeline_mode=pl.Buffered(1))
    nd = len(shape) - 1
    return pl.BlockSpec((None,) + tuple(shape[1:]), lambda *_: (lead,) + (0,) * nd,
                        pipeline_mode=pl.Buffered(1))


_whole = _resident


def _whole_out(shape):
    nd = len(shape)
    return pl.BlockSpec(shape, lambda *_: (0,) * nd)


def _sds(shape, dtype=F32):
    return jax.ShapeDtypeStruct(shape, dtype)


def _flat_cast_blocks(to_cast, steps, step_index):
    flat = [a.reshape(-1, a.shape[-1]) for a in to_cast]
    for a in flat:
        assert a.shape[0] % (steps * BF16_SUBLANES) == 0
    specs = [pl.BlockSpec((a.shape[0] // steps, a.shape[1]), lambda *g: (step_index(*g), 0))
             for a in flat]
    return flat, specs


def _prompt_memkv(mem, mem_norm3, wkv, heads, to_cast):
    depth = wkv.shape[0]
    b, m, d = mem.shape
    dh = d // heads
    flat, cast_specs = _flat_cast_blocks(to_cast, depth * b, lambda i, n: i * b + n)
    outs = pl.pallas_call(
        functools.partial(_p_memkv_kernel, n_cast=len(flat)),
        grid=(depth, b),
        in_specs=[pl.BlockSpec((None, m, d), lambda i, n: (n, 0, 0)),
                  pl.BlockSpec((None, 1, d), lambda i, n: (i, 0, 0)),
                  pl.BlockSpec((None, d, 2 * d), lambda i, n: (i, 0, 0))] + cast_specs,
        out_specs=[pl.BlockSpec((None, None, m, heads, dh), lambda i, n: (i, n, 0, 0, 0)),
                   pl.BlockSpec((None, None, m, heads, dh), lambda i, n: (i, n, 0, 0, 0)),
                   pl.BlockSpec((None, None, d, m), lambda i, n: (i, n, 0, 0)),
                   pl.BlockSpec((None, None, m, d), lambda i, n: (i, n, 0, 0))] + cast_specs,
        out_shape=[_sds((depth, b, m, heads, dh)), _sds((depth, b, m, heads, dh)),
                   _sds((depth, b, d, m), BF16), _sds((depth, b, m, d), BF16)]
                  + [_sds(a.shape, BF16) for a in flat],
        scratch_shapes=[pltpu.VMEM((d, 2 * d), BF16)],
        compiler_params=_params(2),
        name="prompt_memkv",
    )(mem, mem_norm3, wkv, *flat)
    return outs[:4], [o.reshape(a.shape) for o, a in zip(outs[4:], to_cast)]


def _x_spec(tt, d):
    return pl.BlockSpec((None, tt, d), lambda n, t: (n, t, 0))


def _state_spec(rows, cols):
    return pl.BlockSpec((None, rows, cols), lambda n, t: (n, 0, 0))


def _xt_spec(b, tr, d):
    return pl.BlockSpec((b, tr, d), lambda t: (0, t, 0))


def _steps_to_sequences(s, b):
    hrows, d = s.shape
    return s.reshape(hrows // b, b, d).transpose(1, 0, 2)


def _prompt_lru(x, layer, j, w, to_cast):
    b, t, d = x.shape
    tr = LRU_TIME_ROWS
    hist = w['lru_conv_w'].shape[1] - 1
    assert b == SUBLANES and tr % SUBLANES == 0 and tr >= hist and d % LANES == 0
    nl, hrows = d // LANES, hist * b
    steps = t // tr
    flat, cast_specs = _flat_cast_blocks(to_cast, steps, lambda s: s)
    ins = [(x, _xt_spec(b, tr, d)),
           (w['norm_mix'], _resident(w['norm_mix'].shape)),
           (w['lru_w_in'], _resident(w['lru_w_in'].shape, j)),
           (w['lru_conv_w'], _resident(w['lru_conv_w'].shape, j)),
           (w['lru_conv_b'], _resident(w['lru_conv_b'].shape)),
           (w['lru_wa'], _resident(w['lru_wa'].shape, j)),
           (w['lru_ba'], _resident(w['lru_ba'].shape)),
           (w['lru_wx'], _resident(w['lru_wx'].shape, j)),
           (w['lru_bx'], _resident(w['lru_bx'].shape)),
           (w['lru_lambda'], _resident(w['lru_lambda'].shape)),
           (w['lru_w_out'], _resident(w['lru_w_out'].shape, j))]
    ins += list(zip(flat, cast_specs))
    outs = pl.pallas_call(
        functools.partial(_p_lru_kernel, layer=layer, j=j, n_cast=len(flat)),
        grid=(steps,),
        in_specs=[s for _, s in ins],
        out_specs=[_xt_spec(b, tr, d), _whole_out((b, d)), _whole_out((hrows, d))] + cast_specs,
        out_shape=([_sds((b, t, d)), _sds((b, d)), _sds((hrows, d))]
                   + [_sds(a.shape, BF16) for a in flat]),
        scratch_shapes=[pltpu.VMEM((nl, hrows + b * tr, LANES), F32),
                        pltpu.VMEM((nl, b * tr, LANES), F32),
                        pltpu.VMEM((b, d), F32)],
        compiler_params=_params(1),
        name="prompt_lru",
    )(*[a for a, _ in ins])
    casted = [o.reshape(a.shape) for o, a in zip(outs[3:], to_cast)]
    return outs[0], outs[1], outs[2], casted


def _prompt_cmod(x, layer, j, w):
    b, t, d = x.shape
    tr = CMOD_TIME_ROWS
    hist = w['cm_dw_w'].shape[1] - 1
    assert b == SUBLANES and tr % SUBLANES == 0 and tr >= hist and d % CMOD_COL_TILE == 0
    nl, hrows = d // LANES, hist * b
    ins = [(x, _xt_spec(b, tr, d)),
           (w['norm_mix'], _resident(w['norm_mix'].shape)),
           (w['cm_w_pw1'], _resident(w['cm_w_pw1'].shape, j)),
           (w['cm_b_pw1'], _resident(w['cm_b_pw1'].shape)),
           (w['cm_dw_w'], _resident(w['cm_dw_w'].shape, j)),
           (w['cm_dw_b'], _resident(w['cm_dw_b'].shape)),
           (w['cm_ln_g'], _resident(w['cm_ln_g'].shape)),
           (w['cm_ln_b'], _resident(w['cm_ln_b'].shape)),
           (w['cm_w_pw2'], _resident(w['cm_w_pw2'].shape, j))]
    return pl.pallas_call(
        functools.partial(_p_cmod_kernel, layer=layer, j=j),
        grid=(t // tr,),
        in_specs=[s for _, s in ins],
        out_specs=[_xt_spec(b, tr, d), _whole_out((hrows, d))],
        out_shape=[_sds((b, t, d)), _sds((hrows, d))],
        scratch_shapes=[pltpu.VMEM((nl, hrows + b * tr, LANES), F32),
                        pltpu.VMEM((nl, b * tr, LANES), F32)],
        compiler_params=_params(1),
        name="prompt_cmod",
    )(*[a for a, _ in ins])


def _prompt_xattn(x, layer, kt, vb, w):
    b, t, d = x.shape
    tt = XATTN_T_TILE
    m = vb.shape[2]
    ins = [(x, _x_spec(tt, d)),
           (w['norm_xa'], _resident(w['norm_xa'].shape)),
           (w['xa_w_q'], _resident(w['xa_w_q'].shape, layer)),
           (kt, pl.BlockSpec((None, None, d, m), lambda n, t_: (layer, n, 0, 0))),
           (vb, pl.BlockSpec((None, None, m, d), lambda n, t_: (layer, n, 0, 0))),
           (w['xa_w_o'], _resident(w['xa_w_o'].shape, layer))]
    return pl.pallas_call(
        functools.partial(_p_xattn_kernel, layer=layer),
        grid=(b, t // tt),
        in_specs=[s for _, s in ins],
        out_specs=_x_spec(tt, d),
        out_shape=_sds((b, t, d)),
        compiler_params=_params(2),
        name="prompt_xattn",
    )(*[a for a, _ in ins])


def _prompt_ffn_sample_attn(x, layer, final, q, cache_k, cache_v, w):
    b, t, d = x.shape
    tt = FFN_T_TILE
    n_t = t // tt
    dff = w['ffn_w_down'].shape[1]
    hist = w['ffn_conv_w'].shape[1] - 1
    r = q.shape[0]
    _, _, m, nh, dh = cache_k.shape
    assert r % (b * n_t) == 0 and dh % LANES == 0 and nh * (dh // LANES) == SUBLANES
    nb = r // (b * n_t)
    kv_spec = pl.BlockSpec((None, nb, m, SUBLANES, LANES),
                           lambda n, s: (layer, n * n_t + s, 0, 0, 0))
    row_spec = pl.BlockSpec((nb, SUBLANES, LANES), lambda n, s: (n * n_t + s, 0, 0))
    ins = [(x, _x_spec(tt, d)),
           (w['norm_ffn'], _resident(w['norm_ffn'].shape)),
           (w['ffn_w_up'], _resident(w['ffn_w_up'].shape, layer)),
           (w['ffn_conv_w'], _resident(w['ffn_conv_w'].shape, layer)),
           (w['ffn_conv_b'], _resident(w['ffn_conv_b'].shape)),
           (w['ffn_w_down'], _resident(w['ffn_w_down'].shape, layer)),
           (w['norm_final'], _resident(w['norm_final'].shape)),
           (_pack_heads(q.reshape(r, nh, dh)), row_spec),
           (_pack_heads(cache_k), kv_spec), (_pack_heads(cache_v), kv_spec)]
    xo, fbuf, o = pl.pallas_call(
        functools.partial(_p_ffn_kernel, layer=layer, final=final),
        grid=(b, n_t),
        in_specs=[s for _, s in ins],
        out_specs=[_x_spec(tt, d), _state_spec(hist, dff), row_spec],
        out_shape=[_sds((b, t, d)), _sds((b, hist, dff)), _sds((r, SUBLANES, LANES))],
        scratch_shapes=[pltpu.VMEM((SUBLANES + tt, dff), F32)],
        compiler_params=pltpu.CompilerParams(dimension_semantics=("arbitrary", "arbitrary"),
                                             vmem_limit_bytes=FFN_VMEM_LIMIT),
        name="prompt_ffn_sample_attn",
    )(*[a for a, _ in ins])
    o = o.reshape(r, dh // LANES, nh, LANES).swapaxes(-3, -2)
    return xo, fbuf, o.reshape(r, d)


def _pack_heads(a):
    *lead, nh, dh = a.shape
    a = a.reshape(*lead, nh, dh // LANES, LANES).swapaxes(-3, -2)
    return a.reshape(*lead, (dh // LANES) * nh, LANES)


def _sample_lru(x, h0, cst, layer, j, w):
    r, d = x.shape
    ins = [(x, _whole(x.shape)), (h0, _whole(h0.shape, j)), (cst, _whole(cst.shape, j)),
           (w['norm_mix'], _whole(w['norm_mix'].shape)),
           (w['lru_w_in'], _whole(w['lru_w_in'].shape, j)),
           (w['lru_conv_w'], _whole(w['lru_conv_w'].shape, j)),
           (w['lru_conv_b'], _whole(w['lru_conv_b'].shape)),
           (w['lru_wa'], _whole(w['lru_wa'].shape, j)),
           (w['lru_ba'], _whole(w['lru_ba'].shape)),
           (w['lru_wx'], _whole(w['lru_wx'].shape, j)),
           (w['lru_bx'], _whole(w['lru_bx'].shape)),
           (w['lru_lambda'], _whole(w['lru_lambda'].shape)),
           (w['lru_w_out'], _whole(w['lru_w_out'].shape, j)),
           (w['norm_xa'], _whole(w['norm_xa'].shape)),
           (w['xa_w_q'], _whole(w['xa_w_q'].shape, layer))]
    return pl.pallas_call(
        functools.partial(_s_lru_kernel, layer=layer, j=j),
        in_specs=[s for _, s in ins],
        out_specs=[_whole_out((r, d)), _whole_out((r, d)), _whole_out((r, d)),
                   _whole_out(cst.shape[1:])],
        out_shape=[_sds((r, d)), _sds((r, d)), _sds((r, d)), _sds(cst.shape[1:])],
        grid=(1,),
        compiler_params=_params(1),
        name="sample_lru",
    )(*[a for a, _ in ins])


def _sample_ffn(x, o, st, layer, final, w):
    r, d = x.shape
    ins = [(x, _whole(x.shape)), (o, _whole(o.shape)),
           (w['xa_w_o'], _whole(w['xa_w_o'].shape, layer)),
           (w['norm_ffn'], _whole(w['norm_ffn'].shape)),
           (w['ffn_w_up'], _whole(w['ffn_w_up'].shape, layer)),
           (w['ffn_conv_w'], _whole(w['ffn_conv_w'].shape, layer)),
           (w['ffn_conv_b'], _whole(w['ffn_conv_b'].shape)),
           (w['ffn_w_down'], _whole(w['ffn_w_down'].shape, layer)),
           (st, _whole(st.shape, layer)),
           (w['norm_final'], _whole(w['norm_final'].shape))]
    return pl.pallas_call(
        functools.partial(_s_ffn_kernel, layer=layer, final=final),
        in_specs=[s for _, s in ins],
        out_specs=[_whole_out((r, d)), _whole_out(st.shape[1:])],
        out_shape=[_sds((r, d)), _sds(st.shape[1:])],
        grid=(1,),
        compiler_params=_params(1),
        name="sample_ffn",
    )(*[a for a, _ in ins])


def _sample_cmod(x, st, layer, j, w):
    r, d = x.shape
    ins = [(x, _whole(x.shape)),
           (w['norm_mix'], _whole(w['norm_mix'].shape)),
           (w['cm_w_pw1'], _whole(w['cm_w_pw1'].shape, j)),
           (w['cm_b_pw1'], _whole(w['cm_b_pw1'].shape))]
    g = pl.pallas_call(
        functools.partial(_s_cmod_pre_kernel, layer=layer, j=j),
        in_specs=[s for _, s in ins],
        out_specs=_whole_out((r, d)),
        out_shape=_sds((r, d)),
        grid=(1,),
        compiler_params=_params(1),
        name="sample_cmod_pre",
    )(*[a for a, _ in ins])

    rows = SAMPLE_CONV_ROWS
    hist = st.shape[1]
    ins = [(st, pl.BlockSpec((None, hist, rows, d), lambda n: (j, 0, n, 0))),
           (g, pl.BlockSpec((rows, d), lambda n: (n, 0))),
           (w['cm_dw_w'], _resident(w['cm_dw_w'].shape, j)),
           (w['cm_dw_b'], _resident(w['cm_dw_b'].shape)),
           (w['cm_ln_g'], _resident(w['cm_ln_g'].shape)),
           (w['cm_ln_b'], _resident(w['cm_ln_b'].shape))]
    c, new_st = pl.pallas_call(
        functools.partial(_s_cmod_conv_kernel, j=j),
        grid=(r // rows,),
        in_specs=[s for _, s in ins],
        out_specs=[pl.BlockSpec((rows, d), lambda n: (n, 0)),
                   pl.BlockSpec((hist, rows, d), lambda n: (0, n, 0))],
        out_shape=[_sds((r, d)), _sds((hist, r, d))],
        compiler_params=_params(1),
        name="sample_cmod_conv",
    )(*[a for a, _ in ins])

    ins = [(x, _whole(x.shape)), (c, _whole(c.shape)),
           (w['cm_w_pw2'], _whole(w['cm_w_pw2'].shape, j)),
           (w['norm_xa'], _whole(w['norm_xa'].shape)),
           (w['xa_w_q'], _whole(w['xa_w_q'].shape, layer))]
    x1, q = pl.pallas_call(
        functools.partial(_s_cmod_post_kernel, layer=layer),
        in_specs=[s for _, s in ins],
        out_specs=[_whole_out((r, d)), _whole_out((r, d))],
        out_shape=[_sds((r, d)), _sds((r, d))],
        grid=(1,),
        compiler_params=_params(1),
        name="sample_cmod_post",
    )(*[a for a, _ in ins])
    return x1, q, new_st


_MIXER0_MATMUL_WEIGHTS = ('lru_w_in', 'lru_wa', 'lru_wx', 'lru_w_out')
_LATE_MATMUL_WEIGHTS = ('cm_w_pw1', 'cm_w_pw2', 'xa_w_q', 'xa_w_o', 'ffn_w_up', 'ffn_w_down')


def kernel(x_prompt, x_sample, state_lru_h, state_lru_conv, state_cmod_conv, state_ffn_conv, cache_mem_k, cache_mem_v, mem_prompt, norm_mix, norm_xa, norm_ffn, norm_final, lru_w_in, lru_conv_w, lru_conv_b, lru_wa, lru_ba, lru_wx, lru_bx, lru_lambda, lru_w_out, cm_w_pw1, cm_b_pw1, cm_dw_w, cm_dw_b, cm_ln_g, cm_ln_b, cm_w_pw2, mem_norm, xa_w_q, xa_w_kv, xa_w_o, ffn_w_up, ffn_conv_w, ffn_conv_b, ffn_w_down):
    w = dict(norm_mix=norm_mix, norm_xa=norm_xa, norm_ffn=norm_ffn,
             norm_final=norm_final.reshape(1, -1),
             lru_w_in=lru_w_in, lru_conv_w=lru_conv_w, lru_conv_b=lru_conv_b, lru_wa=lru_wa,
             lru_ba=lru_ba, lru_wx=lru_wx, lru_bx=lru_bx, lru_lambda=lru_lambda,
             lru_w_out=lru_w_out, cm_w_pw1=cm_w_pw1, cm_b_pw1=cm_b_pw1, cm_dw_w=cm_dw_w,
             cm_dw_b=cm_dw_b, cm_ln_g=cm_ln_g, cm_ln_b=cm_ln_b, cm_w_pw2=cm_w_pw2,
             xa_w_q=xa_w_q, xa_w_kv=xa_w_kv, xa_w_o=xa_w_o,
             ffn_w_up=ffn_w_up, ffn_conv_w=ffn_conv_w, ffn_conv_b=ffn_conv_b,
             ffn_w_down=ffn_w_down)

    depth = norm_mix.shape[0]
    bsz, _, d = x_prompt.shape
    n_mem = mem_prompt.shape[1]
    heads = cache_mem_k.shape[3]

    (k_p, v_p, kt_p, vb_p), casted = _prompt_memkv(
        mem_prompt, mem_norm.reshape(depth, 1, d), w['xa_w_kv'], heads,
        [w[n] for n in _MIXER0_MATMUL_WEIGHTS])
    w.update(zip(_MIXER0_MATMUL_WEIGHTS, casted))
    x = x_prompt
    r = x_sample.shape[0]
    xs = x_sample.reshape(r, d)
    lconv = state_lru_conv.transpose(0, 2, 1, 3)
    cconv = state_cmod_conv.transpose(0, 2, 1, 3)
    p_h, p_lconv, p_cconv, p_fconv = [], [], [], []
    s_h, s_lconv, s_cconv, s_fconv = [], [], [], []
    for i in range(depth):
        j = i // 2
        if i % 2 == 0:
            late = [n for n in _LATE_MATMUL_WEIGHTS if w[n].dtype != BF16]
            x, h_last, cbuf, casted = _prompt_lru(x, i, j, w, [w[n] for n in late])
            w.update(zip(late, casted))
            p_h.append(h_last)
            p_lconv.append(_steps_to_sequences(cbuf, bsz))
            xs, q, h_new, c_new = _sample_lru(xs, state_lru_h, lconv, i, j, w)
            s_h.append(h_new)
            s_lconv.append(c_new.transpose(1, 0, 2))
        else:
            x, cbuf = _prompt_cmod(x, i, j, w)
            p_cconv.append(_steps_to_sequences(cbuf, bsz))
            xs, q, c_new = _sample_cmod(xs, cconv, i, j, w)
            s_cconv.append(c_new.transpose(1, 0, 2))
        x = _prompt_xattn(x, i, kt_p, vb_p, w)
        x, fbuf, o = _prompt_ffn_sample_attn(x, i, i == depth - 1, q, cache_mem_k, cache_mem_v, w)
        p_fconv.append(fbuf)
        xs, f_new = _sample_ffn(xs, o, state_ffn_conv, i, i == depth - 1, w)
        s_fconv.append(f_new)
    y_prompt = x
    y_sample = xs.reshape(x_sample.shape)

    return (y_prompt, y_sample,
            jnp.stack(p_h), jnp.stack(p_lconv), jnp.stack(p_cconv), jnp.stack(p_fconv),
            k_p, v_p,
            jnp.stack(s_h), jnp.stack(s_lconv), jnp.stack(s_cconv), jnp.stack(s_fconv))
```

```python
import functools

import jax
import jax.numpy as jnp
from jax import lax
from jax.experimental import pallas as pl
from jax.experimental.pallas import tpu as pltpu

F32 = jnp.float32
BF16 = jnp.bfloat16

EPS = 1e-6
RG_C = 8.0
N_RG_BLOCKS = 4
XA_HEADS = 4
SUBLANES = 8
LANES = 128
BF16_SUBLANES = 16
VMEM_LIMIT = 56 << 20
FFN_VMEM_LIMIT = 60 << 20

LRU_TIME_ROWS = 64
CMOD_TIME_ROWS = 64
XATTN_T_TILE = 1024
FFN_T_TILE = 512
FFN_COL_TILE = 1536
CMOD_COL_TILE = 256
CMOD_ROW_TILE = 256
SAMPLE_FFN_BLOCKS = 4
SAMPLE_CONV_ROWS = 32


def _rms(x, g):
    return x * lax.rsqrt(jnp.mean(x * x, axis=-1, keepdims=True) + EPS) * g


def _layer_norm(x, g, b):
    mu = jnp.mean(x, axis=-1, keepdims=True)
    xc = x - mu
    var = jnp.mean(xc * xc, axis=-1, keepdims=True)
    return xc * lax.rsqrt(var + EPS) * g + b


def _mm(a, w):
    return jnp.dot(a.astype(BF16), w, preferred_element_type=F32)


_GELU_C = 0.7978845608028654


def _gelu(x):
    inner = x * (_GELU_C + (_GELU_C * 0.044715) * (x * x))
    return x * (0.5 + 0.5 * jnp.tanh(inner))


def _softplus(z):
    return jnp.maximum(z, 0.0) + jnp.log1p(jnp.exp(-jnp.abs(z)))


def _block_diag(xb, w_ref):
    blk = xb.shape[-1] // N_RG_BLOCKS
    return jnp.concatenate(
        [jnp.dot(xb[:, n * blk:(n + 1) * blk], w_ref[n], preferred_element_type=F32)
         for n in range(N_RG_BLOCKS)], axis=-1)


def _lru_decay_input(rec, ga, gx, lam):
    log_a = -RG_C * jax.nn.sigmoid(ga) * _softplus(-lam)
    a = jnp.exp(log_a)
    u = jnp.sqrt(1.0 - a * a) * jax.nn.sigmoid(gx) * rec
    return a, u


def _lru_gates(rec, wa_ref, ba, wx_ref, bx, lam):
    rb = rec.astype(BF16)
    return _lru_decay_input(rec, _block_diag(rb, wa_ref) + ba, _block_diag(rb, wx_ref) + bx, lam)


def _glu(ab, d):
    return ab[:, :d] * jax.nn.sigmoid(ab[:, d:])


def _cast_blocks(srcs, dsts):
    for src, dst in zip(srcs, dsts):
        dst[...] = src[...].astype(BF16)


def _p_memkv_kernel(m_ref, g_ref, wkv_ref, *rest, n_cast):
    cast_in = rest[:n_cast]
    k_ref, v_ref, kt_ref, vb_ref = rest[n_cast:n_cast + 4]
    cast_out = rest[n_cast + 4:2 * n_cast + 4]
    wkv_s, = rest[2 * n_cast + 4:]
    _cast_blocks(cast_in, cast_out)

    @pl.when(pl.program_id(1) == 0)
    def _():
        wkv_s[...] = wkv_ref[...].astype(BF16)

    d = m_ref.shape[-1]
    hn = _rms(m_ref[...], g_ref[...]).astype(BF16)
    k = jnp.dot(hn, wkv_s[:, :d], preferred_element_type=F32)
    v = jnp.dot(hn, wkv_s[:, d:], preferred_element_type=F32)
    dh = d // k_ref.shape[1]
    for h in range(k_ref.shape[1]):
        k_ref[:, h, :] = k[:, h * dh:(h + 1) * dh]
        v_ref[:, h, :] = v[:, h * dh:(h + 1) * dh]
    kt_ref[...] = k.T.astype(BF16)
    vb_ref[...] = v.astype(BF16)


def _p_lru_kernel(x_ref, nm_ref, win_ref, cw_ref, cb_ref, wa_ref, ba_ref, wx_ref, bx_ref,
                  lam_ref, wout_ref, *rest, layer, j, n_cast):
    cast_in = rest[:n_cast]
    xo_ref, ho_ref, co_ref = rest[n_cast:n_cast + 3]
    cast_out = rest[n_cast + 3:2 * n_cast + 3]
    rec_t, y_t, h_s = rest[2 * n_cast + 3:]
    _cast_blocks(cast_in, cast_out)

    nb, tr, d = x_ref.shape
    rows = nb * tr
    kw = cw_ref.shape[0]
    hist = kw - 1
    hrows = hist * nb
    nl = d // LANES
    tc = pl.program_id(0)

    @pl.when(tc == 0)
    def _():
        rec_t[:, 0:hrows, :] = jnp.zeros((nl, hrows, LANES), F32)
        h_s[...] = jnp.zeros_like(h_s)

    x = x_ref[...].reshape(rows, d)
    hn = _rms(x, nm_ref[layer:layer + 1, :]).astype(BF16)
    cblk = d // N_RG_BLOCKS
    lanes_per_blk = cblk // LANES

    def rec_proj(n):
        lo = d + n * cblk
        return jnp.dot(hn, win_ref[:, lo:lo + cblk], preferred_element_type=F32)

    out = None
    nxt = rec_proj(0)
    for n in range(N_RG_BLOCKS):
        rec = nxt
        gate = jnp.dot(hn, win_ref[:, n * cblk:(n + 1) * cblk], preferred_element_type=F32)
        if n + 1 < N_RG_BLOCKS:
            nxt = rec_proj(n + 1)
        cs = slice(n * cblk, (n + 1) * cblk)
        slabs = []
        for lt in range(lanes_per_blk):
            l = n * lanes_per_blk + lt
            ls = slice(l * LANES, (l + 1) * LANES)
            for b in range(nb):
                rec_t[l, pl.ds(hrows + b, tr, stride=nb), :] = (
                    rec[b * tr:(b + 1) * tr, lt * LANES:(lt + 1) * LANES])
            acc = cb_ref[j:j + 1, ls] + cw_ref[hist:kw, ls] * rec_t[l, hrows:hrows + rows, :]
            for k in range(hist):
                acc = acc + cw_ref[k:k + 1, ls] * rec_t[l, k * nb:k * nb + rows, :]
            slabs.append(acc)
        conv = jnp.concatenate(slabs, axis=-1)
        cb16 = conv.astype(BF16)
        ga = jnp.dot(cb16, wa_ref[n], preferred_element_type=F32) + ba_ref[j:j + 1, cs]
        gx = jnp.dot(cb16, wx_ref[n], preferred_element_type=F32) + bx_ref[j:j + 1, cs]
        a, u = _lru_decay_input(conv, ga, gx, lam_ref[j:j + 1, cs])

        h = h_s[:, cs]
        for t in range(tr):
            h = a[t * nb:(t + 1) * nb, :] * h + u[t * nb:(t + 1) * nb, :]
            for lt in range(lanes_per_blk):
                y_t[n * lanes_per_blk + lt, t * nb:(t + 1) * nb, :] = (
                    h[:, lt * LANES:(lt + 1) * LANES])
        h_s[:, cs] = h
        ho_ref[:, cs] = h

        y = jnp.concatenate(
            [jnp.concatenate([y_t[n * lanes_per_blk + lt, pl.ds(b, tr, stride=nb), :]
                              for lt in range(lanes_per_blk)], axis=-1)
             for b in range(nb)], axis=0)
        part = jnp.dot((_gelu(gate) * y).astype(BF16), wout_ref[cs, :],
                       preferred_element_type=F32)
        out = part if out is None else out + part
    tail = rec_t[:, rows:rows + hrows, :]
    co_ref[...] = jnp.concatenate([tail[l] for l in range(nl)], axis=-1)
    rec_t[:, 0:hrows, :] = tail
    xo_ref[...] = (x + out).reshape(nb, tr, d)


def _p_cmod_kernel(x_ref, nm_ref, pw1_ref, b1_ref, dw_ref, dwb_ref, lng_ref, lnb_ref, pw2_ref,
                   xo_ref, so_ref, g_t, c_t, *, layer, j):
    nb, tr, d = x_ref.shape
    rows = nb * tr
    kw = dw_ref.shape[0]
    hist = kw - 1
    hrows = hist * nb
    lanes_per_col = CMOD_COL_TILE // LANES
    tc = pl.program_id(0)

    @pl.when(tc == 0)
    def _():
        g_t[:, 0:hrows, :] = jnp.zeros((d // LANES, hrows, LANES), F32)

    x = x_ref[...].reshape(rows, d)
    hn = _rms(x, nm_ref[layer:layer + 1, :]).astype(BF16)
    def pw1(c):
        lo = c * CMOD_COL_TILE
        cs = slice(lo, lo + CMOD_COL_TILE)
        gs = slice(d + lo, d + lo + CMOD_COL_TILE)
        return (jnp.dot(hn, pw1_ref[:, cs], preferred_element_type=F32) + b1_ref[j:j + 1, cs],
                jnp.dot(hn, pw1_ref[:, gs], preferred_element_type=F32) + b1_ref[j:j + 1, gs])

    n_col = d // CMOD_COL_TILE
    nxt = pw1(0)
    for c in range(n_col):
        a, bg = nxt
        if c + 1 < n_col:
            nxt = pw1(c + 1)
        g = a * jax.nn.sigmoid(bg)
        for lt in range(lanes_per_col):
            l = c * lanes_per_col + lt
            ls = slice(l * LANES, (l + 1) * LANES)
            for b in range(nb):
                g_t[l, pl.ds(hrows + b, tr, stride=nb), :] = (
                    g[b * tr:(b + 1) * tr, lt * LANES:(lt + 1) * LANES])
            for rb in range(rows // CMOD_ROW_TILE):
                r0 = hrows + rb * CMOD_ROW_TILE
                acc = dwb_ref[j:j + 1, ls] + dw_ref[hist:kw, ls] * g_t[l, r0:r0 + CMOD_ROW_TILE, :]
                for s in range(1, kw):
                    acc = acc + (dw_ref[hist - s:kw - s, ls]
                                 * g_t[l, r0 - s * nb:r0 - s * nb + CMOD_ROW_TILE, :])
                c_t[l, rb * CMOD_ROW_TILE:(rb + 1) * CMOD_ROW_TILE, :] = acc
    tail = g_t[:, rows:rows + hrows, :]
    so_ref[...] = jnp.concatenate([tail[l] for l in range(d // LANES)], axis=-1)
    g_t[:, 0:hrows, :] = tail

    conv = jnp.concatenate(
        [jnp.concatenate([c_t[l, pl.ds(b, tr, stride=nb), :] for l in range(d // LANES)], axis=-1)
         for b in range(nb)], axis=0)
    c = jax.nn.silu(_layer_norm(conv, lng_ref[j:j + 1, :], lnb_ref[j:j + 1, :]))
    xo_ref[...] = (x + _mm(c, pw2_ref[...])).reshape(nb, tr, d)


def _p_xattn_kernel(x_ref, nm_ref, wq_ref, kt_ref, v_ref, wo_ref, xo_ref, *, layer):
    tt, d = x_ref.shape
    dh = d // XA_HEADS
    x = x_ref[...]
    q = _mm(_rms(x, nm_ref[layer:layer + 1, :]), wq_ref[...]).astype(BF16)
    cols = [slice(h * dh, (h + 1) * dh) for h in range(XA_HEADS)]
    scores = [jnp.dot(q[:, cs], kt_ref[cs, :], preferred_element_type=F32) * (dh ** -0.5)
              for cs in cols]
    heads = []
    for s, cs in zip(scores, cols):
        e = jnp.exp(s - jnp.max(s, axis=-1, keepdims=True))
        p = e / jnp.sum(e, axis=-1, keepdims=True)
        heads.append(jnp.dot(p.astype(BF16), v_ref[:, cs], preferred_element_type=F32))
    o = jnp.concatenate(heads, axis=-1)
    xo_ref[...] = x + _mm(o, wo_ref[...])


def _sample_attend(q_ref, k_ref, v_ref, o_ref, seqs, heads):
    rows, lanes = q_ref.shape[-2:]
    dh = lanes * (rows // heads)
    for b in seqs:
        q = q_ref[b] * (dh ** -0.5)
        prod = k_ref[b] * q[None]
        shift = heads
        while shift < rows:
            prod = prod + pltpu.roll(prod, shift, axis=1)
            shift *= 2
        s = jnp.sum(prod, axis=-1, keepdims=True)
        e = jnp.exp(s - jnp.max(s, axis=0, keepdims=True))
        o_ref[b] = jnp.sum(e * v_ref[b], axis=0) / jnp.sum(e, axis=0)


def _p_ffn_kernel(x_ref, nm_ref, wup_ref, cw_ref, cb_ref, wdn_ref, nf_ref, q_ref, k_ref, v_ref,
                  xo_ref, so_ref, ao_ref, buf, *, layer, final):
    tt, d = x_ref.shape
    dff = wdn_ref.shape[0]
    kw = cw_ref.shape[0]
    hist = kw - 1
    tc = pl.program_id(1)

    @pl.when(tc == 0)
    def _():
        buf[0:SUBLANES, :] = jnp.zeros((SUBLANES, dff), F32)

    x = x_ref[...]
    hn = _rms(x, nm_ref[layer:layer + 1, :]).astype(BF16)
    def up(c):
        lo = c * FFN_COL_TILE
        return (jnp.dot(hn, wup_ref[:, lo:lo + FFN_COL_TILE], preferred_element_type=F32),
                jnp.dot(hn, wup_ref[:, dff + lo:dff + lo + FFN_COL_TILE],
                        preferred_element_type=F32))

    n_col = dff // FFN_COL_TILE
    n_seq = q_ref.shape[0]
    acc = None
    nxt = up(0)
    for c in range(n_col):
        cs = slice(c * FFN_COL_TILE, (c + 1) * FFN_COL_TILE)
        g, u = nxt
        if c + 1 < n_col:
            nxt = up(c + 1)
        _sample_attend(q_ref, k_ref, v_ref, ao_ref,
                       range(c * n_seq // n_col, (c + 1) * n_seq // n_col), XA_HEADS)
        buf[SUBLANES:SUBLANES + tt, cs] = g
        conv = cb_ref[layer:layer + 1, cs] + cw_ref[hist:kw, cs] * g
        for k in range(hist):
            off = SUBLANES - hist + k
            conv = conv + cw_ref[k:k + 1, cs] * buf[off:off + tt, cs]
        act = (_gelu(conv) * u).astype(BF16)
        part = jnp.dot(act, wdn_ref[cs, :], preferred_element_type=F32)
        acc = part if acc is None else acc + part
    tail = buf[SUBLANES + tt - hist:SUBLANES + tt, :]
    so_ref[...] = tail
    buf[SUBLANES - hist:SUBLANES, :] = tail

    xn = x + acc
    if final:
        xn = _rms(xn, nf_ref[...])
    xo_ref[...] = xn


def _s_lru_kernel(x_ref, h0_ref, cst_ref, nm_ref, win_ref, cw_ref, cb_ref, wa_ref, ba_ref,
                  wx_ref, bx_ref, lam_ref, wout_ref, nxa_ref, wq_ref,
                  xo_ref, q_ref, ho_ref, co_ref, *, layer, j):
    d = x_ref.shape[-1]
    kw = cw_ref.shape[0]
    hist = kw - 1
    x = x_ref[...]
    hn = _rms(x, nm_ref[layer:layer + 1, :]).astype(BF16)
    gate = jnp.dot(hn, win_ref[:, :d], preferred_element_type=F32)
    rec = jnp.dot(hn, win_ref[:, d:], preferred_element_type=F32)

    conv = cb_ref[j:j + 1, :] + cw_ref[hist:kw, :] * rec
    for k in range(hist):
        conv = conv + cw_ref[k:k + 1, :] * cst_ref[k]
    for k in range(hist - 1):
        co_ref[k] = cst_ref[k + 1]
    co_ref[hist - 1] = rec

    a, u = _lru_gates(conv, wa_ref, ba_ref[j:j + 1, :], wx_ref, bx_ref[j:j + 1, :],
                      lam_ref[j:j + 1, :])
    h = a * h0_ref[...] + u
    ho_ref[...] = h
    x1 = x + _mm(_gelu(gate) * h, wout_ref[...])
    xo_ref[...] = x1
    q_ref[...] = _mm(_rms(x1, nxa_ref[layer:layer + 1, :]), wq_ref[...])


def _s_ffn_kernel(x_ref, o_ref, wo_ref, nm_ref, wg_ref, wu_ref, cw_ref, cb_ref, wdn_ref, st_ref,
                  nf_ref, xo_ref, so_ref, x_s, hn_s, acc_s, *, layer, final):
    c = pl.program_id(0)
    kw = cw_ref.shape[0]
    hist = kw - 1

    @pl.when(c == 0)
    def _():
        x = x_ref[...] + _mm(o_ref[...], wo_ref[...])
        x_s[...] = x
        hn_s[...] = _rms(x, nm_ref[layer:layer + 1, :]).astype(BF16)
        acc_s[...] = jnp.zeros_like(acc_s)

    hn = hn_s[...]
    g = jnp.dot(hn, wg_ref[...], preferred_element_type=F32)
    u = jnp.dot(hn, wu_ref[...], preferred_element_type=F32)
    conv = cb_ref[layer:layer + 1, :] + cw_ref[hist:kw, :] * g
    for k in range(hist):
        conv = conv + cw_ref[k:k + 1, :] * st_ref[:, k, :]
    for k in range(hist - 1):
        so_ref[:, k, :] = st_ref[:, k + 1, :]
    so_ref[:, hist - 1, :] = g
    acc_s[...] += _mm(_gelu(conv) * u, wdn_ref[...])

    @pl.when(c == pl.num_programs(0) - 1)
    def _():
        xn = x_s[...] + acc_s[...]
        if final:
            xn = _rms(xn, nf_ref[...])
        xo_ref[...] = xn


def _s_cmod_pre_kernel(x_ref, nm_ref, pw1_ref, b1_ref, g_ref, *, layer, j):
    d = x_ref.shape[-1]
    hn = _rms(x_ref[...], nm_ref[layer:layer + 1, :])
    g_ref[...] = _glu(_mm(hn, pw1_ref[...]) + b1_ref[j:j + 1, :], d)


def _s_cmod_conv_kernel(st_ref, g_ref, dw_ref, dwb_ref, lng_ref, lnb_ref, c_ref, so_ref, *, j):
    hist = st_ref.shape[0]
    kw = hist + 1
    g = g_ref[...]
    c = dwb_ref[j:j + 1, :] + dw_ref[hist:kw, :] * g
    for k in range(hist):
        c = c + dw_ref[k:k + 1, :] * st_ref[k]
    for k in range(hist - 1):
        so_ref[k] = st_ref[k + 1]
    so_ref[hist - 1] = g
    c_ref[...] = jax.nn.silu(_layer_norm(c, lng_ref[j:j + 1, :], lnb_ref[j:j + 1, :]))


def _s_cmod_post_kernel(x_ref, c_ref, pw2_ref, nxa_ref, wq_ref, xo_ref, q_ref, *, layer):
    x1 = x_ref[...] + _mm(c_ref[...], pw2_ref[...])
    xo_ref[...] = x1
    q_ref[...] = _mm(_rms(x1, nxa_ref[layer:layer + 1, :]), wq_ref[...])


def _params(n_grid):
    return pltpu.CompilerParams(dimension_semantics=("arbitrary",) * n_grid,
                                vmem_limit_bytes=VMEM_LIMIT)


def _resident(shape, lead=None):
    if lead is None:
        nd = len(shape)
        return pl.BlockSpec(shape, lambda *_: (0,) * nd, pipeline_mode=pl.Buffered(1))
    nd = len(shape) - 1
    return pl.BlockSpec((None,) + tuple(shape[1:]), lambda *_: (lead,) + (0,) * nd,
                        pipeline_mode=pl.Buffered(1))


_whole = _resident


def _whole_out(shape):
    nd = len(shape)
    return pl.BlockSpec(shape, lambda *_: (0,) * nd)


def _sds(shape, dtype=F32):
    return jax.ShapeDtypeStruct(shape, dtype)


def _flat_cast_blocks(to_cast, steps, step_index):
    flat = [a.reshape(-1, a.shape[-1]) for a in to_cast]
    for a in flat:
        assert a.shape[0] % (steps * BF16_SUBLANES) == 0
    specs = [pl.BlockSpec((a.shape[0] // steps, a.shape[1]), lambda *g: (step_index(*g), 0))
             for a in flat]
    return flat, specs


def _prompt_memkv(mem, mem_norm3, wkv, heads, to_cast):
    depth = wkv.shape[0]
    b, m, d = mem.shape
    dh = d // heads
    flat, cast_specs = _flat_cast_blocks(to_cast, depth * b, lambda i, n: i * b + n)
    outs = pl.pallas_call(
        functools.partial(_p_memkv_kernel, n_cast=len(flat)),
        grid=(depth, b),
        in_specs=[pl.BlockSpec((None, m, d), lambda i, n: (n, 0, 0)),
                  pl.BlockSpec((None, 1, d), lambda i, n: (i, 0, 0)),
                  pl.BlockSpec((None, d, 2 * d), lambda i, n: (i, 0, 0))] + cast_specs,
        out_specs=[pl.BlockSpec((None, None, m, heads, dh), lambda i, n: (i, n, 0, 0, 0)),
                   pl.BlockSpec((None, None, m, heads, dh), lambda i, n: (i, n, 0, 0, 0)),
                   pl.BlockSpec((None, None, d, m), lambda i, n: (i, n, 0, 0)),
                   pl.BlockSpec((None, None, m, d), lambda i, n: (i, n, 0, 0))] + cast_specs,
        out_shape=[_sds((depth, b, m, heads, dh)), _sds((depth, b, m, heads, dh)),
                   _sds((depth, b, d, m), BF16), _sds((depth, b, m, d), BF16)]
                  + [_sds(a.shape, BF16) for a in flat],
        scratch_shapes=[pltpu.VMEM((d, 2 * d), BF16)],
        compiler_params=_params(2),
        name="prompt_memkv",
    )(mem, mem_norm3, wkv, *flat)
    return outs[:4], [o.reshape(a.shape) for o, a in zip(outs[4:], to_cast)]


def _x_spec(tt, d):
    return pl.BlockSpec((None, tt, d), lambda n, t: (n, t, 0))


def _state_spec(rows, cols):
    return pl.BlockSpec((None, rows, cols), lambda n, t: (n, 0, 0))


def _xt_spec(b, tr, d):
    return pl.BlockSpec((b, tr, d), lambda t: (0, t, 0))


def _steps_to_sequences(s, b):
    hrows, d = s.shape
    return s.reshape(hrows // b, b, d).transpose(1, 0, 2)


def _prompt_lru(x, layer, j, w, to_cast):
    b, t, d = x.shape
    tr = LRU_TIME_ROWS
    hist = w['lru_conv_w'].shape[1] - 1
    assert b == SUBLANES and tr % SUBLANES == 0 and tr >= hist and d % LANES == 0
    nl, hrows = d // LANES, hist * b
    steps = t // tr
    flat, cast_specs = _flat_cast_blocks(to_cast, steps, lambda s: s)
    ins = [(x, _xt_spec(b, tr, d)),
           (w['norm_mix'], _resident(w['norm_mix'].shape)),
           (w['lru_w_in'], _resident(w['lru_w_in'].shape, j)),
           (w['lru_conv_w'], _resident(w['lru_conv_w'].shape, j)),
           (w['lru_conv_b'], _resident(w['lru_conv_b'].shape)),
           (w['lru_wa'], _resident(w['lru_wa'].shape, j)),
           (w['lru_ba'], _resident(w['lru_ba'].shape)),
           (w['lru_wx'], _resident(w['lru_wx'].shape, j)),
           (w['lru_bx'], _resident(w['lru_bx'].shape)),
           (w['lru_lambda'], _resident(w['lru_lambda'].shape)),
           (w['lru_w_out'], _resident(w['lru_w_out'].shape, j))]
    ins += list(zip(flat, cast_specs))
    outs = pl.pallas_call(
        functools.partial(_p_lru_kernel, layer=layer, j=j, n_cast=len(flat)),
        grid=(steps,),
        in_specs=[s for _, s in ins],
        out_specs=[_xt_spec(b, tr, d), _whole_out((b, d)), _whole_out((hrows, d))] + cast_specs,
        out_shape=([_sds((b, t, d)), _sds((b, d)), _sds((hrows, d))]
                   + [_sds(a.shape, BF16) for a in flat]),
        scratch_shapes=[pltpu.VMEM((nl, hrows + b * tr, LANES), F32),
                        pltpu.VMEM((nl, b * tr, LANES), F32),
                        pltpu.VMEM((b, d), F32)],
        compiler_params=_params(1),
        name="prompt_lru",
    )(*[a for a, _ in ins])
    casted = [o.reshape(a.shape) for o, a in zip(outs[3:], to_cast)]
    return outs[0], outs[1], outs[2], casted


def _prompt_cmod(x, layer, j, w):
    b, t, d = x.shape
    tr = CMOD_TIME_ROWS
    hist = w['cm_dw_w'].shape[1] - 1
    assert b == SUBLANES and tr % SUBLANES == 0 and tr >= hist and d % CMOD_COL_TILE == 0
    nl, hrows = d // LANES, hist * b
    ins = [(x, _xt_spec(b, tr, d)),
           (w['norm_mix'], _resident(w['norm_mix'].shape)),
           (w['cm_w_pw1'], _resident(w['cm_w_pw1'].shape, j)),
           (w['cm_b_pw1'], _resident(w['cm_b_pw1'].shape)),
           (w['cm_dw_w'], _resident(w['cm_dw_w'].shape, j)),
           (w['cm_dw_b'], _resident(w['cm_dw_b'].shape)),
           (w['cm_ln_g'], _resident(w['cm_ln_g'].shape)),
           (w['cm_ln_b'], _resident(w['cm_ln_b'].shape)),
           (w['cm_w_pw2'], _resident(w['cm_w_pw2'].shape, j))]
    return pl.pallas_call(
        functools.partial(_p_cmod_kernel, layer=layer, j=j),
        grid=(t // tr,),
        in_specs=[s for _, s in ins],
        out_specs=[_xt_spec(b, tr, d), _whole_out((hrows, d))],
        out_shape=[_sds((b, t, d)), _sds((hrows, d))],
        scratch_shapes=[pltpu.VMEM((nl, hrows + b * tr, LANES), F32),
                        pltpu.VMEM((nl, b * tr, LANES), F32)],
        compiler_params=_params(1),
        name="prompt_cmod",
    )(*[a for a, _ in ins])


def _prompt_xattn(x, layer, kt, vb, w):
    b, t, d = x.shape
    tt = XATTN_T_TILE
    m = vb.shape[2]
    ins = [(x, _x_spec(tt, d)),
           (w['norm_xa'], _resident(w['norm_xa'].shape)),
           (w['xa_w_q'], _resident(w['xa_w_q'].shape, layer)),
           (kt, pl.BlockSpec((None, None, d, m), lambda n, t_: (layer, n, 0, 0))),
           (vb, pl.BlockSpec((None, None, m, d), lambda n, t_: (layer, n, 0, 0))),
           (w['xa_w_o'], _resident(w['xa_w_o'].shape, layer))]
    return pl.pallas_call(
        functools.partial(_p_xattn_kernel, layer=layer),
        grid=(b, t // tt),
        in_specs=[s for _, s in ins],
        out_specs=_x_spec(tt, d),
        out_shape=_sds((b, t, d)),
        compiler_params=_params(2),
        name="prompt_xattn",
    )(*[a for a, _ in ins])


def _prompt_ffn_sample_attn(x, layer, final, q, cache_k, cache_v, w):
    b, t, d = x.shape
    tt = FFN_T_TILE
    n_t = t // tt
    dff = w['ffn_w_down'].shape[1]
    hist = w['ffn_conv_w'].shape[1] - 1
    r = q.shape[0]
    _, _, m, nh, dh = cache_k.shape
    assert r % (b * n_t) == 0 and dh % LANES == 0 and nh * (dh // LANES) == SUBLANES
    nb = r // (b * n_t)
    kv_spec = pl.BlockSpec((None, nb, m, SUBLANES, LANES),
                           lambda n, s: (layer, n * n_t + s, 0, 0, 0))
    row_spec = pl.BlockSpec((nb, SUBLANES, LANES), lambda n, s: (n * n_t + s, 0, 0))
    ins = [(x, _x_spec(tt, d)),
           (w['norm_ffn'], _resident(w['norm_ffn'].shape)),
           (w['ffn_w_up'], _resident(w['ffn_w_up'].shape, layer)),
           (w['ffn_conv_w'], _resident(w['ffn_conv_w'].shape, layer)),
           (w['ffn_conv_b'], _resident(w['ffn_conv_b'].shape)),
           (w['ffn_w_down'], _resident(w['ffn_w_down'].shape, layer)),
           (w['norm_final'], _resident(w['norm_final'].shape)),
           (_pack_heads(q.reshape(r, nh, dh)), row_spec),
           (_pack_heads(cache_k), kv_spec), (_pack_heads(cache_v), kv_spec)]
    xo, fbuf, o = pl.pallas_call(
        functools.partial(_p_ffn_kernel, layer=layer, final=final),
        grid=(b, n_t),
        in_specs=[s for _, s in ins],
        out_specs=[_x_spec(tt, d), _state_spec(hist, dff), row_spec],
        out_shape=[_sds((b, t, d)), _sds((b, hist, dff)), _sds((r, SUBLANES, LANES))],
        scratch_shapes=[pltpu.VMEM((SUBLANES + tt, dff), F32)],
        compiler_params=pltpu.CompilerParams(dimension_semantics=("arbitrary", "arbitrary"),
                                             vmem_limit_bytes=FFN_VMEM_LIMIT),
        name="prompt_ffn_sample_attn",
    )(*[a for a, _ in ins])
    o = o.reshape(r, dh // LANES, nh, LANES).swapaxes(-3, -2)
    return xo, fbuf, o.reshape(r, d)


def _pack_heads(a):
    *lead, nh, dh = a.shape
    a = a.reshape(*lead, nh, dh // LANES, LANES).swapaxes(-3, -2)
    return a.reshape(*lead, (dh // LANES) * nh, LANES)


def _sample_lru(x, h0, cst, layer, j, w):
    r, d = x.shape
    ins = [(x, _whole(x.shape)), (h0, _whole(h0.shape, j)), (cst, _whole(cst.shape, j)),
           (w['norm_mix'], _whole(w['norm_mix'].shape)),
           (w['lru_w_in'], _whole(w['lru_w_in'].shape, j)),
           (w['lru_conv_w'], _whole(w['lru_conv_w'].shape, j)),
           (w['lru_conv_b'], _whole(w['lru_conv_b'].shape)),
           (w['lru_wa'], _whole(w['lru_wa'].shape, j)),
           (w['lru_ba'], _whole(w['lru_ba'].shape)),
           (w['lru_wx'], _whole(w['lru_wx'].shape, j)),
           (w['lru_bx'], _whole(w['lru_bx'].shape)),
           (w['lru_lambda'], _whole(w['lru_lambda'].shape)),
           (w['lru_w_out'], _whole(w['lru_w_out'].shape, j)),
           (w['norm_xa'], _whole(w['norm_xa'].shape)),
           (w['xa_w_q'], _whole(w['xa_w_q'].shape, layer))]
    return pl.pallas_call(
        functools.partial(_s_lru_kernel, layer=layer, j=j),
        in_specs=[s for _, s in ins],
        out_specs=[_whole_out((r, d)), _whole_out((r, d)), _whole_out((r, d)),
                   _whole_out(cst.shape[1:])],
        out_shape=[_sds((r, d)), _sds((r, d)), _sds((r, d)), _sds(cst.shape[1:])],
        grid=(1,),
        compiler_params=_params(1),
        name="sample_lru",
    )(*[a for a, _ in ins])


def _sample_ffn(x, o, st, layer, final, w):
    r, d = x.shape
    _, _, hist, dff = st.shape
    kw = hist + 1
    nc = SAMPLE_FFN_BLOCKS
    cols = dff // nc
    depth = w['ffn_conv_b'].shape[0]
    ins = [(x, _whole(x.shape)), (o, _whole(o.shape)),
           (w['xa_w_o'], _whole(w['xa_w_o'].shape, layer)),
           (w['norm_ffn'], _whole(w['norm_ffn'].shape)),
           (w['ffn_w_up'], pl.BlockSpec((None, d, cols), lambda c: (layer, 0, c))),
           (w['ffn_w_up'], pl.BlockSpec((None, d, cols), lambda c: (layer, 0, nc + c))),
           (w['ffn_conv_w'], pl.BlockSpec((None, kw, cols), lambda c: (layer, 0, c))),
           (w['ffn_conv_b'], pl.BlockSpec((depth, cols), lambda c: (0, c))),
           (w['ffn_w_down'], pl.BlockSpec((None, cols, d), lambda c: (layer, c, 0))),
           (st, pl.BlockSpec((None, r, hist, cols), lambda c: (layer, 0, 0, c))),
           (w['norm_final'], _whole(w['norm_final'].shape))]
    return pl.pallas_call(
        functools.partial(_s_ffn_kernel, layer=layer, final=final),
        grid=(nc,),
        in_specs=[s for _, s in ins],
        out_specs=[_whole_out((r, d)), pl.BlockSpec((r, hist, cols), lambda c: (0, 0, c))],
        out_shape=[_sds((r, d)), _sds((r, hist, dff))],
        scratch_shapes=[pltpu.VMEM((r, d), F32), pltpu.VMEM((r, d), BF16),
                        pltpu.VMEM((r, d), F32)],
        compiler_params=_params(1),
        name="sample_ffn",
    )(*[a for a, _ in ins])


def _sample_cmod(x, st, layer, j, w):
    r, d = x.shape
    ins = [(x, _whole(x.shape)),
           (w['norm_mix'], _whole(w['norm_mix'].shape)),
           (w['cm_w_pw1'], _whole(w['cm_w_pw1'].shape, j)),
           (w['cm_b_pw1'], _whole(w['cm_b_pw1'].shape))]
    g = pl.pallas_call(
        functools.partial(_s_cmod_pre_kernel, layer=layer, j=j),
        in_specs=[s for _, s in ins],
        out_specs=_whole_out((r, d)),
        out_shape=_sds((r, d)),
        grid=(1,),
        compiler_params=_params(1),
        name="sample_cmod_pre",
    )(*[a for a, _ in ins])

    rows = SAMPLE_CONV_ROWS
    hist = st.shape[1]
    ins = [(st, pl.BlockSpec((None, hist, rows, d), lambda n: (j, 0, n, 0))),
           (g, pl.BlockSpec((rows, d), lambda n: (n, 0))),
           (w['cm_dw_w'], _resident(w['cm_dw_w'].shape, j)),
           (w['cm_dw_b'], _resident(w['cm_dw_b'].shape)),
           (w['cm_ln_g'], _resident(w['cm_ln_g'].shape)),
           (w['cm_ln_b'], _resident(w['cm_ln_b'].shape))]
    c, new_st = pl.pallas_call(
        functools.partial(_s_cmod_conv_kernel, j=j),
        grid=(r // rows,),
        in_specs=[s for _, s in ins],
        out_specs=[pl.BlockSpec((rows, d), lambda n: (n, 0)),
                   pl.BlockSpec((hist, rows, d), lambda n: (0, n, 0))],
        out_shape=[_sds((r, d)), _sds((hist, r, d))],
        compiler_params=_params(1),
        name="sample_cmod_conv",
    )(*[a for a, _ in ins])

    ins = [(x, _whole(x.shape)), (c, _whole(c.shape)),
           (w['cm_w_pw2'], _whole(w['cm_w_pw2'].shape, j)),
           (w['norm_xa'], _whole(w['norm_xa'].shape)),
           (w['xa_w_q'], _whole(w['xa_w_q'].shape, layer))]
    x1, q = pl.pallas_call(
        functools.partial(_s_cmod_post_kernel, layer=layer),
        in_specs=[s for _, s in ins],
        out_specs=[_whole_out((r, d)), _whole_out((r, d))],
        out_shape=[_sds((r, d)), _sds((r, d))],
        grid=(1,),
        compiler_params=_params(1),
        name="sample_cmod_post",
    )(*[a for a, _ in ins])
    return x1, q, new_st


_MIXER0_MATMUL_WEIGHTS = ('lru_w_in', 'lru_wa', 'lru_wx', 'lru_w_out')
_LATE_MATMUL_WEIGHTS = ('cm_w_pw1', 'cm_w_pw2', 'xa_w_q', 'xa_w_o', 'ffn_w_up', 'ffn_w_down')


def kernel(x_prompt, x_sample, state_lru_h, state_lru_conv, state_cmod_conv, state_ffn_conv, cache_mem_k, cache_mem_v, mem_prompt, norm_mix, norm_xa, norm_ffn, norm_final, lru_w_in, lru_conv_w, lru_conv_b, lru_wa, lru_ba, lru_wx, lru_bx, lru_lambda, lru_w_out, cm_w_pw1, cm_b_pw1, cm_dw_w, cm_dw_b, cm_ln_g, cm_ln_b, cm_w_pw2, mem_norm, xa_w_q, xa_w_kv, xa_w_o, ffn_w_up, ffn_conv_w, ffn_conv_b, ffn_w_down):
    w = dict(norm_mix=norm_mix, norm_xa=norm_xa, norm_ffn=norm_ffn,
             norm_final=norm_final.reshape(1, -1),
             lru_w_in=lru_w_in, lru_conv_w=lru_conv_w, lru_conv_b=lru_conv_b, lru_wa=lru_wa,
             lru_ba=lru_ba, lru_wx=lru_wx, lru_bx=lru_bx, lru_lambda=lru_lambda,
             lru_w_out=lru_w_out, cm_w_pw1=cm_w_pw1, cm_b_pw1=cm_b_pw1, cm_dw_w=cm_dw_w,
             cm_dw_b=cm_dw_b, cm_ln_g=cm_ln_g, cm_ln_b=cm_ln_b, cm_w_pw2=cm_w_pw2,
             xa_w_q=xa_w_q, xa_w_kv=xa_w_kv, xa_w_o=xa_w_o,
             ffn_w_up=ffn_w_up, ffn_conv_w=ffn_conv_w, ffn_conv_b=ffn_conv_b,
             ffn_w_down=ffn_w_down)

    depth = norm_mix.shape[0]
    bsz, _, d = x_prompt.shape
    n_mem = mem_prompt.shape[1]
    heads = cache_mem_k.shape[3]

    (k_p, v_p, kt_p, vb_p), casted = _prompt_memkv(
        mem_prompt, mem_norm.reshape(depth, 1, d), w['xa_w_kv'], heads,
        [w[n] for n in _MIXER0_MATMUL_WEIGHTS])
    w.update(zip(_MIXER0_MATMUL_WEIGHTS, casted))
    x = x_prompt
    r = x_sample.shape[0]
    xs = x_sample.reshape(r, d)
    lconv = state_lru_conv.transpose(0, 2, 1, 3)
    cconv = state_cmod_conv.transpose(0, 2, 1, 3)
    p_h, p_lconv, p_cconv, p_fconv = [], [], [], []
    s_h, s_lconv, s_cconv, s_fconv = [], [], [], []
    for i in range(depth):
        j = i // 2
        if i % 2 == 0:
            late = [n for n in _LATE_MATMUL_WEIGHTS if w[n].dtype != BF16]
            x, h_last, cbuf, casted = _prompt_lru(x, i, j, w, [w[n] for n in late])
            w.update(zip(late, casted))
            p_h.append(h_last)
            p_lconv.append(_steps_to_sequences(cbuf, bsz))
            xs, q, h_new, c_new = _sample_lru(xs, state_lru_h, lconv, i, j, w)
            s_h.append(h_new)
            s_lconv.append(c_new.transpose(1, 0, 2))
        else:
            x, cbuf = _prompt_cmod(x, i, j, w)
            p_cconv.append(_steps_to_sequences(cbuf, bsz))
            xs, q, c_new = _sample_cmod(xs, cconv, i, j, w)
            s_cconv.append(c_new.transpose(1, 0, 2))
        x = _prompt_xattn(x, i, kt_p, vb_p, w)
        x, fbuf, o = _prompt_ffn_sample_attn(x, i, i == depth - 1, q, cache_mem_k, cache_mem_v, w)
        p_fconv.append(fbuf)
        xs, f_new = _sample_ffn(xs, o, state_ffn_conv, i, i == depth - 1, w)
        s_fconv.append(f_new)
    y_prompt = x
    y_sample = xs.reshape(x_sample.shape)

    return (y_prompt, y_sample,
            jnp.stack(p_h), jnp.stack(p_lconv), jnp.stack(p_cconv), jnp.stack(p_fconv),
            k_p, v_p,
            jnp.stack(s_h), jnp.stack(s_lconv), jnp.stack(s_cconv), jnp.stack(s_fconv))
```

```python
import functools

import jax
import jax.numpy as jnp
from jax import lax
from jax.experimental import pallas as pl
from jax.experimental.pallas import tpu as pltpu

F32 = jnp.float32
BF16 = jnp.bfloat16

EPS = 1e-6
RG_C = 8.0
N_RG_BLOCKS = 4
XA_HEADS = 4
SUBLANES = 8
LANES = 128
BF16_SUBLANES = 16
VMEM_LIMIT = 56 << 20
FFN_VMEM_LIMIT = 60 << 20

LRU_TIME_ROWS = 64
CMOD_TIME_ROWS = 64
XATTN_T_TILE = 1024
FFN_T_TILE = 512
FFN_COL_TILE = 1536
CMOD_COL_TILE = 256
CMOD_ROW_TILE = 256
SAMPLE_FFN_BLOCKS = 4
SAMPLE_CONV_ROWS = 32


def _rms(x, g):
    return x * lax.rsqrt(jnp.mean(x * x, axis=-1, keepdims=True) + EPS) * g


def _layer_norm(x, g, b):
    mu = jnp.mean(x, axis=-1, keepdims=True)
    xc = x - mu
    var = jnp.mean(xc * xc, axis=-1, keepdims=True)
    return xc * lax.rsqrt(var + EPS) * g + b


def _mm(a, w):
    return jnp.dot(a.astype(BF16), w, preferred_element_type=F32)


_GELU_C = 0.7978845608028654


def _gelu(x):
    inner = x * (_GELU_C + (_GELU_C * 0.044715) * (x * x))
    return x * (0.5 + 0.5 * jnp.tanh(inner))


def _softplus(z):
    return jnp.maximum(z, 0.0) + jnp.log1p(jnp.exp(-jnp.abs(z)))


def _block_diag(xb, w_ref):
    blk = xb.shape[-1] // N_RG_BLOCKS
    return jnp.concatenate(
        [jnp.dot(xb[:, n * blk:(n + 1) * blk], w_ref[n], preferred_element_type=F32)
         for n in range(N_RG_BLOCKS)], axis=-1)


def _lru_decay_input(rec, ga, gx, lam):
    log_a = -RG_C * jax.nn.sigmoid(ga) * _softplus(-lam)
    a = jnp.exp(log_a)
    u = jnp.sqrt(1.0 - a * a) * jax.nn.sigmoid(gx) * rec
    return a, u


def _lru_gates(rec, wa_ref, ba, wx_ref, bx, lam):
    rb = rec.astype(BF16)
    return _lru_decay_input(rec, _block_diag(rb, wa_ref) + ba, _block_diag(rb, wx_ref) + bx, lam)


def _glu(ab, d):
    return ab[:, :d] * jax.nn.sigmoid(ab[:, d:])


def _cast_blocks(srcs, dsts):
    for src, dst in zip(srcs, dsts):
        dst[...] = src[...].astype(BF16)


def _p_memkv_kernel(m_ref, g_ref, wkv_ref, *rest, n_cast, heads):
    cast_in = rest[:n_cast]
    k_ref, v_ref, kt_ref, vb_ref = rest[n_cast:n_cast + 4]
    cast_out = rest[n_cast + 4:2 * n_cast + 4]
    wkv_s, = rest[2 * n_cast + 4:]
    _cast_blocks(cast_in, cast_out)

    @pl.when(pl.program_id(1) == 0)
    def _():
        wkv_s[...] = wkv_ref[...].astype(BF16)

    d = m_ref.shape[-1]
    hn = _rms(m_ref[...], g_ref[...]).astype(BF16)
    k = jnp.dot(hn, wkv_s[:, :d], preferred_element_type=F32)
    v = jnp.dot(hn, wkv_s[:, d:], preferred_element_type=F32)
    m = k.shape[0]
    packed_rows = k_ref.shape[0] // m
    dh = d // heads
    for r in range(packed_rows):
        lt, h = divmod(r, heads)
        cs = slice(h * dh + lt * LANES, h * dh + (lt + 1) * LANES)
        k_ref[pl.ds(r, m, stride=packed_rows), :] = k[:, cs]
        v_ref[pl.ds(r, m, stride=packed_rows), :] = v[:, cs]
    kt_ref[...] = k.T.astype(BF16)
    vb_ref[...] = v.astype(BF16)


def _p_lru_kernel(x_ref, nm_ref, win_ref, cw_ref, cb_ref, wa_ref, ba_ref, wx_ref, bx_ref,
                  lam_ref, wout_ref, *rest, layer, j, n_cast):
    cast_in = rest[:n_cast]
    xo_ref, ho_ref, co_ref = rest[n_cast:n_cast + 3]
    cast_out = rest[n_cast + 3:2 * n_cast + 3]
    rec_t, y_t, h_s = rest[2 * n_cast + 3:]
    _cast_blocks(cast_in, cast_out)

    nb, tr, d = x_ref.shape
    rows = nb * tr
    kw = cw_ref.shape[0]
    hist = kw - 1
    hrows = hist * nb
    nl = d // LANES
    tc = pl.program_id(0)

    @pl.when(tc == 0)
    def _():
        rec_t[:, 0:hrows, :] = jnp.zeros((nl, hrows, LANES), F32)
        h_s[...] = jnp.zeros_like(h_s)

    x = x_ref[...].reshape(rows, d)
    hn = _rms(x, nm_ref[layer:layer + 1, :]).astype(BF16)
    cblk = d // N_RG_BLOCKS
    lanes_per_blk = cblk // LANES

    def rec_proj(n):
        lo = d + n * cblk
        return jnp.dot(hn, win_ref[:, lo:lo + cblk], preferred_element_type=F32)

    out = None
    nxt = rec_proj(0)
    for n in range(N_RG_BLOCKS):
        rec = nxt
        gate = jnp.dot(hn, win_ref[:, n * cblk:(n + 1) * cblk], preferred_element_type=F32)
        if n + 1 < N_RG_BLOCKS:
            nxt = rec_proj(n + 1)
        cs = slice(n * cblk, (n + 1) * cblk)
        slabs = []
        for lt in range(lanes_per_blk):
            l = n * lanes_per_blk + lt
            ls = slice(l * LANES, (l + 1) * LANES)
            for b in range(nb):
                rec_t[l, pl.ds(hrows + b, tr, stride=nb), :] = (
                    rec[b * tr:(b + 1) * tr, lt * LANES:(lt + 1) * LANES])
            acc = cb_ref[j:j + 1, ls] + cw_ref[hist:kw, ls] * rec_t[l, hrows:hrows + rows, :]
            for k in range(hist):
                acc = acc + cw_ref[k:k + 1, ls] * rec_t[l, k * nb:k * nb + rows, :]
            slabs.append(acc)
        conv = jnp.concatenate(slabs, axis=-1)
        cb16 = conv.astype(BF16)
        ga = jnp.dot(cb16, wa_ref[n], preferred_element_type=F32) + ba_ref[j:j + 1, cs]
        gx = jnp.dot(cb16, wx_ref[n], preferred_element_type=F32) + bx_ref[j:j + 1, cs]
        a, u = _lru_decay_input(conv, ga, gx, lam_ref[j:j + 1, cs])

        h = h_s[:, cs]
        for t in range(tr):
            h = a[t * nb:(t + 1) * nb, :] * h + u[t * nb:(t + 1) * nb, :]
            for lt in range(lanes_per_blk):
                y_t[n * lanes_per_blk + lt, t * nb:(t + 1) * nb, :] = (
                    h[:, lt * LANES:(lt + 1) * LANES])
        h_s[:, cs] = h
        ho_ref[:, cs] = h

        y = jnp.concatenate(
            [jnp.concatenate([y_t[n * lanes_per_blk + lt, pl.ds(b, tr, stride=nb), :]
                              for lt in range(lanes_per_blk)], axis=-1)
             for b in range(nb)], axis=0)
        part = jnp.dot((_gelu(gate) * y).astype(BF16), wout_ref[cs, :],
                       preferred_element_type=F32)
        out = part if out is None else out + part
    tail = rec_t[:, rows:rows + hrows, :]
    co_ref[...] = jnp.concatenate([tail[l] for l in range(nl)], axis=-1)
    rec_t[:, 0:hrows, :] = tail
    xo_ref[...] = (x + out).reshape(nb, tr, d)


def _p_cmod_kernel(x_ref, nm_ref, pw1_ref, b1_ref, dw_ref, dwb_ref, lng_ref, lnb_ref, pw2_ref,
                   xo_ref, so_ref, g_t, c_t, *, layer, j):
    nb, tr, d = x_ref.shape
    rows = nb * tr
    kw = dw_ref.shape[0]
    hist = kw - 1
    hrows = hist * nb
    lanes_per_col = CMOD_COL_TILE // LANES
    tc = pl.program_id(0)

    @pl.when(tc == 0)
    def _():
        g_t[:, 0:hrows, :] = jnp.zeros((d // LANES, hrows, LANES), F32)

    x = x_ref[...].reshape(rows, d)
    hn = _rms(x, nm_ref[layer:layer + 1, :]).astype(BF16)
    def pw1(c):
        lo = c * CMOD_COL_TILE
        cs = slice(lo, lo + CMOD_COL_TILE)
        gs = slice(d + lo, d + lo + CMOD_COL_TILE)
        return (jnp.dot(hn, pw1_ref[:, cs], preferred_element_type=F32) + b1_ref[j:j + 1, cs],
                jnp.dot(hn, pw1_ref[:, gs], preferred_element_type=F32) + b1_ref[j:j + 1, gs])

    n_col = d // CMOD_COL_TILE
    nxt = pw1(0)
    for c in range(n_col):
        a, bg = nxt
        if c + 1 < n_col:
            nxt = pw1(c + 1)
        g = a * jax.nn.sigmoid(bg)
        for lt in range(lanes_per_col):
            l = c * lanes_per_col + lt
            ls = slice(l * LANES, (l + 1) * LANES)
            for b in range(nb):
                g_t[l, pl.ds(hrows + b, tr, stride=nb), :] = (
                    g[b * tr:(b + 1) * tr, lt * LANES:(lt + 1) * LANES])
            for rb in range(rows // CMOD_ROW_TILE):
                r0 = hrows + rb * CMOD_ROW_TILE
                acc = dwb_ref[j:j + 1, ls] + dw_ref[hist:kw, ls] * g_t[l, r0:r0 + CMOD_ROW_TILE, :]
                for s in range(1, kw):
                    acc = acc + (dw_ref[hist - s:kw - s, ls]
                                 * g_t[l, r0 - s * nb:r0 - s * nb + CMOD_ROW_TILE, :])
                c_t[l, rb * CMOD_ROW_TILE:(rb + 1) * CMOD_ROW_TILE, :] = acc
    tail = g_t[:, rows:rows + hrows, :]
    so_ref[...] = jnp.concatenate([tail[l] for l in range(d // LANES)], axis=-1)
    g_t[:, 0:hrows, :] = tail

    conv = jnp.concatenate(
        [jnp.concatenate([c_t[l, pl.ds(b, tr, stride=nb), :] for l in range(d // LANES)], axis=-1)
         for b in range(nb)], axis=0)
    c = jax.nn.silu(_layer_norm(conv, lng_ref[j:j + 1, :], lnb_ref[j:j + 1, :]))
    xo_ref[...] = (x + _mm(c, pw2_ref[...])).reshape(nb, tr, d)


def _p_xattn_kernel(x_ref, nm_ref, wq_ref, kt_ref, v_ref, wo_ref, xo_ref, *, layer):
    tt, d = x_ref.shape
    dh = d // XA_HEADS
    x = x_ref[...]
    q = _mm(_rms(x, nm_ref[layer:layer + 1, :]), wq_ref[...]).astype(BF16)
    cols = [slice(h * dh, (h + 1) * dh) for h in range(XA_HEADS)]
    scores = [jnp.dot(q[:, cs], kt_ref[cs, :], preferred_element_type=F32) * (dh ** -0.5)
              for cs in cols]
    heads = []
    for s, cs in zip(scores, cols):
        e = jnp.exp(s - jnp.max(s, axis=-1, keepdims=True))
        p = e / jnp.sum(e, axis=-1, keepdims=True)
        heads.append(jnp.dot(p.astype(BF16), v_ref[:, cs], preferred_element_type=F32))
    o = jnp.concatenate(heads, axis=-1)
    xo_ref[...] = x + _mm(o, wo_ref[...])


def _sample_attend(q_ref, k_ref, v_ref, o_ref, seqs, heads):
    rows, lanes = q_ref.shape[-2:]
    dh = lanes * (rows // heads)
    for b in seqs:
        q = q_ref[b] * (dh ** -0.5)
        prod = k_ref[b] * q[None]
        shift = heads
        while shift < rows:
            prod = prod + pltpu.roll(prod, shift, axis=1)
            shift *= 2
        s = jnp.sum(prod, axis=-1, keepdims=True)
        e = jnp.exp(s - jnp.max(s, axis=0, keepdims=True))
        o_ref[b] = jnp.sum(e * v_ref[b], axis=0) / jnp.sum(e, axis=0)


def _p_ffn_kernel(x_ref, nm_ref, wup_ref, cw_ref, cb_ref, wdn_ref, nf_ref, q_ref, k_ref, v_ref,
                  xo_ref, so_ref, ao_ref, buf, *, layer, final):
    tt, d = x_ref.shape
    dff = wdn_ref.shape[0]
    kw = cw_ref.shape[0]
    hist = kw - 1
    tc = pl.program_id(1)

    @pl.when(tc == 0)
    def _():
        buf[0:SUBLANES, :] = jnp.zeros((SUBLANES, dff), F32)

    x = x_ref[...]
    hn = _rms(x, nm_ref[layer:layer + 1, :]).astype(BF16)
    def up(c):
        lo = c * FFN_COL_TILE
        return (jnp.dot(hn, wup_ref[:, lo:lo + FFN_COL_TILE], preferred_element_type=F32),
                jnp.dot(hn, wup_ref[:, dff + lo:dff + lo + FFN_COL_TILE],
                        preferred_element_type=F32))

    n_col = dff // FFN_COL_TILE
    n_seq = q_ref.shape[0]
    acc = None
    nxt = up(0)
    for c in range(n_col):
        cs = slice(c * FFN_COL_TILE, (c + 1) * FFN_COL_TILE)
        g, u = nxt
        if c + 1 < n_col:
            nxt = up(c + 1)
        _sample_attend(q_ref, k_ref, v_ref, ao_ref,
                       range(c * n_seq // n_col, (c + 1) * n_seq // n_col), XA_HEADS)
        buf[SUBLANES:SUBLANES + tt, cs] = g
        conv = cb_ref[layer:layer + 1, cs] + cw_ref[hist:kw, cs] * g
        for k in range(hist):
            off = SUBLANES - hist + k
            conv = conv + cw_ref[k:k + 1, cs] * buf[off:off + tt, cs]
        act = (_gelu(conv) * u).astype(BF16)
        part = jnp.dot(act, wdn_ref[cs, :], preferred_element_type=F32)
        acc = part if acc is None else acc + part
    tail = buf[SUBLANES + tt - hist:SUBLANES + tt, :]
    so_ref[...] = tail
    buf[SUBLANES - hist:SUBLANES, :] = tail

    xn = x + acc
    if final:
        xn = _rms(xn, nf_ref[...])
    xo_ref[...] = xn


def _s_lru_kernel(x_ref, h0_ref, cst_ref, nm_ref, win_ref, cw_ref, cb_ref, wa_ref, ba_ref,
                  wx_ref, bx_ref, lam_ref, wout_ref, nxa_ref, wq_ref,
                  xo_ref, q_ref, ho_ref, co_ref, *, layer, j):
    d = x_ref.shape[-1]
    kw = cw_ref.shape[0]
    hist = kw - 1
    x = x_ref[...]
    hn = _rms(x, nm_ref[layer:layer + 1, :]).astype(BF16)
    gate = jnp.dot(hn, win_ref[:, :d], preferred_element_type=F32)
    rec = jnp.dot(hn, win_ref[:, d:], preferred_element_type=F32)

    conv = cb_ref[j:j + 1, :] + cw_ref[hist:kw, :] * rec
    for k in range(hist):
        conv = conv + cw_ref[k:k + 1, :] * cst_ref[k]
    for k in range(hist - 1):
        co_ref[k] = cst_ref[k + 1]
    co_ref[hist - 1] = rec

    a, u = _lru_gates(conv, wa_ref, ba_ref[j:j + 1, :], wx_ref, bx_ref[j:j + 1, :],
                      lam_ref[j:j + 1, :])
    h = a * h0_ref[...] + u
    ho_ref[...] = h
    x1 = x + _mm(_gelu(gate) * h, wout_ref[...])
    xo_ref[...] = x1
    q_ref[...] = _mm(_rms(x1, nxa_ref[layer:layer + 1, :]), wq_ref[...])


def _s_ffn_kernel(x_ref, o_ref, wo_ref, nm_ref, wg_ref, wu_ref, cw_ref, cb_ref, wdn_ref, st_ref,
                  nf_ref, xo_ref, so_ref, x_s, hn_s, acc_s, *, layer, final):
    c = pl.program_id(0)
    kw = cw_ref.shape[0]
    hist = kw - 1

    @pl.when(c == 0)
    def _():
        x = x_ref[...] + _mm(o_ref[...], wo_ref[...])
        x_s[...] = x
        hn_s[...] = _rms(x, nm_ref[layer:layer + 1, :]).astype(BF16)
        acc_s[...] = jnp.zeros_like(acc_s)

    hn = hn_s[...]
    g = jnp.dot(hn, wg_ref[...], preferred_element_type=F32)
    u = jnp.dot(hn, wu_ref[...], preferred_element_type=F32)
    conv = cb_ref[layer:layer + 1, :] + cw_ref[hist:kw, :] * g
    for k in range(hist):
        conv = conv + cw_ref[k:k + 1, :] * st_ref[:, k, :]
    for k in range(hist - 1):
        so_ref[:, k, :] = st_ref[:, k + 1, :]
    so_ref[:, hist - 1, :] = g
    acc_s[...] += _mm(_gelu(conv) * u, wdn_ref[...])

    @pl.when(c == pl.num_programs(0) - 1)
    def _():
        xn = x_s[...] + acc_s[...]
        if final:
            xn = _rms(xn, nf_ref[...])
        xo_ref[...] = xn


def _s_cmod_pre_kernel(x_ref, nm_ref, pw1_ref, b1_ref, g_ref, *, layer, j):
    d = x_ref.shape[-1]
    hn = _rms(x_ref[...], nm_ref[layer:layer + 1, :])
    g_ref[...] = _glu(_mm(hn, pw1_ref[...]) + b1_ref[j:j + 1, :], d)


def _s_cmod_conv_kernel(st_ref, g_ref, dw_ref, dwb_ref, lng_ref, lnb_ref, c_ref, so_ref, *, j):
    hist = st_ref.shape[0]
    kw = hist + 1
    g = g_ref[...]
    c = dwb_ref[j:j + 1, :] + dw_ref[hist:kw, :] * g
    for k in range(hist):
        c = c + dw_ref[k:k + 1, :] * st_ref[k]
    for k in range(hist - 1):
        so_ref[k] = st_ref[k + 1]
    so_ref[hist - 1] = g
    c_ref[...] = jax.nn.silu(_layer_norm(c, lng_ref[j:j + 1, :], lnb_ref[j:j + 1, :]))


def _s_cmod_post_kernel(x_ref, c_ref, pw2_ref, nxa_ref, wq_ref, xo_ref, q_ref, *, layer):
    x1 = x_ref[...] + _mm(c_ref[...], pw2_ref[...])
    xo_ref[...] = x1
    q_ref[...] = _mm(_rms(x1, nxa_ref[layer:layer + 1, :]), wq_ref[...])


def _params(n_grid):
    return pltpu.CompilerParams(dimension_semantics=("arbitrary",) * n_grid,
                                vmem_limit_bytes=VMEM_LIMIT)


def _resident(shape, lead=None):
    if lead is None:
        nd = len(shape)
        return pl.BlockSpec(shape, lambda *_: (0,) * nd, pipeline_mode=pl.Buffered(1))
    nd = len(shape) - 1
    return pl.BlockSpec((None,) + tuple(shape[1:]), lambda *_: (lead,) + (0,) * nd,
                        pipeline_mode=pl.Buffered(1))


_whole = _resident


def _whole_out(shape):
    nd = len(shape)
    return pl.BlockSpec(shape, lambda *_: (0,) * nd)


def _sds(shape, dtype=F32):
    return jax.ShapeDtypeStruct(shape, dtype)


def _flat_cast_blocks(to_cast, steps, step_index):
    flat = [a.reshape(-1, a.shape[-1]) for a in to_cast]
    for a in flat:
        assert a.shape[0] % (steps * BF16_SUBLANES) == 0
    specs = [pl.BlockSpec((a.shape[0] // steps, a.shape[1]), lambda *g: (step_index(*g), 0))
             for a in flat]
    return flat, specs


def _prompt_memkv(mem, mem_norm3, wkv, heads, to_cast):
    depth = wkv.shape[0]
    b, m, d = mem.shape
    assert d % LANES == 0
    packed = d // LANES
    flat, cast_specs = _flat_cast_blocks(to_cast, depth * b, lambda i, n: i * b + n)
    outs = pl.pallas_call(
        functools.partial(_p_memkv_kernel, n_cast=len(flat), heads=heads),
        grid=(depth, b),
        in_specs=[pl.BlockSpec((None, m, d), lambda i, n: (n, 0, 0)),
                  pl.BlockSpec((None, 1, d), lambda i, n: (i, 0, 0)),
                  pl.BlockSpec((None, d, 2 * d), lambda i, n: (i, 0, 0))] + cast_specs,
        out_specs=[pl.BlockSpec((None, None, m * packed, LANES), lambda i, n: (i, n, 0, 0)),
                   pl.BlockSpec((None, None, m * packed, LANES), lambda i, n: (i, n, 0, 0)),
                   pl.BlockSpec((None, None, d, m), lambda i, n: (i, n, 0, 0)),
                   pl.BlockSpec((None, None, m, d), lambda i, n: (i, n, 0, 0))] + cast_specs,
        out_shape=[_sds((depth, b, m * packed, LANES)), _sds((depth, b, m * packed, LANES)),
                   _sds((depth, b, d, m), BF16), _sds((depth, b, m, d), BF16)]
                  + [_sds(a.shape, BF16) for a in flat],
        scratch_shapes=[pltpu.VMEM((d, 2 * d), BF16)],
        compiler_params=_params(2),
        name="prompt_memkv",
    )(mem, mem_norm3, wkv, *flat)
    k5, v5 = [_unpack_heads(o.reshape(depth, b, m, packed, LANES), heads) for o in outs[:2]]
    return (k5, v5, outs[2], outs[3]), [o.reshape(a.shape) for o, a in zip(outs[4:], to_cast)]


def _x_spec(tt, d):
    return pl.BlockSpec((None, tt, d), lambda n, t: (n, t, 0))


def _state_spec(rows, cols):
    return pl.BlockSpec((None, rows, cols), lambda n, t: (n, 0, 0))


def _xt_spec(b, tr, d):
    return pl.BlockSpec((b, tr, d), lambda t: (0, t, 0))


def _steps_to_sequences(s, b):
    hrows, d = s.shape
    return s.reshape(hrows // b, b, d).transpose(1, 0, 2)


def _prompt_lru(x, layer, j, w, to_cast):
    b, t, d = x.shape
    tr = LRU_TIME_ROWS
    hist = w['lru_conv_w'].shape[1] - 1
    assert b == SUBLANES and tr % SUBLANES == 0 and tr >= hist and d % LANES == 0
    nl, hrows = d // LANES, hist * b
    steps = t // tr
    flat, cast_specs = _flat_cast_blocks(to_cast, steps, lambda s: s)
    ins = [(x, _xt_spec(b, tr, d)),
           (w['norm_mix'], _resident(w['norm_mix'].shape)),
           (w['lru_w_in'], _resident(w['lru_w_in'].shape, j)),
           (w['lru_conv_w'], _resident(w['lru_conv_w'].shape, j)),
           (w['lru_conv_b'], _resident(w['lru_conv_b'].shape)),
           (w['lru_wa'], _resident(w['lru_wa'].shape, j)),
           (w['lru_ba'], _resident(w['lru_ba'].shape)),
           (w['lru_wx'], _resident(w['lru_wx'].shape, j)),
           (w['lru_bx'], _resident(w['lru_bx'].shape)),
           (w['lru_lambda'], _resident(w['lru_lambda'].shape)),
           (w['lru_w_out'], _resident(w['lru_w_out'].shape, j))]
    ins += list(zip(flat, cast_specs))
    outs = pl.pallas_call(
        functools.partial(_p_lru_kernel, layer=layer, j=j, n_cast=len(flat)),
        grid=(steps,),
        in_specs=[s for _, s in ins],
        out_specs=[_xt_spec(b, tr, d), _whole_out((b, d)), _whole_out((hrows, d))] + cast_specs,
        out_shape=([_sds((b, t, d)), _sds((b, d)), _sds((hrows, d))]
                   + [_sds(a.shape, BF16) for a in flat]),
        scratch_shapes=[pltpu.VMEM((nl, hrows + b * tr, LANES), F32),
                        pltpu.VMEM((nl, b * tr, LANES), F32),
                        pltpu.VMEM((b, d), F32)],
        compiler_params=_params(1),
        name="prompt_lru",
    )(*[a for a, _ in ins])
    casted = [o.reshape(a.shape) for o, a in zip(outs[3:], to_cast)]
    return outs[0], outs[1], outs[2], casted


def _prompt_cmod(x, layer, j, w):
    b, t, d = x.shape
    tr = CMOD_TIME_ROWS
    hist = w['cm_dw_w'].shape[1] - 1
    assert b == SUBLANES and tr % SUBLANES == 0 and tr >= hist and d % CMOD_COL_TILE == 0
    nl, hrows = d // LANES, hist * b
    ins = [(x, _xt_spec(b, tr, d)),
           (w['norm_mix'], _resident(w['norm_mix'].shape)),
           (w['cm_w_pw1'], _resident(w['cm_w_pw1'].shape, j)),
           (w['cm_b_pw1'], _resident(w['cm_b_pw1'].shape)),
           (w['cm_dw_w'], _resident(w['cm_dw_w'].shape, j)),
           (w['cm_dw_b'], _resident(w['cm_dw_b'].shape)),
           (w['cm_ln_g'], _resident(w['cm_ln_g'].shape)),
           (w['cm_ln_b'], _resident(w['cm_ln_b'].shape)),
           (w['cm_w_pw2'], _resident(w['cm_w_pw2'].shape, j))]
    return pl.pallas_call(
        functools.partial(_p_cmod_kernel, layer=layer, j=j),
        grid=(t // tr,),
        in_specs=[s for _, s in ins],
        out_specs=[_xt_spec(b, tr, d), _whole_out((hrows, d))],
        out_shape=[_sds((b, t, d)), _sds((hrows, d))],
        scratch_shapes=[pltpu.VMEM((nl, hrows + b * tr, LANES), F32),
                        pltpu.VMEM((nl, b * tr, LANES), F32)],
        compiler_params=_params(1),
        name="prompt_cmod",
    )(*[a for a, _ in ins])


def _prompt_xattn(x, layer, kt, vb, w):
    b, t, d = x.shape
    tt = XATTN_T_TILE
    m = vb.shape[2]
    ins = [(x, _x_spec(tt, d)),
           (w['norm_xa'], _resident(w['norm_xa'].shape)),
           (w['xa_w_q'], _resident(w['xa_w_q'].shape, layer)),
           (kt, pl.BlockSpec((None, None, d, m), lambda n, t_: (layer, n, 0, 0))),
           (vb, pl.BlockSpec((None, None, m, d), lambda n, t_: (layer, n, 0, 0))),
           (w['xa_w_o'], _resident(w['xa_w_o'].shape, layer))]
    return pl.pallas_call(
        functools.partial(_p_xattn_kernel, layer=layer),
        grid=(b, t // tt),
        in_specs=[s for _, s in ins],
        out_specs=_x_spec(tt, d),
        out_shape=_sds((b, t, d)),
        compiler_params=_params(2),
        name="prompt_xattn",
    )(*[a for a, _ in ins])


def _prompt_ffn_sample_attn(x, layer, final, q, cache_k, cache_v, w):
    b, t, d = x.shape
    tt = FFN_T_TILE
    n_t = t // tt
    dff = w['ffn_w_down'].shape[1]
    hist = w['ffn_conv_w'].shape[1] - 1
    r = q.shape[0]
    _, _, m, nh, dh = cache_k.shape
    assert r % (b * n_t) == 0 and dh % LANES == 0 and nh * (dh // LANES) == SUBLANES
    nb = r // (b * n_t)
    kv_spec = pl.BlockSpec((None, nb, m, SUBLANES, LANES),
                           lambda n, s: (layer, n * n_t + s, 0, 0, 0))
    row_spec = pl.BlockSpec((nb, SUBLANES, LANES), lambda n, s: (n * n_t + s, 0, 0))
    ins = [(x, _x_spec(tt, d)),
           (w['norm_ffn'], _resident(w['norm_ffn'].shape)),
           (w['ffn_w_up'], _resident(w['ffn_w_up'].shape, layer)),
           (w['ffn_conv_w'], _resident(w['ffn_conv_w'].shape, layer)),
           (w['ffn_conv_b'], _resident(w['ffn_conv_b'].shape)),
           (w['ffn_w_down'], _resident(w['ffn_w_down'].shape, layer)),
           (w['norm_final'], _resident(w['norm_final'].shape)),
           (_pack_heads(q.reshape(r, nh, dh)), row_spec),
           (_pack_heads(cache_k), kv_spec), (_pack_heads(cache_v), kv_spec)]
    xo, fbuf, o = pl.pallas_call(
        functools.partial(_p_ffn_kernel, layer=layer, final=final),
        grid=(b, n_t),
        in_specs=[s for _, s in ins],
        out_specs=[_x_spec(tt, d), _state_spec(hist, dff), row_spec],
        out_shape=[_sds((b, t, d)), _sds((b, hist, dff)), _sds((r, SUBLANES, LANES))],
        scratch_shapes=[pltpu.VMEM((SUBLANES + tt, dff), F32)],
        compiler_params=pltpu.CompilerParams(dimension_semantics=("arbitrary", "arbitrary"),
                                             vmem_limit_bytes=FFN_VMEM_LIMIT),
        name="prompt_ffn_sample_attn",
    )(*[a for a, _ in ins])
    return xo, fbuf, _unpack_heads(o, nh).reshape(r, d)


def _pack_heads(a):
    *lead, nh, dh = a.shape
    a = a.reshape(*lead, nh, dh // LANES, LANES).swapaxes(-3, -2)
    return a.reshape(*lead, (dh // LANES) * nh, LANES)


def _unpack_heads(a, nh):
    *lead, rows, lanes = a.shape
    a = a.reshape(*lead, rows // nh, nh, lanes).swapaxes(-3, -2)
    return a.reshape(*lead, nh, (rows // nh) * lanes)


def _sample_lru(x, h0, cst, layer, j, w):
    r, d = x.shape
    ins = [(x, _whole(x.shape)), (h0, _whole(h0.shape, j)), (cst, _whole(cst.shape, j)),
           (w['norm_mix'], _whole(w['norm_mix'].shape)),
           (w['lru_w_in'], _whole(w['lru_w_in'].shape, j)),
           (w['lru_conv_w'], _whole(w['lru_conv_w'].shape, j)),
           (w['lru_conv_b'], _whole(w['lru_conv_b'].shape)),
           (w['lru_wa'], _whole(w['lru_wa'].shape, j)),
           (w['lru_ba'], _whole(w['lru_ba'].shape)),
           (w['lru_wx'], _whole(w['lru_wx'].shape, j)),
           (w['lru_bx'], _whole(w['lru_bx'].shape)),
           (w['lru_lambda'], _whole(w['lru_lambda'].shape)),
           (w['lru_w_out'], _whole(w['lru_w_out'].shape, j)),
           (w['norm_xa'], _whole(w['norm_xa'].shape)),
           (w['xa_w_q'], _whole(w['xa_w_q'].shape, layer))]
    return pl.pallas_call(
        functools.partial(_s_lru_kernel, layer=layer, j=j),
        in_specs=[s for _, s in ins],
        out_specs=[_whole_out((r, d)), _whole_out((r, d)), _whole_out((r, d)),
                   _whole_out(cst.shape[1:])],
        out_shape=[_sds((r, d)), _sds((r, d)), _sds((r, d)), _sds(cst.shape[1:])],
        grid=(1,),
        compiler_params=_params(1),
        name="sample_lru",
    )(*[a for a, _ in ins])


def _sample_ffn(x, o, st, layer, final, w):
    r, d = x.shape
    _, _, hist, dff = st.shape
    kw = hist + 1
    nc = SAMPLE_FFN_BLOCKS
    cols = dff // nc
    depth = w['ffn_conv_b'].shape[0]
    ins = [(x, _whole(x.shape)), (o, _whole(o.shape)),
           (w['xa_w_o'], _whole(w['xa_w_o'].shape, layer)),
           (w['norm_ffn'], _whole(w['norm_ffn'].shape)),
           (w['ffn_w_up'], pl.BlockSpec((None, d, cols), lambda c: (layer, 0, c))),
           (w['ffn_w_up'], pl.BlockSpec((None, d, cols), lambda c: (layer, 0, nc + c))),
           (w['ffn_conv_w'], pl.BlockSpec((None, kw, cols), lambda c: (layer, 0, c))),
           (w['ffn_conv_b'], pl.BlockSpec((depth, cols), lambda c: (0, c))),
           (w['ffn_w_down'], pl.BlockSpec((None, cols, d), lambda c: (layer, c, 0))),
           (st, pl.BlockSpec((None, r, hist, cols), lambda c: (layer, 0, 0, c))),
           (w['norm_final'], _whole(w['norm_final'].shape))]
    return pl.pallas_call(
        functools.partial(_s_ffn_kernel, layer=layer, final=final),
        grid=(nc,),
        in_specs=[s for _, s in ins],
        out_specs=[_whole_out((r, d)), pl.BlockSpec((r, hist, cols), lambda c: (0, 0, c))],
        out_shape=[_sds((r, d)), _sds((r, hist, dff))],
        scratch_shapes=[pltpu.VMEM((r, d), F32), pltpu.VMEM((r, d), BF16),
                        pltpu.VMEM((r, d), F32)],
        compiler_params=_params(1),
        name="sample_ffn",
    )(*[a for a, _ in ins])


def _sample_cmod(x, st, layer, j, w):
    r, d = x.shape
    ins = [(x, _whole(x.shape)),
           (w['norm_mix'], _whole(w['norm_mix'].shape)),
           (w['cm_w_pw1'], _whole(w['cm_w_pw1'].shape, j)),
           (w['cm_b_pw1'], _whole(w['cm_b_pw1'].shape))]
    g = pl.pallas_call(
        functools.partial(_s_cmod_pre_kernel, layer=layer, j=j),
        in_specs=[s for _, s in ins],
        out_specs=_whole_out((r, d)),
        out_shape=_sds((r, d)),
        grid=(1,),
        compiler_params=_params(1),
        name="sample_cmod_pre",
    )(*[a for a, _ in ins])

    rows = SAMPLE_CONV_ROWS
    hist = st.shape[1]
    ins = [(st, pl.BlockSpec((None, hist, rows, d), lambda n: (j, 0, n, 0))),
           (g, pl.BlockSpec((rows, d), lambda n: (n, 0))),
           (w['cm_dw_w'], _resident(w['cm_dw_w'].shape, j)),
           (w['cm_dw_b'], _resident(w['cm_dw_b'].shape)),
           (w['cm_ln_g'], _resident(w['cm_ln_g'].shape)),
           (w['cm_ln_b'], _resident(w['cm_ln_b'].shape))]
    c, new_st = pl.pallas_call(
        functools.partial(_s_cmod_conv_kernel, j=j),
        grid=(r // rows,),
        in_specs=[s for _, s in ins],
        out_specs=[pl.BlockSpec((rows, d), lambda n: (n, 0)),
                   pl.BlockSpec((hist, rows, d), lambda n: (0, n, 0))],
        out_shape=[_sds((r, d)), _sds((hist, r, d))],
        compiler_params=_params(1),
        name="sample_cmod_conv",
    )(*[a for a, _ in ins])

    ins = [(x, _whole(x.shape)), (c, _whole(c.shape)),
           (w['cm_w_pw2'], _whole(w['cm_w_pw2'].shape, j)),
           (w['norm_xa'], _whole(w['norm_xa'].shape)),
           (w['xa_w_q'], _whole(w['xa_w_q'].shape, layer))]
    x1, q = pl.pallas_call(
        functools.partial(_s_cmod_post_kernel, layer=layer),
        in_specs=[s for _, s in ins],
        out_specs=[_whole_out((r, d)), _whole_out((r, d))],
        out_shape=[_sds((r, d)), _sds((r, d))],
        grid=(1,),
        compiler_params=_params(1),
        name="sample_cmod_post",
    )(*[a for a, _ in ins])
    return x1, q, new_st


_MIXER0_MATMUL_WEIGHTS = ('lru_w_in', 'lru_wa', 'lru_wx', 'lru_w_out')
_LATE_MATMUL_WEIGHTS = ('cm_w_pw1', 'cm_w_pw2', 'xa_w_q', 'xa_w_o', 'ffn_w_up', 'ffn_w_down')


def kernel(x_prompt, x_sample, state_lru_h, state_lru_conv, state_cmod_conv, state_ffn_conv, cache_mem_k, cache_mem_v, mem_prompt, norm_mix, norm_xa, norm_ffn, norm_final, lru_w_in, lru_conv_w, lru_conv_b, lru_wa, lru_ba, lru_wx, lru_bx, lru_lambda, lru_w_out, cm_w_pw1, cm_b_pw1, cm_dw_w, cm_dw_b, cm_ln_g, cm_ln_b, cm_w_pw2, mem_norm, xa_w_q, xa_w_kv, xa_w_o, ffn_w_up, ffn_conv_w, ffn_conv_b, ffn_w_down):
    w = dict(norm_mix=norm_mix, norm_xa=norm_xa, norm_ffn=norm_ffn,
             norm_final=norm_final.reshape(1, -1),
             lru_w_in=lru_w_in, lru_conv_w=lru_conv_w, lru_conv_b=lru_conv_b, lru_wa=lru_wa,
             lru_ba=lru_ba, lru_wx=lru_wx, lru_bx=lru_bx, lru_lambda=lru_lambda,
             lru_w_out=lru_w_out, cm_w_pw1=cm_w_pw1, cm_b_pw1=cm_b_pw1, cm_dw_w=cm_dw_w,
             cm_dw_b=cm_dw_b, cm_ln_g=cm_ln_g, cm_ln_b=cm_ln_b, cm_w_pw2=cm_w_pw2,
             xa_w_q=xa_w_q, xa_w_kv=xa_w_kv, xa_w_o=xa_w_o,
             ffn_w_up=ffn_w_up, ffn_conv_w=ffn_conv_w, ffn_conv_b=ffn_conv_b,
             ffn_w_down=ffn_w_down)

    depth = norm_mix.shape[0]
    bsz, _, d = x_prompt.shape
    n_mem = mem_prompt.shape[1]
    heads = cache_mem_k.shape[3]

    (k_p, v_p, kt_p, vb_p), casted = _prompt_memkv(
        mem_prompt, mem_norm.reshape(depth, 1, d), w['xa_w_kv'], heads,
        [w[n] for n in _MIXER0_MATMUL_WEIGHTS])
    w.update(zip(_MIXER0_MATMUL_WEIGHTS, casted))
    x = x_prompt
    r = x_sample.shape[0]
    xs = x_sample.reshape(r, d)
    lconv = state_lru_conv.transpose(0, 2, 1, 3)
    cconv = state_cmod_conv.transpose(0, 2, 1, 3)
    p_h, p_lconv, p_cconv, p_fconv = [], [], [], []
    s_h, s_lconv, s_cconv, s_fconv = [], [], [], []
    for i in range(depth):
        j = i // 2
        if i % 2 == 0:
            late = [n for n in _LATE_MATMUL_WEIGHTS if w[n].dtype != BF16]
            x, h_last, cbuf, casted = _prompt_lru(x, i, j, w, [w[n] for n in late])
            w.update(zip(late, casted))
            p_h.append(h_last)
            p_lconv.append(_steps_to_sequences(cbuf, bsz))
            xs, q, h_new, c_new = _sample_lru(xs, state_lru_h, lconv, i, j, w)
            s_h.append(h_new)
            s_lconv.append(c_new.transpose(1, 0, 2))
        else:
            x, cbuf = _prompt_cmod(x, i, j, w)
            p_cconv.append(_steps_to_sequences(cbuf, bsz))
            xs, q, c_new = _sample_cmod(xs, cconv, i, j, w)
            s_cconv.append(c_new.transpose(1, 0, 2))
        x = _prompt_xattn(x, i, kt_p, vb_p, w)
        x, fbuf, o = _prompt_ffn_sample_attn(x, i, i == depth - 1, q, cache_mem_k, cache_mem_v, w)
        p_fconv.append(fbuf)
        xs, f_new = _sample_ffn(xs, o, state_ffn_conv, i, i == depth - 1, w)
        s_fconv.append(f_new)
    y_prompt = x
    y_sample = xs.reshape(x_sample.shape)

    return (y_prompt, y_sample,
            jnp.stack(p_h), jnp.stack(p_lconv), jnp.stack(p_cconv), jnp.stack(p_fconv),
            k_p, v_p,
            jnp.stack(s_h), jnp.stack(s_lconv), jnp.stack(s_cconv), jnp.stack(s_fconv))
```

```python
import functools

import jax
import jax.numpy as jnp
from jax import lax
from jax.experimental import pallas as pl
from jax.experimental.pallas import tpu as pltpu

F32 = jnp.float32
BF16 = jnp.bfloat16

EPS = 1e-6
RG_C = 8.0
N_RG_BLOCKS = 4
XA_HEADS = 4
SUBLANES = 8
LANES = 128
BF16_SUBLANES = 16
VMEM_LIMIT = 56 << 20
FFN_VMEM_LIMIT = 60 << 20

LRU_TIME_ROWS = 64
CMOD_TIME_ROWS = 128
XATTN_T_TILE = 1024
FFN_T_TILE = 512
FFN_COL_TILE = 1536
CMOD_COL_TILE = 256
CMOD_ROW_TILE = 256
SAMPLE_CONV_ROWS = 32


def _rms(x, g):
    return x * lax.rsqrt(jnp.mean(x * x, axis=-1, keepdims=True) + EPS) * g


def _layer_norm(x, g, b):
    mu = jnp.mean(x, axis=-1, keepdims=True)
    xc = x - mu
    var = jnp.mean(xc * xc, axis=-1, keepdims=True)
    return xc * lax.rsqrt(var + EPS) * g + b


def _mm(a, w):
    return jnp.dot(a.astype(BF16), w, preferred_element_type=F32)


_GELU_C = 0.7978845608028654


def _gelu(x):
    inner = x * (_GELU_C + (_GELU_C * 0.044715) * (x * x))
    return x * (0.5 + 0.5 * jnp.tanh(inner))


def _softplus(z):
    return jnp.maximum(z, 0.0) + jnp.log1p(jnp.exp(-jnp.abs(z)))


def _block_diag(xb, w_ref):
    blk = xb.shape[-1] // N_RG_BLOCKS
    return jnp.concatenate(
        [jnp.dot(xb[:, n * blk:(n + 1) * blk], w_ref[n], preferred_element_type=F32)
         for n in range(N_RG_BLOCKS)], axis=-1)


def _lru_decay_input(rec, ga, gx, lam):
    log_a = -RG_C * jax.nn.sigmoid(ga) * _softplus(-lam)
    a = jnp.exp(log_a)
    u = jnp.sqrt(1.0 - a * a) * jax.nn.sigmoid(gx) * rec
    return a, u


def _lru_gates(rec, wa_ref, ba, wx_ref, bx, lam):
    rb = rec.astype(BF16)
    return _lru_decay_input(rec, _block_diag(rb, wa_ref) + ba, _block_diag(rb, wx_ref) + bx, lam)


def _glu(ab, d):
    return ab[:, :d] * jax.nn.sigmoid(ab[:, d:])


def _cast_blocks(srcs, dsts):
    for src, dst in zip(srcs, dsts):
        dst[...] = src[...].astype(BF16)


def _p_memkv_kernel(m_ref, g_ref, wkv_ref, *rest, n_cast, heads):
    cast_in = rest[:n_cast]
    k_ref, v_ref, kt_ref, vb_ref = rest[n_cast:n_cast + 4]
    cast_out = rest[n_cast + 4:2 * n_cast + 4]
    wkv_s, = rest[2 * n_cast + 4:]
    _cast_blocks(cast_in, cast_out)

    @pl.when(pl.program_id(1) == 0)
    def _():
        wkv_s[...] = wkv_ref[...].astype(BF16)

    d = m_ref.shape[-1]
    hn = _rms(m_ref[...], g_ref[...]).astype(BF16)
    k = jnp.dot(hn, wkv_s[:, :d], preferred_element_type=F32)
    v = jnp.dot(hn, wkv_s[:, d:], preferred_element_type=F32)
    m = k.shape[0]
    packed_rows = k_ref.shape[0] // m
    dh = d // heads
    for r in range(packed_rows):
        lt, h = divmod(r, heads)
        cs = slice(h * dh + lt * LANES, h * dh + (lt + 1) * LANES)
        k_ref[pl.ds(r, m, stride=packed_rows), :] = k[:, cs]
        v_ref[pl.ds(r, m, stride=packed_rows), :] = v[:, cs]
    kt_ref[...] = k.T.astype(BF16)
    vb_ref[...] = v.astype(BF16)


def _p_lru_kernel(x_ref, nm_ref, win_ref, cw_ref, cb_ref, wa_ref, ba_ref, wx_ref, bx_ref,
                  lam_ref, wout_ref, *rest, layer, j, n_cast):
    cast_in = rest[:n_cast]
    xo_ref, ho_ref, co_ref = rest[n_cast:n_cast + 3]
    cast_out = rest[n_cast + 3:2 * n_cast + 3]
    rec_t, y_t, h_s = rest[2 * n_cast + 3:]
    _cast_blocks(cast_in, cast_out)

    nb, tr, d = x_ref.shape
    rows = nb * tr
    kw = cw_ref.shape[0]
    hist = kw - 1
    hrows = hist * nb
    nl = d // LANES
    tc = pl.program_id(0)

    @pl.when(tc == 0)
    def _():
        rec_t[:, 0:hrows, :] = jnp.zeros((nl, hrows, LANES), F32)
        h_s[...] = jnp.zeros_like(h_s)

    x = x_ref[...].reshape(rows, d)
    hn = _rms(x, nm_ref[layer:layer + 1, :]).astype(BF16)
    cblk = d // N_RG_BLOCKS
    lanes_per_blk = cblk // LANES

    def rec_proj(n):
        lo = d + n * cblk
        return jnp.dot(hn, win_ref[:, lo:lo + cblk], preferred_element_type=F32)

    out = None
    nxt = rec_proj(0)
    for n in range(N_RG_BLOCKS):
        rec = nxt
        gate = jnp.dot(hn, win_ref[:, n * cblk:(n + 1) * cblk], preferred_element_type=F32)
        if n + 1 < N_RG_BLOCKS:
            nxt = rec_proj(n + 1)
        cs = slice(n * cblk, (n + 1) * cblk)
        slabs = []
        for lt in range(lanes_per_blk):
            l = n * lanes_per_blk + lt
            ls = slice(l * LANES, (l + 1) * LANES)
            for b in range(nb):
                rec_t[l, pl.ds(hrows + b, tr, stride=nb), :] = (
                    rec[b * tr:(b + 1) * tr, lt * LANES:(lt + 1) * LANES])
            acc = cb_ref[j:j + 1, ls] + cw_ref[hist:kw, ls] * rec_t[l, hrows:hrows + rows, :]
            for k in range(hist):
                acc = acc + cw_ref[k:k + 1, ls] * rec_t[l, k * nb:k * nb + rows, :]
            slabs.append(acc)
        conv = jnp.concatenate(slabs, axis=-1)
        cb16 = conv.astype(BF16)
        ga = jnp.dot(cb16, wa_ref[n], preferred_element_type=F32) + ba_ref[j:j + 1, cs]
        gx = jnp.dot(cb16, wx_ref[n], preferred_element_type=F32) + bx_ref[j:j + 1, cs]
        a, u = _lru_decay_input(conv, ga, gx, lam_ref[j:j + 1, cs])

        h = h_s[:, cs]
        for t in range(tr):
            h = a[t * nb:(t + 1) * nb, :] * h + u[t * nb:(t + 1) * nb, :]
            for lt in range(lanes_per_blk):
                y_t[n * lanes_per_blk + lt, t * nb:(t + 1) * nb, :] = (
                    h[:, lt * LANES:(lt + 1) * LANES])
        h_s[:, cs] = h
        ho_ref[:, cs] = h

        y = jnp.concatenate(
            [jnp.concatenate([y_t[n * lanes_per_blk + lt, pl.ds(b, tr, stride=nb), :]
                              for lt in range(lanes_per_blk)], axis=-1)
             for b in range(nb)], axis=0)
        part = jnp.dot((_gelu(gate) * y).astype(BF16), wout_ref[cs, :],
                       preferred_element_type=F32)
        out = part if out is None else out + part
    tail = rec_t[:, rows:rows + hrows, :]
    co_ref[...] = jnp.concatenate([tail[l] for l in range(nl)], axis=-1)
    rec_t[:, 0:hrows, :] = tail
    xo_ref[...] = (x + out).reshape(nb, tr, d)


def _p_cmod_kernel(x_ref, nm_ref, pw1_ref, b1_ref, dw_ref, dwb_ref, lng_ref, lnb_ref, pw2_ref,
                   xo_ref, so_ref, g_t, c_t, *, layer, j):
    nb, tr, d = x_ref.shape
    rows = nb * tr
    kw = dw_ref.shape[0]
    hist = kw - 1
    hrows = hist * nb
    lanes_per_col = CMOD_COL_TILE // LANES
    tc = pl.program_id(0)

    @pl.when(tc == 0)
    def _():
        g_t[:, 0:hrows, :] = jnp.zeros((d // LANES, hrows, LANES), F32)

    x = x_ref[...].reshape(rows, d)
    hn = _rms(x, nm_ref[layer:layer + 1, :]).astype(BF16)
    def pw1(c):
        lo = c * CMOD_COL_TILE
        cs = slice(lo, lo + CMOD_COL_TILE)
        gs = slice(d + lo, d + lo + CMOD_COL_TILE)
        return (jnp.dot(hn, pw1_ref[:, cs], preferred_element_type=F32) + b1_ref[j:j + 1, cs],
                jnp.dot(hn, pw1_ref[:, gs], preferred_element_type=F32) + b1_ref[j:j + 1, gs])

    n_col = d // CMOD_COL_TILE
    nxt = pw1(0)
    for c in range(n_col):
        a, bg = nxt
        if c + 1 < n_col:
            nxt = pw1(c + 1)
        g = a * jax.nn.sigmoid(bg)
        for lt in range(lanes_per_col):
            l = c * lanes_per_col + lt
            ls = slice(l * LANES, (l + 1) * LANES)
            for b in range(nb):
                g_t[l, pl.ds(hrows + b, tr, stride=nb), :] = (
                    g[b * tr:(b + 1) * tr, lt * LANES:(lt + 1) * LANES])
            for rb in range(rows // CMOD_ROW_TILE):
                r0 = hrows + rb * CMOD_ROW_TILE
                acc = dwb_ref[j:j + 1, ls] + dw_ref[hist:kw, ls] * g_t[l, r0:r0 + CMOD_ROW_TILE, :]
                for s in range(1, kw):
                    acc = acc + (dw_ref[hist - s:kw - s, ls]
                                 * g_t[l, r0 - s * nb:r0 - s * nb + CMOD_ROW_TILE, :])
                c_t[l, rb * CMOD_ROW_TILE:(rb + 1) * CMOD_ROW_TILE, :] = acc
    tail = g_t[:, rows:rows + hrows, :]
    so_ref[...] = jnp.concatenate([tail[l] for l in range(d // LANES)], axis=-1)
    g_t[:, 0:hrows, :] = tail

    conv = jnp.concatenate(
        [jnp.concatenate([c_t[l, pl.ds(b, tr, stride=nb), :] for l in range(d // LANES)], axis=-1)
         for b in range(nb)], axis=0)
    c = jax.nn.silu(_layer_norm(conv, lng_ref[j:j + 1, :], lnb_ref[j:j + 1, :]))
    xo_ref[...] = (x + _mm(c, pw2_ref[...])).reshape(nb, tr, d)


def _p_xattn_kernel(x_ref, nm_ref, wq_ref, kt_ref, v_ref, wo_ref, xo_ref, *, layer):
    tt, d = x_ref.shape
    dh = d // XA_HEADS
    x = x_ref[...]
    q = _mm(_rms(x, nm_ref[layer:layer + 1, :]), wq_ref[...]).astype(BF16)
    cols = [slice(h * dh, (h + 1) * dh) for h in range(XA_HEADS)]
    scores = [jnp.dot(q[:, cs], kt_ref[cs, :], preferred_element_type=F32) * (dh ** -0.5)
              for cs in cols]
    heads = []
    for s, cs in zip(scores, cols):
        e = jnp.exp(s - jnp.max(s, axis=-1, keepdims=True))
        p = e / jnp.sum(e, axis=-1, keepdims=True)
        heads.append(jnp.dot(p.astype(BF16), v_ref[:, cs], preferred_element_type=F32))
    o = jnp.concatenate(heads, axis=-1)
    xo_ref[...] = x + _mm(o, wo_ref[...])


def _sample_attend(q_ref, k_ref, v_ref, o_ref, seqs, heads):
    rows, lanes = q_ref.shape[-2:]
    dh = lanes * (rows // heads)
    for b in seqs:
        q = q_ref[b] * (dh ** -0.5)
        prod = k_ref[b] * q[None]
        shift = heads
        while shift < rows:
            prod = prod + pltpu.roll(prod, shift, axis=1)
            shift *= 2
        s = jnp.sum(prod, axis=-1, keepdims=True)
        e = jnp.exp(s - jnp.max(s, axis=0, keepdims=True))
        o_ref[b] = jnp.sum(e * v_ref[b], axis=0) / jnp.sum(e, axis=0)


def _p_ffn_kernel(x_ref, nm_ref, wup_ref, cw_ref, cb_ref, wdn_ref, nf_ref, q_ref, k_ref, v_ref,
                  xo_ref, so_ref, ao_ref, buf, *, layer, final):
    tt, d = x_ref.shape
    dff = wdn_ref.shape[0]
    kw = cw_ref.shape[0]
    hist = kw - 1
    tc = pl.program_id(1)

    @pl.when(tc == 0)
    def _():
        buf[0:SUBLANES, :] = jnp.zeros((SUBLANES, dff), F32)

    x = x_ref[...]
    hn = _rms(x, nm_ref[layer:layer + 1, :]).astype(BF16)
    def up(c):
        lo = c * FFN_COL_TILE
        return (jnp.dot(hn, wup_ref[:, lo:lo + FFN_COL_TILE], preferred_element_type=F32),
                jnp.dot(hn, wup_ref[:, dff + lo:dff + lo + FFN_COL_TILE],
                        preferred_element_type=F32))

    n_col = dff // FFN_COL_TILE
    n_seq = q_ref.shape[0]
    acc = None
    nxt = up(0)
    for c in range(n_col):
        cs = slice(c * FFN_COL_TILE, (c + 1) * FFN_COL_TILE)
        g, u = nxt
        if c + 1 < n_col:
            nxt = up(c + 1)
        _sample_attend(q_ref, k_ref, v_ref, ao_ref,
                       range(c * n_seq // n_col, (c + 1) * n_seq // n_col), XA_HEADS)
        buf[SUBLANES:SUBLANES + tt, cs] = g
        conv = cb_ref[layer:layer + 1, cs] + cw_ref[hist:kw, cs] * g
        for k in range(hist):
            off = SUBLANES - hist + k
            conv = conv + cw_ref[k:k + 1, cs] * buf[off:off + tt, cs]
        act = (_gelu(conv) * u).astype(BF16)
        part = jnp.dot(act, wdn_ref[cs, :], preferred_element_type=F32)
        acc = part if acc is None else acc + part
    tail = buf[SUBLANES + tt - hist:SUBLANES + tt, :]
    so_ref[...] = tail
    buf[SUBLANES - hist:SUBLANES, :] = tail

    xn = x + acc
    if final:
        xn = _rms(xn, nf_ref[...])
    xo_ref[...] = xn


def _s_lru_kernel(x_ref, h0_ref, cst_ref, nm_ref, win_ref, cw_ref, cb_ref, wa_ref, ba_ref,
                  wx_ref, bx_ref, lam_ref, wout_ref, nxa_ref, wq_ref,
                  xo_ref, q_ref, ho_ref, co_ref, *, layer, j):
    d = x_ref.shape[-1]
    kw = cw_ref.shape[0]
    hist = kw - 1
    x = x_ref[...]
    hn = _rms(x, nm_ref[layer:layer + 1, :]).astype(BF16)
    gate = jnp.dot(hn, win_ref[:, :d], preferred_element_type=F32)
    rec = jnp.dot(hn, win_ref[:, d:], preferred_element_type=F32)

    conv = cb_ref[j:j + 1, :] + cw_ref[hist:kw, :] * rec
    for k in range(hist):
        conv = conv + cw_ref[k:k + 1, :] * cst_ref[k]
    for k in range(hist - 1):
        co_ref[k] = cst_ref[k + 1]
    co_ref[hist - 1] = rec

    a, u = _lru_gates(conv, wa_ref, ba_ref[j:j + 1, :], wx_ref, bx_ref[j:j + 1, :],
                      lam_ref[j:j + 1, :])
    h = a * h0_ref[...] + u
    ho_ref[...] = h
    x1 = x + _mm(_gelu(gate) * h, wout_ref[...])
    xo_ref[...] = x1
    q_ref[...] = _mm(_rms(x1, nxa_ref[layer:layer + 1, :]), wq_ref[...])


def _s_ffn_kernel(x_ref, o_ref, wo_ref, nm_ref, wup_ref, cw_ref, cb_ref, wdn_ref, st_ref, nf_ref,
                  xo_ref, so_ref, *, layer, final):
    dff = wdn_ref.shape[0]
    kw = cw_ref.shape[0]
    hist = kw - 1
    x = x_ref[...] + _mm(o_ref[...], wo_ref[...])
    hn = _rms(x, nm_ref[layer:layer + 1, :]).astype(BF16)
    g = jnp.dot(hn, wup_ref[:, :dff], preferred_element_type=F32)
    u = jnp.dot(hn, wup_ref[:, dff:], preferred_element_type=F32)
    conv = cb_ref[layer:layer + 1, :] + cw_ref[hist:kw, :] * g
    for k in range(hist):
        conv = conv + cw_ref[k:k + 1, :] * st_ref[:, k, :]
    for k in range(hist - 1):
        so_ref[:, k, :] = st_ref[:, k + 1, :]
    so_ref[:, hist - 1, :] = g
    xn = x + _mm(_gelu(conv) * u, wdn_ref[...])
    if final:
        xn = _rms(xn, nf_ref[...])
    xo_ref[...] = xn


def _s_cmod_pre_kernel(x_ref, nm_ref, pw1_ref, b1_ref, g_ref, *, layer, j):
    d = x_ref.shape[-1]
    hn = _rms(x_ref[...], nm_ref[layer:layer + 1, :])
    g_ref[...] = _glu(_mm(hn, pw1_ref[...]) + b1_ref[j:j + 1, :], d)


def _s_cmod_conv_kernel(st_ref, g_ref, dw_ref, dwb_ref, lng_ref, lnb_ref, c_ref, so_ref, *, j):
    hist = st_ref.shape[0]
    kw = hist + 1
    g = g_ref[...]
    c = dwb_ref[j:j + 1, :] + dw_ref[hist:kw, :] * g
    for k in range(hist):
        c = c + dw_ref[k:k + 1, :] * st_ref[k]
    for k in range(hist - 1):
        so_ref[k] = st_ref[k + 1]
    so_ref[hist - 1] = g
    c_ref[...] = jax.nn.silu(_layer_norm(c, lng_ref[j:j + 1, :], lnb_ref[j:j + 1, :]))


def _s_cmod_post_kernel(x_ref, c_ref, pw2_ref, nxa_ref, wq_ref, xo_ref, q_ref, *, layer):
    x1 = x_ref[...] + _mm(c_ref[...], pw2_ref[...])
    xo_ref[...] = x1
    q_ref[...] = _mm(_rms(x1, nxa_ref[layer:layer + 1, :]), wq_ref[...])


def _params(n_grid):
    return pltpu.CompilerParams(dimension_semantics=("arbitrary",) * n_grid,
                                vmem_limit_bytes=VMEM_LIMIT)


def _resident(shape, lead=None):
    if lead is None:
        nd = len(shape)
        return pl.BlockSpec(shape, lambda *_: (0,) * nd, pipeline_mode=pl.Buffered(1))
    nd = len(shape) - 1
    return pl.BlockSpec((None,) + tuple(shape[1:]), lambda *_: (lead,) + (0,) * nd,
                        pipeline_mode=pl.Buffered(1))


_whole = _resident


def _whole_out(shape):
    nd = len(shape)
    return pl.BlockSpec(shape, lambda *_: (0,) * nd)


def _sds(shape, dtype=F32):
    return jax.ShapeDtypeStruct(shape, dtype)


def _flat_cast_blocks(to_cast, steps, step_index):
    flat = [a.reshape(-1, a.shape[-1]) for a in to_cast]
    for a in flat:
        assert a.shape[0] % (steps * BF16_SUBLANES) == 0
    specs = [pl.BlockSpec((a.shape[0] // steps, a.shape[1]), lambda *g: (step_index(*g), 0))
             for a in flat]
    return flat, specs


def _prompt_memkv(mem, mem_norm3, wkv, heads, to_cast):
    depth = wkv.shape[0]
    b, m, d = mem.shape
    assert d % LANES == 0
    packed = d // LANES
    flat, cast_specs = _flat_cast_blocks(to_cast, depth * b, lambda i, n: i * b + n)
    outs = pl.pallas_call(
        functools.partial(_p_memkv_kernel, n_cast=len(flat), heads=heads),
        grid=(depth, b),
        in_specs=[pl.BlockSpec((None, m, d), lambda i, n: (n, 0, 0)),
                  pl.BlockSpec((None, 1, d), lambda i, n: (i, 0, 0)),
                  pl.BlockSpec((None, d, 2 * d), lambda i, n: (i, 0, 0))] + cast_specs,
        out_specs=[pl.BlockSpec((None, None, m * packed, LANES), lambda i, n: (i, n, 0, 0)),
                   pl.BlockSpec((None, None, m * packed, LANES), lambda i, n: (i, n, 0, 0)),
                   pl.BlockSpec((None, None, d, m), lambda i, n: (i, n, 0, 0)),
                   pl.BlockSpec((None, None, m, d), lambda i, n: (i, n, 0, 0))] + cast_specs,
        out_shape=[_sds((depth, b, m * packed, LANES)), _sds((depth, b, m * packed, LANES)),
                   _sds((depth, b, d, m), BF16), _sds((depth, b, m, d), BF16)]
                  + [_sds(a.shape, BF16) for a in flat],
        scratch_shapes=[pltpu.VMEM((d, 2 * d), BF16)],
        compiler_params=_params(2),
        name="prompt_memkv",
    )(mem, mem_norm3, wkv, *flat)
    k5, v5 = [_unpack_heads(o.reshape(depth, b, m, packed, LANES), heads) for o in outs[:2]]
    return (k5, v5, outs[2], outs[3]), [o.reshape(a.shape) for o, a in zip(outs[4:], to_cast)]


def _x_spec(tt, d):
    return pl.BlockSpec((None, tt, d), lambda n, t: (n, t, 0))


def _state_spec(rows, cols):
    return pl.BlockSpec((None, rows, cols), lambda n, t: (n, 0, 0))


def _xt_spec(b, tr, d):
    return pl.BlockSpec((b, tr, d), lambda t: (0, t, 0))


def _steps_to_sequences(s, b):
    hrows, d = s.shape
    return s.reshape(hrows // b, b, d).transpose(1, 0, 2)


def _prompt_lru(x, layer, j, w, to_cast):
    b, t, d = x.shape
    tr = LRU_TIME_ROWS
    hist = w['lru_conv_w'].shape[1] - 1
    assert b == SUBLANES and tr % SUBLANES == 0 and tr >= hist and d % LANES == 0
    nl, hrows = d // LANES, hist * b
    steps = t // tr
    flat, cast_specs = _flat_cast_blocks(to_cast, steps, lambda s: s)
    ins = [(x, _xt_spec(b, tr, d)),
           (w['norm_mix'], _resident(w['norm_mix'].shape)),
           (w['lru_w_in'], _resident(w['lru_w_in'].shape, j)),
           (w['lru_conv_w'], _resident(w['lru_conv_w'].shape, j)),
           (w['lru_conv_b'], _resident(w['lru_conv_b'].shape)),
           (w['lru_wa'], _resident(w['lru_wa'].shape, j)),
           (w['lru_ba'], _resident(w['lru_ba'].shape)),
           (w['lru_wx'], _resident(w['lru_wx'].shape, j)),
           (w['lru_bx'], _resident(w['lru_bx'].shape)),
           (w['lru_lambda'], _resident(w['lru_lambda'].shape)),
           (w['lru_w_out'], _resident(w['lru_w_out'].shape, j))]
    ins += list(zip(flat, cast_specs))
    outs = pl.pallas_call(
        functools.partial(_p_lru_kernel, layer=layer, j=j, n_cast=len(flat)),
        grid=(steps,),
        in_specs=[s for _, s in ins],
        out_specs=[_xt_spec(b, tr, d), _whole_out((b, d)), _whole_out((hrows, d))] + cast_specs,
        out_shape=([_sds((b, t, d)), _sds((b, d)), _sds((hrows, d))]
                   + [_sds(a.shape, BF16) for a in flat]),
        scratch_shapes=[pltpu.VMEM((nl, hrows + b * tr, LANES), F32),
                        pltpu.VMEM((nl, b * tr, LANES), F32),
                        pltpu.VMEM((b, d), F32)],
        compiler_params=_params(1),
        name="prompt_lru",
    )(*[a for a, _ in ins])
    casted = [o.reshape(a.shape) for o, a in zip(outs[3:], to_cast)]
    return outs[0], outs[1], outs[2], casted


def _prompt_cmod(x, layer, j, w):
    b, t, d = x.shape
    tr = CMOD_TIME_ROWS
    hist = w['cm_dw_w'].shape[1] - 1
    assert b == SUBLANES and tr % SUBLANES == 0 and tr >= hist and d % CMOD_COL_TILE == 0
    nl, hrows = d // LANES, hist * b
    ins = [(x, _xt_spec(b, tr, d)),
           (w['norm_mix'], _resident(w['norm_mix'].shape)),
           (w['cm_w_pw1'], _resident(w['cm_w_pw1'].shape, j)),
           (w['cm_b_pw1'], _resident(w['cm_b_pw1'].shape)),
           (w['cm_dw_w'], _resident(w['cm_dw_w'].shape, j)),
           (w['cm_dw_b'], _resident(w['cm_dw_b'].shape)),
           (w['cm_ln_g'], _resident(w['cm_ln_g'].shape)),
           (w['cm_ln_b'], _resident(w['cm_ln_b'].shape)),
           (w['cm_w_pw2'], _resident(w['cm_w_pw2'].shape, j))]
    return pl.pallas_call(
        functools.partial(_p_cmod_kernel, layer=layer, j=j),
        grid=(t // tr,),
        in_specs=[s for _, s in ins],
        out_specs=[_xt_spec(b, tr, d), _whole_out((hrows, d))],
        out_shape=[_sds((b, t, d)), _sds((hrows, d))],
        scratch_shapes=[pltpu.VMEM((nl, hrows + b * tr, LANES), F32),
                        pltpu.VMEM((nl, b * tr, LANES), F32)],
        compiler_params=_params(1),
        name="prompt_cmod",
    )(*[a for a, _ in ins])


def _prompt_xattn(x, layer, kt, vb, w):
    b, t, d = x.shape
    tt = XATTN_T_TILE
    m = vb.shape[2]
    ins = [(x, _x_spec(tt, d)),
           (w['norm_xa'], _resident(w['norm_xa'].shape)),
           (w['xa_w_q'], _resident(w['xa_w_q'].shape, layer)),
           (kt, pl.BlockSpec((None, None, d, m), lambda n, t_: (layer, n, 0, 0))),
           (vb, pl.BlockSpec((None, None, m, d), lambda n, t_: (layer, n, 0, 0))),
           (w['xa_w_o'], _resident(w['xa_w_o'].shape, layer))]
    return pl.pallas_call(
        functools.partial(_p_xattn_kernel, layer=layer),
        grid=(b, t // tt),
        in_specs=[s for _, s in ins],
        out_specs=_x_spec(tt, d),
        out_shape=_sds((b, t, d)),
        compiler_params=_params(2),
        name="prompt_xattn",
    )(*[a for a, _ in ins])


def _prompt_ffn_sample_attn(x, layer, final, q, cache_k, cache_v, w):
    b, t, d = x.shape
    tt = FFN_T_TILE
    n_t = t // tt
    dff = w['ffn_w_down'].shape[1]
    hist = w['ffn_conv_w'].shape[1] - 1
    r = q.shape[0]
    _, _, m, nh, dh = cache_k.shape
    assert r % (b * n_t) == 0 and dh % LANES == 0 and nh * (dh // LANES) == SUBLANES
    nb = r // (b * n_t)
    kv_spec = pl.BlockSpec((None, nb, m, SUBLANES, LANES),
                           lambda n, s: (layer, n * n_t + s, 0, 0, 0))
    row_spec = pl.BlockSpec((nb, SUBLANES, LANES), lambda n, s: (n * n_t + s, 0, 0))
    ins = [(x, _x_spec(tt, d)),
           (w['norm_ffn'], _resident(w['norm_ffn'].shape)),
           (w['ffn_w_up'], _resident(w['ffn_w_up'].shape, layer)),
           (w['ffn_conv_w'], _resident(w['ffn_conv_w'].shape, layer)),
           (w['ffn_conv_b'], _resident(w['ffn_conv_b'].shape)),
           (w['ffn_w_down'], _resident(w['ffn_w_down'].shape, layer)),
           (w['norm_final'], _resident(w['norm_final'].shape)),
           (_pack_heads(q.reshape(r, nh, dh)), row_spec),
           (_pack_heads(cache_k), kv_spec), (_pack_heads(cache_v), kv_spec)]
    xo, fbuf, o = pl.pallas_call(
        functools.partial(_p_ffn_kernel, layer=layer, final=final),
        grid=(b, n_t),
        in_specs=[s for _, s in ins],
        out_specs=[_x_spec(tt, d), _state_spec(hist, dff), row_spec],
        out_shape=[_sds((b, t, d)), _sds((b, hist, dff)), _sds((r, SUBLANES, LANES))],
        scratch_shapes=[pltpu.VMEM((SUBLANES + tt, dff), F32)],
        compiler_params=pltpu.CompilerParams(dimension_semantics=("arbitrary", "arbitrary"),
                                             vmem_limit_bytes=FFN_VMEM_LIMIT),
        name="prompt_ffn_sample_attn",
    )(*[a for a, _ in ins])
    return xo, fbuf, _unpack_heads(o, nh).reshape(r, d)


def _pack_heads(a):
    *lead, nh, dh = a.shape
    a = a.reshape(*lead, nh, dh // LANES, LANES).swapaxes(-3, -2)
    return a.reshape(*lead, (dh // LANES) * nh, LANES)


def _unpack_heads(a, nh):
    *lead, rows, lanes = a.shape
    a = a.reshape(*lead, rows // nh, nh, lanes).swapaxes(-3, -2)
    return a.reshape(*lead, nh, (rows // nh) * lanes)


def _sample_lru(x, h0, cst, layer, j, w):
    r, d = x.shape
    ins = [(x, _whole(x.shape)), (h0, _whole(h0.shape, j)), (cst, _whole(cst.shape, j)),
           (w['norm_mix'], _whole(w['norm_mix'].shape)),
           (w['lru_w_in'], _whole(w['lru_w_in'].shape, j)),
           (w['lru_conv_w'], _whole(w['lru_conv_w'].shape, j)),
           (w['lru_conv_b'], _whole(w['lru_conv_b'].shape)),
           (w['lru_wa'], _whole(w['lru_wa'].shape, j)),
           (w['lru_ba'], _whole(w['lru_ba'].shape)),
           (w['lru_wx'], _whole(w['lru_wx'].shape, j)),
           (w['lru_bx'], _whole(w['lru_bx'].shape)),
           (w['lru_lambda'], _whole(w['lru_lambda'].shape)),
           (w['lru_w_out'], _whole(w['lru_w_out'].shape, j)),
           (w['norm_xa'], _whole(w['norm_xa'].shape)),
           (w['xa_w_q'], _whole(w['xa_w_q'].shape, layer))]
    return pl.pallas_call(
        functools.partial(_s_lru_kernel, layer=layer, j=j),
        in_specs=[s for _, s in ins],
        out_specs=[_whole_out((r, d)), _whole_out((r, d)), _whole_out((r, d)),
                   _whole_out(cst.shape[1:])],
        out_shape=[_sds((r, d)), _sds((r, d)), _sds((r, d)), _sds(cst.shape[1:])],
        grid=(1,),
        compiler_params=_params(1),
        name="sample_lru",
    )(*[a for a, _ in ins])


def _sample_ffn(x, o, st, layer, final, w):
    r, d = x.shape
    ins = [(x, _whole(x.shape)), (o, _whole(o.shape)),
           (w['xa_w_o'], _whole(w['xa_w_o'].shape, layer)),
           (w['norm_ffn'], _whole(w['norm_ffn'].shape)),
           (w['ffn_w_up'], _whole(w['ffn_w_up'].shape, layer)),
           (w['ffn_conv_w'], _whole(w['ffn_conv_w'].shape, layer)),
           (w['ffn_conv_b'], _whole(w['ffn_conv_b'].shape)),
           (w['ffn_w_down'], _whole(w['ffn_w_down'].shape, layer)),
           (st, _whole(st.shape, layer)),
           (w['norm_final'], _whole(w['norm_final'].shape))]
    return pl.pallas_call(
        functools.partial(_s_ffn_kernel, layer=layer, final=final),
        in_specs=[s for _, s in ins],
        out_specs=[_whole_out((r, d)), _whole_out(st.shape[1:])],
        out_shape=[_sds((r, d)), _sds(st.shape[1:])],
        grid=(1,),
        compiler_params=_params(1),
        name="sample_ffn",
    )(*[a for a, _ in ins])


def _sample_cmod(x, st, layer, j, w):
    r, d = x.shape
    ins = [(x, _whole(x.shape)),
           (w['norm_mix'], _whole(w['norm_mix'].shape)),
           (w['cm_w_pw1'], _whole(w['cm_w_pw1'].shape, j)),
           (w['cm_b_pw1'], _whole(w['cm_b_pw1'].shape))]
    g = pl.pallas_call(
        functools.partial(_s_cmod_pre_kernel, layer=layer, j=j),
        in_specs=[s for _, s in ins],
        out_specs=_whole_out((r, d)),
        out_shape=_sds((r, d)),
        grid=(1,),
        compiler_params=_params(1),
        name="sample_cmod_pre",
    )(*[a for a, _ in ins])

    rows = SAMPLE_CONV_ROWS
    hist = st.shape[1]
    ins = [(st, pl.BlockSpec((None, hist, rows, d), lambda n: (j, 0, n, 0))),
           (g, pl.BlockSpec((rows, d), lambda n: (n, 0))),
           (w['cm_dw_w'], _resident(w['cm_dw_w'].shape, j)),
           (w['cm_dw_b'], _resident(w['cm_dw_b'].shape)),
           (w['cm_ln_g'], _resident(w['cm_ln_g'].shape)),
           (w['cm_ln_b'], _resident(w['cm_ln_b'].shape))]
    c, new_st = pl.pallas_call(
        functools.partial(_s_cmod_conv_kernel, j=j),
        grid=(r // rows,),
        in_specs=[s for _, s in ins],
        out_specs=[pl.BlockSpec((rows, d), lambda n: (n, 0)),
                   pl.BlockSpec((hist, rows, d), lambda n: (0, n, 0))],
        out_shape=[_sds((r, d)), _sds((hist, r, d))],
        compiler_params=_params(1),
        name="sample_cmod_conv",
    )(*[a for a, _ in ins])

    ins = [(x, _whole(x.shape)), (c, _whole(c.shape)),
           (w['cm_w_pw2'], _whole(w['cm_w_pw2'].shape, j)),
           (w['norm_xa'], _whole(w['norm_xa'].shape)),
           (w['xa_w_q'], _whole(w['xa_w_q'].shape, layer))]
    x1, q = pl.pallas_call(
        functools.partial(_s_cmod_post_kernel, layer=layer),
        in_specs=[s for _, s in ins],
        out_specs=[_whole_out((r, d)), _whole_out((r, d))],
        out_shape=[_sds((r, d)), _sds((r, d))],
        grid=(1,),
        compiler_params=_params(1),
        name="sample_cmod_post",
    )(*[a for a, _ in ins])
    return x1, q, new_st


_MIXER0_MATMUL_WEIGHTS = ('lru_w_in', 'lru_wa', 'lru_wx', 'lru_w_out')
_LATE_MATMUL_WEIGHTS = ('cm_w_pw1', 'cm_w_pw2', 'xa_w_q', 'xa_w_o', 'ffn_w_up', 'ffn_w_down')


def kernel(x_prompt, x_sample, state_lru_h, state_lru_conv, state_cmod_conv, state_ffn_conv, cache_mem_k, cache_mem_v, mem_prompt, norm_mix, norm_xa, norm_ffn, norm_final, lru_w_in, lru_conv_w, lru_conv_b, lru_wa, lru_ba, lru_wx, lru_bx, lru_lambda, lru_w_out, cm_w_pw1, cm_b_pw1, cm_dw_w, cm_dw_b, cm_ln_g, cm_ln_b, cm_w_pw2, mem_norm, xa_w_q, xa_w_kv, xa_w_o, ffn_w_up, ffn_conv_w, ffn_conv_b, ffn_w_down):
    w = dict(norm_mix=norm_mix, norm_xa=norm_xa, norm_ffn=norm_ffn,
             norm_final=norm_final.reshape(1, -1),
             lru_w_in=lru_w_in, lru_conv_w=lru_conv_w, lru_conv_b=lru_conv_b, lru_wa=lru_wa,
             lru_ba=lru_ba, lru_wx=lru_wx, lru_bx=lru_bx, lru_lambda=lru_lambda,
             lru_w_out=lru_w_out, cm_w_pw1=cm_w_pw1, cm_b_pw1=cm_b_pw1, cm_dw_w=cm_dw_w,
             cm_dw_b=cm_dw_b, cm_ln_g=cm_ln_g, cm_ln_b=cm_ln_b, cm_w_pw2=cm_w_pw2,
             xa_w_q=xa_w_q, xa_w_kv=xa_w_kv, xa_w_o=xa_w_o,
             ffn_w_up=ffn_w_up, ffn_conv_w=ffn_conv_w, ffn_conv_b=ffn_conv_b,
             ffn_w_down=ffn_w_down)

    depth = norm_mix.shape[0]
    bsz, _, d = x_prompt.shape
    n_mem = mem_prompt.shape[1]
    heads = cache_mem_k.shape[3]

    (k_p, v_p, kt_p, vb_p), casted = _prompt_memkv(
        mem_prompt, mem_norm.reshape(depth, 1, d), w['xa_w_kv'], heads,
        [w[n] for n in _MIXER0_MATMUL_WEIGHTS])
    w.update(zip(_MIXER0_MATMUL_WEIGHTS, casted))
    x = x_prompt
    r = x_sample.shape[0]
    xs = x_sample.reshape(r, d)
    lconv = state_lru_conv.transpose(0, 2, 1, 3)
    cconv = state_cmod_conv.transpose(0, 2, 1, 3)
    p_h, p_lconv, p_cconv, p_fconv = [], [], [], []
    s_h, s_lconv, s_cconv, s_fconv = [], [], [], []
    for i in range(depth):
        j = i // 2
        if i % 2 == 0:
            late = [n for n in _LATE_MATMUL_WEIGHTS if w[n].dtype != BF16]
            x, h_last, cbuf, casted = _prompt_lru(x, i, j, w, [w[n] for n in late])
            w.update(zip(late, casted))
            p_h.append(h_last)
            p_lconv.append(_steps_to_sequences(cbuf, bsz))
            xs, q, h_new, c_new = _sample_lru(xs, state_lru_h, lconv, i, j, w)
            s_h.append(h_new)
            s_lconv.append(c_new.transpose(1, 0, 2))
        else:
            x, cbuf = _prompt_cmod(x, i, j, w)
            p_cconv.append(_steps_to_sequences(cbuf, bsz))
            xs, q, c_new = _sample_cmod(xs, cconv, i, j, w)
            s_cconv.append(c_new.transpose(1, 0, 2))
        x = _prompt_xattn(x, i, kt_p, vb_p, w)
        x, fbuf, o = _prompt_ffn_sample_attn(x, i, i == depth - 1, q, cache_mem_k, cache_mem_v, w)
        p_fconv.append(fbuf)
        xs, f_new = _sample_ffn(xs, o, state_ffn_conv, i, i == depth - 1, w)
        s_fconv.append(f_new)
    y_prompt = x
    y_sample = xs.reshape(x_sample.shape)

    return (y_prompt, y_sample,
            jnp.stack(p_h), jnp.stack(p_lconv), jnp.stack(p_cconv), jnp.stack(p_fconv),
            k_p, v_p,
            jnp.stack(s_h), jnp.stack(s_lconv), jnp.stack(s_cconv), jnp.stack(s_fconv))
```

```python
import functools

import jax
import jax.numpy as jnp
from jax import lax
from jax.experimental import pallas as pl
from jax.experimental.pallas import tpu as pltpu

F32 = jnp.float32
BF16 = jnp.bfloat16

EPS = 1e-6
RG_C = 8.0
N_RG_BLOCKS = 4
XA_HEADS = 4
SUBLANES = 8
LANES = 128
BF16_SUBLANES = 16
VMEM_LIMIT = 56 << 20
FFN_VMEM_LIMIT = 60 << 20

LRU_TIME_ROWS = 64
CMOD_TIME_ROWS = 128
XATTN_T_TILE = 1024
FFN_T_TILE = 512
FFN_COL_TILE = 1536
CMOD_COL_TILE = 256
CMOD_ROW_TILE = 256
SAMPLE_CONV_ROWS = 32


def _rms(x, g):
    return x * lax.rsqrt(jnp.mean(x * x, axis=-1, keepdims=True) + EPS) * g


def _layer_norm(x, g, b):
    mu = jnp.mean(x, axis=-1, keepdims=True)
    xc = x - mu
    var = jnp.mean(xc * xc, axis=-1, keepdims=True)
    return xc * lax.rsqrt(var + EPS) * g + b


def _mm(a, w):
    return jnp.dot(a.astype(BF16), w, preferred_element_type=F32)


_GELU_C = 0.7978845608028654


def _gelu(x):
    inner = x * (_GELU_C + (_GELU_C * 0.044715) * (x * x))
    return x * (0.5 + 0.5 * jnp.tanh(inner))


def _softplus(z):
    return jnp.maximum(z, 0.0) + jnp.log1p(jnp.exp(-jnp.abs(z)))


def _block_diag(xb, w_ref):
    blk = xb.shape[-1] // N_RG_BLOCKS
    return jnp.concatenate(
        [jnp.dot(xb[:, n * blk:(n + 1) * blk], w_ref[n], preferred_element_type=F32)
         for n in range(N_RG_BLOCKS)], axis=-1)


def _lru_decay_input(rec, ga, gx, lam):
    log_a = -RG_C * jax.nn.sigmoid(ga) * _softplus(-lam)
    a = jnp.exp(log_a)
    u = jnp.sqrt(1.0 - a * a) * jax.nn.sigmoid(gx) * rec
    return a, u


def _lru_gates(rec, wa_ref, ba, wx_ref, bx, lam):
    rb = rec.astype(BF16)
    return _lru_decay_input(rec, _block_diag(rb, wa_ref) + ba, _block_diag(rb, wx_ref) + bx, lam)


def _glu(ab, d):
    return ab[:, :d] * jax.nn.sigmoid(ab[:, d:])


def _store_packed_heads(ref, val, heads):
    rows, d = val.shape
    packed = d // LANES
    dh = d // heads
    for r in range(packed):
        lt, h = divmod(r, heads)
        ref[pl.ds(r, rows, stride=packed), :] = val[:, h * dh + lt * LANES:h * dh + (lt + 1) * LANES]


def _load_packed_heads(ref, rows, heads):
    packed = ref.shape[0] // rows
    dh = packed * LANES // heads
    tiles = [None] * packed
    for r in range(packed):
        lt, h = divmod(r, heads)
        tiles[(h * dh) // LANES + lt] = ref[pl.ds(r, rows, stride=packed), :]
    return jnp.concatenate(tiles, axis=-1)


def _cast_blocks(srcs, dsts):
    for src, dst in zip(srcs, dsts):
        dst[...] = src[...].astype(BF16)


def _p_memkv_kernel(m_ref, g_ref, wkv_ref, *rest, n_cast, heads):
    cast_in = rest[:n_cast]
    k_ref, v_ref, kt_ref, vb_ref = rest[n_cast:n_cast + 4]
    cast_out = rest[n_cast + 4:2 * n_cast + 4]
    wkv_s, = rest[2 * n_cast + 4:]
    _cast_blocks(cast_in, cast_out)

    @pl.when(pl.program_id(1) == 0)
    def _():
        wkv_s[...] = wkv_ref[...].astype(BF16)

    d = m_ref.shape[-1]
    hn = _rms(m_ref[...], g_ref[...]).astype(BF16)
    k = jnp.dot(hn, wkv_s[:, :d], preferred_element_type=F32)
    v = jnp.dot(hn, wkv_s[:, d:], preferred_element_type=F32)
    _store_packed_heads(k_ref, k, heads)
    _store_packed_heads(v_ref, v, heads)
    kt_ref[...] = k.T.astype(BF16)
    vb_ref[...] = v.astype(BF16)


def _p_lru_kernel(x_ref, nm_ref, win_ref, cw_ref, cb_ref, wa_ref, ba_ref, wx_ref, bx_ref,
                  lam_ref, wout_ref, *rest, layer, j, n_cast):
    cast_in = rest[:n_cast]
    xo_ref, ho_ref, co_ref = rest[n_cast:n_cast + 3]
    cast_out = rest[n_cast + 3:2 * n_cast + 3]
    rec_t, y_t, h_s = rest[2 * n_cast + 3:]
    _cast_blocks(cast_in, cast_out)

    nb, tr, d = x_ref.shape
    rows = nb * tr
    kw = cw_ref.shape[0]
    hist = kw - 1
    hrows = hist * nb
    nl = d // LANES
    tc = pl.program_id(0)

    @pl.when(tc == 0)
    def _():
        rec_t[:, 0:hrows, :] = jnp.zeros((nl, hrows, LANES), F32)
        h_s[...] = jnp.zeros_like(h_s)

    x = x_ref[...].reshape(rows, d)
    hn = _rms(x, nm_ref[layer:layer + 1, :]).astype(BF16)
    cblk = d // N_RG_BLOCKS
    lanes_per_blk = cblk // LANES

    def rec_proj(n):
        lo = d + n * cblk
        return jnp.dot(hn, win_ref[:, lo:lo + cblk], preferred_element_type=F32)

    out = None
    nxt = rec_proj(0)
    for n in range(N_RG_BLOCKS):
        rec = nxt
        gate = jnp.dot(hn, win_ref[:, n * cblk:(n + 1) * cblk], preferred_element_type=F32)
        if n + 1 < N_RG_BLOCKS:
            nxt = rec_proj(n + 1)
        cs = slice(n * cblk, (n + 1) * cblk)
        slabs = []
        for lt in range(lanes_per_blk):
            l = n * lanes_per_blk + lt
            ls = slice(l * LANES, (l + 1) * LANES)
            for b in range(nb):
                rec_t[l, pl.ds(hrows + b, tr, stride=nb), :] = (
                    rec[b * tr:(b + 1) * tr, lt * LANES:(lt + 1) * LANES])
            acc = cb_ref[j:j + 1, ls] + cw_ref[hist:kw, ls] * rec_t[l, hrows:hrows + rows, :]
            for k in range(hist):
                acc = acc + cw_ref[k:k + 1, ls] * rec_t[l, k * nb:k * nb + rows, :]
            slabs.append(acc)
        conv = jnp.concatenate(slabs, axis=-1)
        cb16 = conv.astype(BF16)
        ga = jnp.dot(cb16, wa_ref[n], preferred_element_type=F32) + ba_ref[j:j + 1, cs]
        gx = jnp.dot(cb16, wx_ref[n], preferred_element_type=F32) + bx_ref[j:j + 1, cs]
        a, u = _lru_decay_input(conv, ga, gx, lam_ref[j:j + 1, cs])

        h = h_s[:, cs]
        for t in range(tr):
            h = a[t * nb:(t + 1) * nb, :] * h + u[t * nb:(t + 1) * nb, :]
            for lt in range(lanes_per_blk):
                y_t[n * lanes_per_blk + lt, t * nb:(t + 1) * nb, :] = (
                    h[:, lt * LANES:(lt + 1) * LANES])
        h_s[:, cs] = h
        ho_ref[:, cs] = h

        y = jnp.concatenate(
            [jnp.concatenate([y_t[n * lanes_per_blk + lt, pl.ds(b, tr, stride=nb), :]
                              for lt in range(lanes_per_blk)], axis=-1)
             for b in range(nb)], axis=0)
        part = jnp.dot((_gelu(gate) * y).astype(BF16), wout_ref[cs, :],
                       preferred_element_type=F32)
        out = part if out is None else out + part
    tail = rec_t[:, rows:rows + hrows, :]
    co_ref[...] = jnp.concatenate([tail[l] for l in range(nl)], axis=-1)
    rec_t[:, 0:hrows, :] = tail
    xo_ref[...] = (x + out).reshape(nb, tr, d)


def _p_cmod_kernel(x_ref, nm_ref, pw1_ref, b1_ref, dw_ref, dwb_ref, lng_ref, lnb_ref, pw2_ref,
                   xo_ref, so_ref, g_t, c_t, *, layer, j):
    nb, tr, d = x_ref.shape
    rows = nb * tr
    kw = dw_ref.shape[0]
    hist = kw - 1
    hrows = hist * nb
    lanes_per_col = CMOD_COL_TILE // LANES
    tc = pl.program_id(0)

    @pl.when(tc == 0)
    def _():
        g_t[:, 0:hrows, :] = jnp.zeros((d // LANES, hrows, LANES), F32)

    x = x_ref[...].reshape(rows, d)
    hn = _rms(x, nm_ref[layer:layer + 1, :]).astype(BF16)
    def pw1(c):
        lo = c * CMOD_COL_TILE
        cs = slice(lo, lo + CMOD_COL_TILE)
        gs = slice(d + lo, d + lo + CMOD_COL_TILE)
        return (jnp.dot(hn, pw1_ref[:, cs], preferred_element_type=F32) + b1_ref[j:j + 1, cs],
                jnp.dot(hn, pw1_ref[:, gs], preferred_element_type=F32) + b1_ref[j:j + 1, gs])

    n_col = d // CMOD_COL_TILE
    nxt = pw1(0)
    for c in range(n_col):
        a, bg = nxt
        if c + 1 < n_col:
            nxt = pw1(c + 1)
        g = a * jax.nn.sigmoid(bg)
        for lt in range(lanes_per_col):
            l = c * lanes_per_col + lt
            ls = slice(l * LANES, (l + 1) * LANES)
            for b in range(nb):
                g_t[l, pl.ds(hrows + b, tr, stride=nb), :] = (
                    g[b * tr:(b + 1) * tr, lt * LANES:(lt + 1) * LANES])
            for rb in range(rows // CMOD_ROW_TILE):
                r0 = hrows + rb * CMOD_ROW_TILE
                acc = dwb_ref[j:j + 1, ls] + dw_ref[hist:kw, ls] * g_t[l, r0:r0 + CMOD_ROW_TILE, :]
                for s in range(1, kw):
                    acc = acc + (dw_ref[hist - s:kw - s, ls]
                                 * g_t[l, r0 - s * nb:r0 - s * nb + CMOD_ROW_TILE, :])
                c_t[l, rb * CMOD_ROW_TILE:(rb + 1) * CMOD_ROW_TILE, :] = acc
    tail = g_t[:, rows:rows + hrows, :]
    so_ref[...] = jnp.concatenate([tail[l] for l in range(d // LANES)], axis=-1)
    g_t[:, 0:hrows, :] = tail

    conv = jnp.concatenate(
        [jnp.concatenate([c_t[l, pl.ds(b, tr, stride=nb), :] for l in range(d // LANES)], axis=-1)
         for b in range(nb)], axis=0)
    c = jax.nn.silu(_layer_norm(conv, lng_ref[j:j + 1, :], lnb_ref[j:j + 1, :]))
    xo_ref[...] = (x + _mm(c, pw2_ref[...])).reshape(nb, tr, d)


def _p_xattn_kernel(x_ref, nm_ref, wq_ref, kt_ref, v_ref, wo_ref, xo_ref, *, layer):
    tt, d = x_ref.shape
    dh = d // XA_HEADS
    x = x_ref[...]
    q = _mm(_rms(x, nm_ref[layer:layer + 1, :]), wq_ref[...]).astype(BF16)
    cols = [slice(h * dh, (h + 1) * dh) for h in range(XA_HEADS)]
    scores = [jnp.dot(q[:, cs], kt_ref[cs, :], preferred_element_type=F32) * (dh ** -0.5)
              for cs in cols]
    heads = []
    for s, cs in zip(scores, cols):
        e = jnp.exp(s - jnp.max(s, axis=-1, keepdims=True))
        p = e / jnp.sum(e, axis=-1, keepdims=True)
        heads.append(jnp.dot(p.astype(BF16), v_ref[:, cs], preferred_element_type=F32))
    o = jnp.concatenate(heads, axis=-1)
    xo_ref[...] = x + _mm(o, wo_ref[...])


def _sample_attend(q_ref, k_ref, v_ref, o_ref, seqs, heads):
    rows, lanes = q_ref.shape[-2:]
    dh = lanes * (rows // heads)
    for b in seqs:
        q = q_ref[b] * (dh ** -0.5)
        prod = k_ref[b] * q[None]
        shift = heads
        while shift < rows:
            prod = prod + pltpu.roll(prod, shift, axis=1)
            shift *= 2
        s = jnp.sum(prod, axis=-1, keepdims=True)
        e = jnp.exp(s - jnp.max(s, axis=0, keepdims=True))
        o_ref[b] = jnp.sum(e * v_ref[b], axis=0) / jnp.sum(e, axis=0)


def _p_ffn_kernel(x_ref, nm_ref, wup_ref, cw_ref, cb_ref, wdn_ref, nf_ref, q_ref, k_ref, v_ref,
                  xo_ref, so_ref, ao_ref, buf, *, layer, final):
    tt, d = x_ref.shape
    dff = wdn_ref.shape[0]
    kw = cw_ref.shape[0]
    hist = kw - 1
    tc = pl.program_id(1)

    @pl.when(tc == 0)
    def _():
        buf[0:SUBLANES, :] = jnp.zeros((SUBLANES, dff), F32)

    x = x_ref[...]
    hn = _rms(x, nm_ref[layer:layer + 1, :]).astype(BF16)
    def up(c):
        lo = c * FFN_COL_TILE
        return (jnp.dot(hn, wup_ref[:, lo:lo + FFN_COL_TILE], preferred_element_type=F32),
                jnp.dot(hn, wup_ref[:, dff + lo:dff + lo + FFN_COL_TILE],
                        preferred_element_type=F32))

    n_col = dff // FFN_COL_TILE
    n_seq = q_ref.shape[0]
    acc = None
    nxt = up(0)
    for c in range(n_col):
        cs = slice(c * FFN_COL_TILE, (c + 1) * FFN_COL_TILE)
        g, u = nxt
        if c + 1 < n_col:
            nxt = up(c + 1)
        _sample_attend(q_ref, k_ref, v_ref, ao_ref,
                       range(c * n_seq // n_col, (c + 1) * n_seq // n_col), XA_HEADS)
        buf[SUBLANES:SUBLANES + tt, cs] = g
        conv = cb_ref[layer:layer + 1, cs] + cw_ref[hist:kw, cs] * g
        for k in range(hist):
            off = SUBLANES - hist + k
            conv = conv + cw_ref[k:k + 1, cs] * buf[off:off + tt, cs]
        act = (_gelu(conv) * u).astype(BF16)
        part = jnp.dot(act, wdn_ref[cs, :], preferred_element_type=F32)
        acc = part if acc is None else acc + part
    tail = buf[SUBLANES + tt - hist:SUBLANES + tt, :]
    so_ref[...] = tail
    buf[SUBLANES - hist:SUBLANES, :] = tail

    xn = x + acc
    if final:
        xn = _rms(xn, nf_ref[...])
    xo_ref[...] = xn


def _s_lru_kernel(x_ref, h0_ref, cst_ref, nm_ref, win_ref, cw_ref, cb_ref, wa_ref, ba_ref,
                  wx_ref, bx_ref, lam_ref, wout_ref, nxa_ref, wq_ref,
                  xo_ref, q_ref, ho_ref, co_ref, *, layer, j):
    d = x_ref.shape[-1]
    kw = cw_ref.shape[0]
    hist = kw - 1
    x = x_ref[...]
    hn = _rms(x, nm_ref[layer:layer + 1, :]).astype(BF16)
    gate = jnp.dot(hn, win_ref[:, :d], preferred_element_type=F32)
    rec = jnp.dot(hn, win_ref[:, d:], preferred_element_type=F32)

    conv = cb_ref[j:j + 1, :] + cw_ref[hist:kw, :] * rec
    for k in range(hist):
        conv = conv + cw_ref[k:k + 1, :] * cst_ref[k]
    for k in range(hist - 1):
        co_ref[k] = cst_ref[k + 1]
    co_ref[hist - 1] = rec

    a, u = _lru_gates(conv, wa_ref, ba_ref[j:j + 1, :], wx_ref, bx_ref[j:j + 1, :],
                      lam_ref[j:j + 1, :])
    h = a * h0_ref[...] + u
    ho_ref[...] = h
    x1 = x + _mm(_gelu(gate) * h, wout_ref[...])
    xo_ref[...] = x1
    _store_packed_heads(q_ref, _mm(_rms(x1, nxa_ref[layer:layer + 1, :]), wq_ref[...]), XA_HEADS)


def _s_ffn_kernel(x_ref, o_ref, wo_ref, nm_ref, wup_ref, cw_ref, cb_ref, wdn_ref, st_ref, nf_ref,
                  xo_ref, so_ref, *, layer, final):
    dff = wdn_ref.shape[0]
    kw = cw_ref.shape[0]
    hist = kw - 1
    x = x_ref[...]
    x = x + _mm(_load_packed_heads(o_ref, x.shape[0], XA_HEADS), wo_ref[...])
    hn = _rms(x, nm_ref[layer:layer + 1, :]).astype(BF16)
    g = jnp.dot(hn, wup_ref[:, :dff], preferred_element_type=F32)
    u = jnp.dot(hn, wup_ref[:, dff:], preferred_element_type=F32)
    conv = cb_ref[layer:layer + 1, :] + cw_ref[hist:kw, :] * g
    for k in range(hist):
        conv = conv + cw_ref[k:k + 1, :] * st_ref[:, k, :]
    for k in range(hist - 1):
        so_ref[:, k, :] = st_ref[:, k + 1, :]
    so_ref[:, hist - 1, :] = g
    xn = x + _mm(_gelu(conv) * u, wdn_ref[...])
    if final:
        xn = _rms(xn, nf_ref[...])
    xo_ref[...] = xn


def _s_cmod_pre_kernel(x_ref, nm_ref, pw1_ref, b1_ref, g_ref, *, layer, j):
    d = x_ref.shape[-1]
    hn = _rms(x_ref[...], nm_ref[layer:layer + 1, :])
    g_ref[...] = _glu(_mm(hn, pw1_ref[...]) + b1_ref[j:j + 1, :], d)


def _s_cmod_conv_kernel(st_ref, g_ref, dw_ref, dwb_ref, lng_ref, lnb_ref, c_ref, so_ref, *, j):
    hist = st_ref.shape[0]
    kw = hist + 1
    g = g_ref[...]
    c = dwb_ref[j:j + 1, :] + dw_ref[hist:kw, :] * g
    for k in range(hist):
        c = c + dw_ref[k:k + 1, :] * st_ref[k]
    for k in range(hist - 1):
        so_ref[k] = st_ref[k + 1]
    so_ref[hist - 1] = g
    c_ref[...] = jax.nn.silu(_layer_norm(c, lng_ref[j:j + 1, :], lnb_ref[j:j + 1, :]))


def _s_cmod_post_kernel(x_ref, c_ref, pw2_ref, nxa_ref, wq_ref, xo_ref, q_ref, *, layer):
    x1 = x_ref[...] + _mm(c_ref[...], pw2_ref[...])
    xo_ref[...] = x1
    _store_packed_heads(q_ref, _mm(_rms(x1, nxa_ref[layer:layer + 1, :]), wq_ref[...]), XA_HEADS)


def _params(n_grid):
    return pltpu.CompilerParams(dimension_semantics=("arbitrary",) * n_grid,
                                vmem_limit_bytes=VMEM_LIMIT)


def _resident(shape, lead=None):
    if lead is None:
        nd = len(shape)
        return pl.BlockSpec(shape, lambda *_: (0,) * nd, pipeline_mode=pl.Buffered(1))
    nd = len(shape) - 1
    return pl.BlockSpec((None,) + tuple(shape[1:]), lambda *_: (lead,) + (0,) * nd,
                        pipeline_mode=pl.Buffered(1))


_whole = _resident


def _whole_out(shape):
    nd = len(shape)
    return pl.BlockSpec(shape, lambda *_: (0,) * nd)


def _sds(shape, dtype=F32):
    return jax.ShapeDtypeStruct(shape, dtype)


def _flat_cast_blocks(to_cast, steps, step_index):
    flat = [a.reshape(-1, a.shape[-1]) for a in to_cast]
    for a in flat:
        assert a.shape[0] % (steps * BF16_SUBLANES) == 0
    specs = [pl.BlockSpec((a.shape[0] // steps, a.shape[1]), lambda *g: (step_index(*g), 0))
             for a in flat]
    return flat, specs


def _prompt_memkv(mem, mem_norm3, wkv, heads, to_cast):
    depth = wkv.shape[0]
    b, m, d = mem.shape
    assert d % LANES == 0
    packed = d // LANES
    flat, cast_specs = _flat_cast_blocks(to_cast, depth * b, lambda i, n: i * b + n)
    outs = pl.pallas_call(
        functools.partial(_p_memkv_kernel, n_cast=len(flat), heads=heads),
        grid=(depth, b),
        in_specs=[pl.BlockSpec((None, m, d), lambda i, n: (n, 0, 0)),
                  pl.BlockSpec((None, 1, d), lambda i, n: (i, 0, 0)),
                  pl.BlockSpec((None, d, 2 * d), lambda i, n: (i, 0, 0))] + cast_specs,
        out_specs=[pl.BlockSpec((None, None, m * packed, LANES), lambda i, n: (i, n, 0, 0)),
                   pl.BlockSpec((None, None, m * packed, LANES), lambda i, n: (i, n, 0, 0)),
                   pl.BlockSpec((None, None, d, m), lambda i, n: (i, n, 0, 0)),
                   pl.BlockSpec((None, None, m, d), lambda i, n: (i, n, 0, 0))] + cast_specs,
        out_shape=[_sds((depth, b, m * packed, LANES)), _sds((depth, b, m * packed, LANES)),
                   _sds((depth, b, d, m), BF16), _sds((depth, b, m, d), BF16)]
                  + [_sds(a.shape, BF16) for a in flat],
        scratch_shapes=[pltpu.VMEM((d, 2 * d), BF16)],
        compiler_params=_params(2),
        name="prompt_memkv",
    )(mem, mem_norm3, wkv, *flat)
    k5, v5 = [_unpack_heads(o.reshape(depth, b, m, packed, LANES), heads) for o in outs[:2]]
    return (k5, v5, outs[2], outs[3]), [o.reshape(a.shape) for o, a in zip(outs[4:], to_cast)]


def _x_spec(tt, d):
    return pl.BlockSpec((None, tt, d), lambda n, t: (n, t, 0))


def _state_spec(rows, cols):
    return pl.BlockSpec((None, rows, cols), lambda n, t: (n, 0, 0))


def _xt_spec(b, tr, d):
    return pl.BlockSpec((b, tr, d), lambda t: (0, t, 0))


def _steps_to_sequences(s, b):
    hrows, d = s.shape
    return s.reshape(hrows // b, b, d).transpose(1, 0, 2)


def _prompt_lru(x, layer, j, w, to_cast):
    b, t, d = x.shape
    tr = LRU_TIME_ROWS
    hist = w['lru_conv_w'].shape[1] - 1
    assert b == SUBLANES and tr % SUBLANES == 0 and tr >= hist and d % LANES == 0
    nl, hrows = d // LANES, hist * b
    steps = t // tr
    flat, cast_specs = _flat_cast_blocks(to_cast, steps, lambda s: s)
    ins = [(x, _xt_spec(b, tr, d)),
           (w['norm_mix'], _resident(w['norm_mix'].shape)),
           (w['lru_w_in'], _resident(w['lru_w_in'].shape, j)),
           (w['lru_conv_w'], _resident(w['lru_conv_w'].shape, j)),
           (w['lru_conv_b'], _resident(w['lru_conv_b'].shape)),
           (w['lru_wa'], _resident(w['lru_wa'].shape, j)),
           (w['lru_ba'], _resident(w['lru_ba'].shape)),
           (w['lru_wx'], _resident(w['lru_wx'].shape, j)),
           (w['lru_bx'], _resident(w['lru_bx'].shape)),
           (w['lru_lambda'], _resident(w['lru_lambda'].shape)),
           (w['lru_w_out'], _resident(w['lru_w_out'].shape, j))]
    ins += list(zip(flat, cast_specs))
    outs = pl.pallas_call(
        functools.partial(_p_lru_kernel, layer=layer, j=j, n_cast=len(flat)),
        grid=(steps,),
        in_specs=[s for _, s in ins],
        out_specs=[_xt_spec(b, tr, d), _whole_out((b, d)), _whole_out((hrows, d))] + cast_specs,
        out_shape=([_sds((b, t, d)), _sds((b, d)), _sds((hrows, d))]
                   + [_sds(a.shape, BF16) for a in flat]),
        scratch_shapes=[pltpu.VMEM((nl, hrows + b * tr, LANES), F32),
                        pltpu.VMEM((nl, b * tr, LANES), F32),
                        pltpu.VMEM((b, d), F32)],
        compiler_params=_params(1),
        name="prompt_lru",
    )(*[a for a, _ in ins])
    casted = [o.reshape(a.shape) for o, a in zip(outs[3:], to_cast)]
    return outs[0], outs[1], outs[2], casted


def _prompt_cmod(x, layer, j, w):
    b, t, d = x.shape
    tr = CMOD_TIME_ROWS
    hist = w['cm_dw_w'].shape[1] - 1
    assert b == SUBLANES and tr % SUBLANES == 0 and tr >= hist and d % CMOD_COL_TILE == 0
    nl, hrows = d // LANES, hist * b
    ins = [(x, _xt_spec(b, tr, d)),
           (w['norm_mix'], _resident(w['norm_mix'].shape)),
           (w['cm_w_pw1'], _resident(w['cm_w_pw1'].shape, j)),
           (w['cm_b_pw1'], _resident(w['cm_b_pw1'].shape)),
           (w['cm_dw_w'], _resident(w['cm_dw_w'].shape, j)),
           (w['cm_dw_b'], _resident(w['cm_dw_b'].shape)),
           (w['cm_ln_g'], _resident(w['cm_ln_g'].shape)),
           (w['cm_ln_b'], _resident(w['cm_ln_b'].shape)),
           (w['cm_w_pw2'], _resident(w['cm_w_pw2'].shape, j))]
    return pl.pallas_call(
        functools.partial(_p_cmod_kernel, layer=layer, j=j),
        grid=(t // tr,),
        in_specs=[s for _, s in ins],
        out_specs=[_xt_spec(b, tr, d), _whole_out((hrows, d))],
        out_shape=[_sds((b, t, d)), _sds((hrows, d))],
        scratch_shapes=[pltpu.VMEM((nl, hrows + b * tr, LANES), F32),
                        pltpu.VMEM((nl, b * tr, LANES), F32)],
        compiler_params=_params(1),
        name="prompt_cmod",
    )(*[a for a, _ in ins])


def _prompt_xattn(x, layer, kt, vb, w):
    b, t, d = x.shape
    tt = XATTN_T_TILE
    m = vb.shape[2]
    ins = [(x, _x_spec(tt, d)),
           (w['norm_xa'], _resident(w['norm_xa'].shape)),
           (w['xa_w_q'], _resident(w['xa_w_q'].shape, layer)),
           (kt, pl.BlockSpec((None, None, d, m), lambda n, t_: (layer, n, 0, 0))),
           (vb, pl.BlockSpec((None, None, m, d), lambda n, t_: (layer, n, 0, 0))),
           (w['xa_w_o'], _resident(w['xa_w_o'].shape, layer))]
    return pl.pallas_call(
        functools.partial(_p_xattn_kernel, layer=layer),
        grid=(b, t // tt),
        in_specs=[s for _, s in ins],
        out_specs=_x_spec(tt, d),
        out_shape=_sds((b, t, d)),
        compiler_params=_params(2),
        name="prompt_xattn",
    )(*[a for a, _ in ins])


def _prompt_ffn_sample_attn(x, layer, final, q, cache_k, cache_v, w):
    b, t, d = x.shape
    tt = FFN_T_TILE
    n_t = t // tt
    dff = w['ffn_w_down'].shape[1]
    hist = w['ffn_conv_w'].shape[1] - 1
    r = q.shape[0] // SUBLANES
    _, _, m, nh, dh = cache_k.shape
    assert r % (b * n_t) == 0 and dh % LANES == 0 and nh * (dh // LANES) == SUBLANES
    nb = r // (b * n_t)
    kv_spec = pl.BlockSpec((None, nb, m, SUBLANES, LANES),
                           lambda n, s: (layer, n * n_t + s, 0, 0, 0))
    row_spec = pl.BlockSpec((nb, SUBLANES, LANES), lambda n, s: (n * n_t + s, 0, 0))
    ins = [(x, _x_spec(tt, d)),
           (w['norm_ffn'], _resident(w['norm_ffn'].shape)),
           (w['ffn_w_up'], _resident(w['ffn_w_up'].shape, layer)),
           (w['ffn_conv_w'], _resident(w['ffn_conv_w'].shape, layer)),
           (w['ffn_conv_b'], _resident(w['ffn_conv_b'].shape)),
           (w['ffn_w_down'], _resident(w['ffn_w_down'].shape, layer)),
           (w['norm_final'], _resident(w['norm_final'].shape)),
           (q.reshape(r, SUBLANES, LANES), row_spec),
           (_pack_heads(cache_k), kv_spec), (_pack_heads(cache_v), kv_spec)]
    xo, fbuf, o = pl.pallas_call(
        functools.partial(_p_ffn_kernel, layer=layer, final=final),
        grid=(b, n_t),
        in_specs=[s for _, s in ins],
        out_specs=[_x_spec(tt, d), _state_spec(hist, dff), row_spec],
        out_shape=[_sds((b, t, d)), _sds((b, hist, dff)), _sds((r, SUBLANES, LANES))],
        scratch_shapes=[pltpu.VMEM((SUBLANES + tt, dff), F32)],
        compiler_params=pltpu.CompilerParams(dimension_semantics=("arbitrary", "arbitrary"),
                                             vmem_limit_bytes=FFN_VMEM_LIMIT),
        name="prompt_ffn_sample_attn",
    )(*[a for a, _ in ins])
    return xo, fbuf, o.reshape(r * SUBLANES, LANES)


def _pack_heads(a):
    *lead, nh, dh = a.shape
    a = a.reshape(*lead, nh, dh // LANES, LANES).swapaxes(-3, -2)
    return a.reshape(*lead, (dh // LANES) * nh, LANES)


def _unpack_heads(a, nh):
    *lead, rows, lanes = a.shape
    a = a.reshape(*lead, rows // nh, nh, lanes).swapaxes(-3, -2)
    return a.reshape(*lead, nh, (rows // nh) * lanes)


def _sample_lru(x, h0, cst, layer, j, w):
    r, d = x.shape
    ins = [(x, _whole(x.shape)), (h0, _whole(h0.shape, j)), (cst, _whole(cst.shape, j)),
           (w['norm_mix'], _whole(w['norm_mix'].shape)),
           (w['lru_w_in'], _whole(w['lru_w_in'].shape, j)),
           (w['lru_conv_w'], _whole(w['lru_conv_w'].shape, j)),
           (w['lru_conv_b'], _whole(w['lru_conv_b'].shape)),
           (w['lru_wa'], _whole(w['lru_wa'].shape, j)),
           (w['lru_ba'], _whole(w['lru_ba'].shape)),
           (w['lru_wx'], _whole(w['lru_wx'].shape, j)),
           (w['lru_bx'], _whole(w['lru_bx'].shape)),
           (w['lru_lambda'], _whole(w['lru_lambda'].shape)),
           (w['lru_w_out'], _whole(w['lru_w_out'].shape, j)),
           (w['norm_xa'], _whole(w['norm_xa'].shape)),
           (w['xa_w_q'], _whole(w['xa_w_q'].shape, layer))]
    return pl.pallas_call(
        functools.partial(_s_lru_kernel, layer=layer, j=j),
        in_specs=[s for _, s in ins],
        out_specs=[_whole_out((r, d)), _whole_out((r * d // LANES, LANES)), _whole_out((r, d)),
                   _whole_out(cst.shape[1:])],
        out_shape=[_sds((r, d)), _sds((r * d // LANES, LANES)), _sds((r, d)),
                   _sds(cst.shape[1:])],
        grid=(1,),
        compiler_params=_params(1),
        name="sample_lru",
    )(*[a for a, _ in ins])


def _sample_ffn(x, o, st, layer, final, w):
    r, d = x.shape
    ins = [(x, _whole(x.shape)), (o, _whole(o.shape)),
           (w['xa_w_o'], _whole(w['xa_w_o'].shape, layer)),
           (w['norm_ffn'], _whole(w['norm_ffn'].shape)),
           (w['ffn_w_up'], _whole(w['ffn_w_up'].shape, layer)),
           (w['ffn_conv_w'], _whole(w['ffn_conv_w'].shape, layer)),
           (w['ffn_conv_b'], _whole(w['ffn_conv_b'].shape)),
           (w['ffn_w_down'], _whole(w['ffn_w_down'].shape, layer)),
           (st, _whole(st.shape, layer)),
           (w['norm_final'], _whole(w['norm_final'].shape))]
    return pl.pallas_call(
        functools.partial(_s_ffn_kernel, layer=layer, final=final),
        in_specs=[s for _, s in ins],
        out_specs=[_whole_out((r, d)), _whole_out(st.shape[1:])],
        out_shape=[_sds((r, d)), _sds(st.shape[1:])],
        grid=(1,),
        compiler_params=_params(1),
        name="sample_ffn",
    )(*[a for a, _ in ins])


def _sample_cmod(x, st, layer, j, w):
    r, d = x.shape
    ins = [(x, _whole(x.shape)),
           (w['norm_mix'], _whole(w['norm_mix'].shape)),
           (w['cm_w_pw1'], _whole(w['cm_w_pw1'].shape, j)),
           (w['cm_b_pw1'], _whole(w['cm_b_pw1'].shape))]
    g = pl.pallas_call(
        functools.partial(_s_cmod_pre_kernel, layer=layer, j=j),
        in_specs=[s for _, s in ins],
        out_specs=_whole_out((r, d)),
        out_shape=_sds((r, d)),
        grid=(1,),
        compiler_params=_params(1),
        name="sample_cmod_pre",
    )(*[a for a, _ in ins])

    rows = SAMPLE_CONV_ROWS
    hist = st.shape[1]
    ins = [(st, pl.BlockSpec((None, hist, rows, d), lambda n: (j, 0, n, 0))),
           (g, pl.BlockSpec((rows, d), lambda n: (n, 0))),
           (w['cm_dw_w'], _resident(w['cm_dw_w'].shape, j)),
           (w['cm_dw_b'], _resident(w['cm_dw_b'].shape)),
           (w['cm_ln_g'], _resident(w['cm_ln_g'].shape)),
           (w['cm_ln_b'], _resident(w['cm_ln_b'].shape))]
    c, new_st = pl.pallas_call(
        functools.partial(_s_cmod_conv_kernel, j=j),
        grid=(r // rows,),
        in_specs=[s for _, s in ins],
        out_specs=[pl.BlockSpec((rows, d), lambda n: (n, 0)),
                   pl.BlockSpec((hist, rows, d), lambda n: (0, n, 0))],
        out_shape=[_sds((r, d)), _sds((hist, r, d))],
        compiler_params=_params(1),
        name="sample_cmod_conv",
    )(*[a for a, _ in ins])

    ins = [(x, _whole(x.shape)), (c, _whole(c.shape)),
           (w['cm_w_pw2'], _whole(w['cm_w_pw2'].shape, j)),
           (w['norm_xa'], _whole(w['norm_xa'].shape)),
           (w['xa_w_q'], _whole(w['xa_w_q'].shape, layer))]
    x1, q = pl.pallas_call(
        functools.partial(_s_cmod_post_kernel, layer=layer),
        in_specs=[s for _, s in ins],
        out_specs=[_whole_out((r, d)), _whole_out((r * d // LANES, LANES))],
        out_shape=[_sds((r, d)), _sds((r * d // LANES, LANES))],
        grid=(1,),
        compiler_params=_params(1),
        name="sample_cmod_post",
    )(*[a for a, _ in ins])
    return x1, q, new_st


_MIXER0_MATMUL_WEIGHTS = ('lru_w_in', 'lru_wa', 'lru_wx', 'lru_w_out')
_LATE_MATMUL_WEIGHTS = ('cm_w_pw1', 'cm_w_pw2', 'xa_w_q', 'xa_w_o', 'ffn_w_up', 'ffn_w_down')


def kernel(x_prompt, x_sample, state_lru_h, state_lru_conv, state_cmod_conv, state_ffn_conv, cache_mem_k, cache_mem_v, mem_prompt, norm_mix, norm_xa, norm_ffn, norm_final, lru_w_in, lru_conv_w, lru_conv_b, lru_wa, lru_ba, lru_wx, lru_bx, lru_lambda, lru_w_out, cm_w_pw1, cm_b_pw1, cm_dw_w, cm_dw_b, cm_ln_g, cm_ln_b, cm_w_pw2, mem_norm, xa_w_q, xa_w_kv, xa_w_o, ffn_w_up, ffn_conv_w, ffn_conv_b, ffn_w_down):
    w = dict(norm_mix=norm_mix, norm_xa=norm_xa, norm_ffn=norm_ffn,
             norm_final=norm_final.reshape(1, -1),
             lru_w_in=lru_w_in, lru_conv_w=lru_conv_w, lru_conv_b=lru_conv_b, lru_wa=lru_wa,
             lru_ba=lru_ba, lru_wx=lru_wx, lru_bx=lru_bx, lru_lambda=lru_lambda,
             lru_w_out=lru_w_out, cm_w_pw1=cm_w_pw1, cm_b_pw1=cm_b_pw1, cm_dw_w=cm_dw_w,
             cm_dw_b=cm_dw_b, cm_ln_g=cm_ln_g, cm_ln_b=cm_ln_b, cm_w_pw2=cm_w_pw2,
             xa_w_q=xa_w_q, xa_w_kv=xa_w_kv, xa_w_o=xa_w_o,
             ffn_w_up=ffn_w_up, ffn_conv_w=ffn_conv_w, ffn_conv_b=ffn_conv_b,
             ffn_w_down=ffn_w_down)

    depth = norm_mix.shape[0]
    bsz, _, d = x_prompt.shape
    n_mem = mem_prompt.shape[1]
    heads = cache_mem_k.shape[3]

    (k_p, v_p, kt_p, vb_p), casted = _prompt_memkv(
        mem_prompt, mem_norm.reshape(depth, 1, d), w['xa_w_kv'], heads,
        [w[n] for n in _MIXER0_MATMUL_WEIGHTS])
    w.update(zip(_MIXER0_MATMUL_WEIGHTS, casted))
    x = x_prompt
    r = x_sample.shape[0]
    xs = x_sample.reshape(r, d)
    lconv = state_lru_conv.transpose(0, 2, 1, 3)
    cconv = state_cmod_conv.transpose(0, 2, 1, 3)
    p_h, p_lconv, p_cconv, p_fconv = [], [], [], []
    s_h, s_lconv, s_cconv, s_fconv = [], [], [], []
    for i in range(depth):
        j = i // 2
        if i % 2 == 0:
            late = [n for n in _LATE_MATMUL_WEIGHTS if w[n].dtype != BF16]
            x, h_last, cbuf, casted = _prompt_lru(x, i, j, w, [w[n] for n in late])
            w.update(zip(late, casted))
            p_h.append(h_last)
            p_lconv.append(_steps_to_sequences(cbuf, bsz))
            xs, q, h_new, c_new = _sample_lru(xs, state_lru_h, lconv, i, j, w)
            s_h.append(h_new)
            s_lconv.append(c_new.transpose(1, 0, 2))
        else:
            x, cbuf = _prompt_cmod(x, i, j, w)
            p_cconv.append(_steps_to_sequences(cbuf, bsz))
            xs, q, c_new = _sample_cmod(xs, cconv, i, j, w)
            s_cconv.append(c_new.transpose(1, 0, 2))
        x = _prompt_xattn(x, i, kt_p, vb_p, w)
        x, fbuf, o = _prompt_ffn_sample_attn(x, i, i == depth - 1, q, cache_mem_k, cache_mem_v, w)
        p_fconv.append(fbuf)
        xs, f_new = _sample_ffn(xs, o, state_ffn_conv, i, i == depth - 1, w)
        s_fconv.append(f_new)
    y_prompt = x
    y_sample = xs.reshape(x_sample.shape)

    return (y_prompt, y_sample,
            jnp.stack(p_h), jnp.stack(p_lconv), jnp.stack(p_cconv), jnp.stack(p_fconv),
            k_p, v_p,
            jnp.stack(s_h), jnp.stack(s_lconv), jnp.stack(s_cconv), jnp.stack(s_fconv))
```
